```python
import jax, jax.numpy as jnp
from jax import lax
import numpy as np

D_MODEL = 1024
BATCH = 4
SEQ = 4096
DEPTH = 4
DEC_BATCH = 32
DEC_SEQ = 4
PAST_LEN = 8192
PAGE_SIZE = 128

HEAD_DIM = 64
SCALE = HEAD_DIM ** -0.5
ROPE_THETA = 10000.0
EPS = 1e-6
TINY = 1e-30
N_MIXERS = 2
N_A_LAYERS = (DEPTH + 1) // 2
N_B_LAYERS = DEPTH // 2
DIL_PATTERNS = ((128, 1), (512, 4), (2048, 16))
N_GROUPS_A = len(DIL_PATTERNS)
HEADS_A = 8
WIDTH_A = HEADS_A * HEAD_DIM
QKV_A = 3 * N_GROUPS_A * WIDTH_A
IN_A = QKV_A + WIDTH_A
HEADS_B = D_MODEL // HEAD_DIM
KV_HEADS_B = 4
GQA_B = HEADS_B // KV_HEADS_B
WIDTH_B = HEADS_B * HEAD_DIM
KVW_B = KV_HEADS_B * HEAD_DIM
N_KV_B = 6
CMP_LEN = 32
CMP_STRIDE = 16
CMP_RATIO = CMP_LEN // CMP_STRIDE
CMP_HIDDEN = 128
SEL_BLOCK = 64
SEL_TOPN = 16
WIN_B = 512
IN_B = WIDTH_B + N_KV_B * KVW_B + 3 * HEADS_B + WIDTH_B
BAND_QB = 128
SEL_QB = 32

kernel_name = 'hybrid_dilated_nsa_decoder_step'


def rms_norm(x, g):
    xf = x.astype(jnp.float32)
    y = xf * lax.rsqrt(jnp.mean(xf * xf, axis=-1, keepdims=True) + EPS)
    return (y * g.astype(jnp.float32)).astype(x.dtype)


def rope(x, pos):
    half = HEAD_DIM // 2
    inv_freq = ROPE_THETA ** (-jnp.arange(half, dtype=jnp.float32) / half)
    ang = pos.astype(jnp.float32)[:, None] * inv_freq[None, :]
    shape = (pos.shape[0],) + (1,) * (x.ndim - 3) + (half,)
    cos = jnp.cos(ang).reshape(shape)
    sin = jnp.sin(ang).reshape(shape)
    xf = x.astype(jnp.float32)
    x1, x2 = xf[..., :half], xf[..., half:]
    return jnp.concatenate([x1 * cos - x2 * sin, x2 * cos + x1 * sin], axis=-1).astype(x.dtype)


def masked_attend(s, mask, v, eqn):
    s = jnp.where(mask, s, -jnp.inf)
    m = jnp.max(s, axis=-1, keepdims=True)
    m = jnp.where(jnp.isfinite(m), m, 0.0)
    e = jnp.exp(s - m)
    den = jnp.sum(e, axis=-1, keepdims=True)
    p = e / jnp.maximum(den, TINY)
    out = jnp.einsum(eqn, p.astype(v.dtype), v)
    lse = (m + jnp.log(den))[..., 0]
    return out, lse, p


def band_attention(q, k, v, kw):
    N, n, H, Dh = q.shape
    KH = k.shape[2]
    G = H // KH
    qb = min(BAND_QB, n)
    nb = -(-n // qb)
    pad_end = nb * qb - n
    qp = jnp.pad(q, ((0, 0), (0, pad_end), (0, 0), (0, 0))).reshape(N, nb, qb, KH, G, Dh)
    kp = jnp.pad(k, ((0, 0), (kw, pad_end), (0, 0), (0, 0)))
    vp = jnp.pad(v, ((0, 0), (kw, pad_end), (0, 0), (0, 0)))
    qi = jnp.arange(qb)[:, None]
    kj = jnp.arange(qb + kw)[None, :]
    dist = qi + kw - kj
    band = (dist >= 0) & (dist <= kw)

    def one_block(b):
        qblk = lax.dynamic_index_in_dim(qp, b, axis=1, keepdims=False)
        kb = lax.dynamic_slice_in_dim(kp, b * qb, qb + kw, axis=1)
        vb = lax.dynamic_slice_in_dim(vp, b * qb, qb + kw, axis=1)
        s = jnp.einsum('nqkgd,nskd->nkgqs', qblk, kb, preferred_element_type=jnp.float32) * SCALE
        valid = band & (b * qb + kj - kw >= 0)
        o, lse, _ = masked_attend(s, valid, vb, 'nkgqs,nskd->nqkgd')
        return o, lse

    o, lse = lax.map(one_block, jnp.arange(nb))
    o = jnp.moveaxis(o, 0, 1).reshape(N, nb * qb, H, Dh)[:, :n]
    lse = jnp.transpose(lse, (1, 0, 4, 2, 3)).reshape(N, nb * qb, H)[:, :n]
    return o, lse


def fold_residues(t, d):
    B, S, H, Dh = t.shape
    return t.reshape(B, S // d, d, H, Dh).transpose(0, 2, 1, 3, 4).reshape(B * d, S // d, H, Dh)


def unfold_residues(t, B, d):
    n = t.shape[1]
    rest = t.shape[2:]
    t = jnp.moveaxis(t.reshape((B, d, n) + rest), 1, 2)
    return t.reshape((B, n * d) + rest)


def project_a(h, w_in, q_gain, k_gain, pos):
    N, T, _ = h.shape
    u = jnp.einsum('ntd,de->nte', h, w_in)
    qkv = u[..., :QKV_A].reshape(N, T, 3, N_GROUPS_A, HEADS_A, HEAD_DIM)
    z = u[..., QKV_A:]
    q = rope(rms_norm(qkv[:, :, 0], q_gain), pos)
    k = rope(rms_norm(qkv[:, :, 1], k_gain), pos)
    return q, k, qkv[:, :, 2], z


def dilated_output(outs, lses, z, w_out):
    wts = jax.nn.softmax(jnp.stack(lses, 0), axis=0)
    o = jnp.einsum('gnth,gnthd->nthd', wts.astype(outs[0].dtype), jnp.stack(outs, 0))
    N, T = o.shape[:2]
    return jnp.einsum('nte,ed->ntd', o.reshape(N, T, WIDTH_A) * jax.nn.silu(z), w_out)


def layer_a_prompt(x, norm_g, w_in, q_gain, k_gain, w_out):
    B, S, _ = x.shape
    q, k, v, z = project_a(rms_norm(x, norm_g), w_in, q_gain, k_gain, jnp.arange(S))
    outs, lses, states = [], [], []
    for g, (win, dil) in enumerate(DIL_PATTERNS):
        o, lse = band_attention(fold_residues(q[:, :, g], dil), fold_residues(k[:, :, g], dil),
                                fold_residues(v[:, :, g], dil), win // dil)
        outs.append(unfold_residues(o, B, dil))
        lses.append(unfold_residues(lse, B, dil))
        states.append(jnp.stack([k[:, :, g], v[:, :, g]], axis=2)[:, -min(win, S):])
    return x + dilated_output(outs, lses, z, w_out), states


def layer_a_sample(x, bufs, past_len, norm_g, w_in, q_gain, k_gain, w_out):
    N, T, _ = x.shape
    q, k, v, z = project_a(rms_norm(x, norm_g), w_in, q_gain, k_gain, past_len + jnp.arange(T))
    outs, lses, states = [], [], []
    for g, (win, dil) in enumerate(DIL_PATTERNS):
        L = bufs[g].shape[1]
        cat = jnp.concatenate([bufs[g], jnp.stack([k[:, :, g], v[:, :, g]], axis=2)], axis=1)
        idx = L + jnp.arange(T)[:, None] - dil * jnp.arange(win // dil + 1)[None, :]
        idc = jnp.maximum(idx, 0)
        kg = cat[:, :, 0][:, idc]
        vg = cat[:, :, 1][:, idc]
        s = jnp.einsum('nthd,ntkhd->nhtk', q[:, :, g], kg, preferred_element_type=jnp.float32) * SCALE
        o, lse, _ = masked_attend(s, (idx >= 0)[None, None], vg, 'nhtk,ntkhd->nthd')
        outs.append(o)
        lses.append(jnp.transpose(lse, (0, 2, 1)))
        states.append(cat[:, -min(win, L + T):])
    return x + dilated_output(outs, lses, z, w_out), states


def project_b(h, w_in, q_gain, k_gain, pos):
    N, T, _ = h.shape
    u = jnp.einsum('ntd,de->nte', h, w_in)
    q = rms_norm(u[..., :WIDTH_B].reshape(N, T, HEADS_B, HEAD_DIM), q_gain)
    o1 = WIDTH_B + N_KV_B * KVW_B
    kv = u[..., WIDTH_B:o1].reshape(N, T, N_KV_B, KV_HEADS_B, HEAD_DIM)
    gates = jax.nn.sigmoid(u[..., o1:o1 + 3 * HEADS_B].reshape(N, T, 3, HEADS_B))
    z = u[..., o1 + 3 * HEADS_B:]
    q_rot = rope(q, pos)
    k_cmp = rms_norm(kv[:, :, 0], k_gain)
    k_sel = rope(rms_norm(kv[:, :, 2], k_gain), pos)
    k_win = rope(rms_norm(kv[:, :, 4], k_gain), pos)
    rows = jnp.stack([k_cmp, kv[:, :, 1], k_sel, kv[:, :, 3]], axis=2)
    win_rows = jnp.stack([k_win, kv[:, :, 5]], axis=2)
    return q, q_rot, rows, win_rows, gates, z


def compress_rows(x, pe, w1, w2):
    N, Tk = x.shape[:2]
    n_cmp = (Tk - CMP_LEN) // CMP_STRIDE + 1
    n_chunk = n_cmp + CMP_RATIO - 1
    chunks = x[:, :n_chunk * CMP_STRIDE].reshape(N, n_chunk, CMP_STRIDE, KV_HEADS_B, HEAD_DIM)
    w1r = w1.reshape(CMP_RATIO, CMP_STRIDE, HEAD_DIM, CMP_HIDDEN)
    proj = jnp.einsum('ncjkd,rjde->rncke', chunks, w1r)
    hid = jnp.einsum('jd,jde->e', pe, w1)
    for r in range(CMP_RATIO):
        hid = hid + proj[r, :, r:r + n_cmp]
    return jnp.einsum('ncke,ed->nckd', jax.nn.silu(hid), w2)


def nsa_cmp_sel(q_c, q_s, rows, q_pos, k_gain, pe, w1, w2):
    N, Tq = q_c.shape[:2]
    Tk = rows.shape[1]
    k_cmp = rms_norm(compress_rows(rows[:, :, 0], pe[0], w1[0], w2[0]), k_gain)
    v_cmp = compress_rows(rows[:, :, 1], pe[1], w1[1], w2[1])
    n_cmp = k_cmp.shape[1]
    cmp_start = jnp.arange(n_cmp) * CMP_STRIDE
    cmp_end = cmp_start + CMP_LEN - 1
    n_sel = -(-Tk // SEL_BLOCK)
    sel_start = jnp.arange(n_sel) * SEL_BLOCK
    cover = ((cmp_start[:, None] <= sel_start[None, :] + SEL_BLOCK - 1)
             & (cmp_end[:, None] >= sel_start[None, :])).astype(jnp.float32)
    pad_k = n_sel * SEL_BLOCK - Tk

    def blocks_of(t):
        t = jnp.pad(t, ((0, 0), (0, pad_k), (0, 0), (0, 0)))
        return t.reshape(N, n_sel, SEL_BLOCK, KV_HEADS_B, HEAD_DIM).transpose(0, 3, 1, 2, 4)

    k_blk = blocks_of(rows[:, :, 2])
    v_blk = blocks_of(rows[:, :, 3])
    top_n = min(SEL_TOPN, n_sel)
    qch = min(SEL_QB, Tq)
    nq = -(-Tq // qch)
    pad_q = nq * qch - Tq

    def chunks(t):
        t = jnp.pad(t, ((0, 0), (0, pad_q), (0, 0), (0, 0)))
        return t.reshape(N, nq, qch, KV_HEADS_B, GQA_B, HEAD_DIM)

    qc_all = chunks(q_c)
    qs_all = chunks(q_s)
    pos_all = jnp.pad(q_pos, (0, pad_q), mode='edge')
    n_ix = jnp.arange(N)[:, None, None, None]
    h_ix = jnp.arange(KV_HEADS_B)[None, :, None, None]
    blk_ids = jnp.arange(n_sel)
    in_blk = jnp.arange(SEL_BLOCK)

    def one_chunk(i):
        qc = lax.dynamic_index_in_dim(qc_all, i, axis=1, keepdims=False)
        qs = lax.dynamic_index_in_dim(qs_all, i, axis=1, keepdims=False)
        pos = lax.dynamic_slice_in_dim(pos_all, i * qch, qch)
        s = jnp.einsum('nqkgd,nckd->nkgqc', qc, k_cmp, preferred_element_type=jnp.float32) * SCALE
        o_c, _, p_c = masked_attend(s, cmp_end[None, :] <= pos[:, None], v_cmp, 'nkgqc,nckd->nqkgd')
        imp = jnp.einsum('nkgqc,cj->nkqj', p_c, cover)
        cur = (pos // SEL_BLOCK)[:, None]
        forced = (blk_ids == 0) | (blk_ids == cur) | (blk_ids == cur - 1)
        imp = jnp.where(forced, jnp.inf, jnp.where(blk_ids <= cur, imp, -jnp.inf))
        _, sel = lax.top_k(imp, top_n)
        kg = k_blk[n_ix, h_ix, sel]
        vg = v_blk[n_ix, h_ix, sel]
        kpos = sel[..., None] * SEL_BLOCK + in_blk
        s2 = jnp.einsum('nqkgd,nkqtbd->nkgqtb', qs, kg, preferred_element_type=jnp.float32) * SCALE
        s2 = s2.reshape(N, KV_HEADS_B, GQA_B, qch, top_n * SEL_BLOCK)
        m2 = (kpos <= pos[:, None, None]).reshape(N, KV_HEADS_B, 1, qch, top_n * SEL_BLOCK)
        vg = vg.reshape(N, KV_HEADS_B, qch, top_n * SEL_BLOCK, HEAD_DIM)
        o_s, _, _ = masked_attend(s2, m2, vg, 'nkgqs,nkqsd->nqkgd')
        return o_c, o_s

    o_c, o_s = lax.map(one_chunk, jnp.arange(nq))
    o_c = jnp.moveaxis(o_c, 0, 1).reshape(N, nq * qch, HEADS_B, HEAD_DIM)[:, :Tq]
    o_s = jnp.moveaxis(o_s, 0, 1).reshape(N, nq * qch, HEADS_B, HEAD_DIM)[:, :Tq]
    return o_c, o_s


def window_sample(q, rows, buf, win):
    N, T = q.shape[:2]
    L = buf.shape[1]
    cat = jnp.concatenate([buf, rows], axis=1)
    dist = (L + jnp.arange(T))[:, None] - jnp.arange(L + T)[None, :]
    qg = q.reshape(N, T, KV_HEADS_B, GQA_B, HEAD_DIM)
    s = jnp.einsum('ntkgd,nskd->nkgts', qg, cat[:, :, 0], preferred_element_type=jnp.float32) * SCALE
    o, _, _ = masked_attend(s, (dist >= 0) & (dist <= win), cat[:, :, 1], 'nkgts,nskd->ntkgd')
    return o.reshape(N, T, HEADS_B, HEAD_DIM), cat[:, -min(win, L + T):]


def nsa_output(o_c, o_s, o_w, gates, z, w_out):
    o = (gates[:, :, 0, :, None] * o_c + gates[:, :, 1, :, None] * o_s
         + gates[:, :, 2, :, None] * o_w)
    N, T = o.shape[:2]
    return jnp.einsum('nte,ed->ntd', o.reshape(N, T, WIDTH_B) * jax.nn.silu(z), w_out)


def layer_b_prompt(x, norm_g, w_in, q_gain, k_gain, pe, w1, w2, w_out):
    B, S, _ = x.shape
    pos = jnp.arange(S)
    q_c, q_r, rows, wrows, gates, z = project_b(rms_norm(x, norm_g), w_in, q_gain, k_gain, pos)
    o_c, o_s = nsa_cmp_sel(q_c, q_r, rows, pos, k_gain, pe, w1, w2)
    o_w, _ = band_attention(q_r, wrows[:, :, 0], wrows[:, :, 1], WIN_B)
    y = x + nsa_output(o_c, o_s, o_w, gates, z, w_out)
    return y, rows, wrows[:, -min(WIN_B, S):]


def layer_b_sample(x, pool, page_table, buf, norm_g, w_in, q_gain, k_gain, pe, w1, w2, w_out):
    N, T, _ = x.shape
    past_len = page_table.shape[1] * PAGE_SIZE
    pos = past_len + jnp.arange(T)
    q_c, q_r, rows, wrows, gates, z = project_b(rms_norm(x, norm_g), w_in, q_gain, k_gain, pos)
    past = pool[page_table].reshape((N, past_len) + pool.shape[2:])
    full = jnp.concatenate([past, rows], axis=1)
    o_c, o_s = nsa_cmp_sel(q_c, q_r, full, pos, k_gain, pe, w1, w2)
    o_w, new_buf = window_sample(q_r, wrows, buf, WIN_B)
    y = x + nsa_output(o_c, o_s, o_w, gates, z, w_out)
    return y, rows, new_buf


def setup_inputs(seed: int = 0) -> dict:
    key = jax.random.key(seed)
    ks = jax.random.split(key, 24)
    f32 = jnp.float32
    n_pages = PAST_LEN // PAGE_SIZE
    n_used = DEC_BATCH * n_pages
    n_pool = n_used + max(1, n_used // 4)

    def nrm(k, shape, scale=1.0):
        return jax.random.normal(k, shape, f32) * scale

    inp = {}
    inp['x_prompt'] = nrm(ks[0], (BATCH, SEQ, D_MODEL))
    inp['x_sample'] = nrm(ks[1], (DEC_BATCH, DEC_SEQ, D_MODEL))
    for g, (win, _) in enumerate(DIL_PATTERNS):
        inp['cache_dil_' + str(g)] = nrm(ks[2 + g], (N_A_LAYERS, DEC_BATCH, min(win, PAST_LEN), 2, HEADS_A, HEAD_DIM))
    inp['cache_nsa_paged'] = nrm(ks[5], (N_B_LAYERS, n_pool, PAGE_SIZE, 4, KV_HEADS_B, HEAD_DIM))
    inp['cache_nsa_win'] = nrm(ks[6], (N_B_LAYERS, DEC_BATCH, min(WIN_B, PAST_LEN), 2, KV_HEADS_B, HEAD_DIM))
    inp['page_table'] = jax.random.permutation(ks[7], n_pool)[:n_used].reshape(DEC_BATCH, n_pages).astype(jnp.int32)
    inp['a_norm'] = 1.0 + nrm(ks[8], (N_A_LAYERS, D_MODEL), 0.02)
    inp['a_w_in'] = nrm(ks[9], (N_A_LAYERS, D_MODEL, IN_A), D_MODEL ** -0.5)
    inp['a_q_norm'] = 1.0 + nrm(ks[10], (N_A_LAYERS, HEAD_DIM), 0.02)
    inp['a_k_norm'] = 1.0 + nrm(ks[11], (N_A_LAYERS, HEAD_DIM), 0.02)
    inp['a_w_out'] = nrm(ks[12], (N_A_LAYERS, WIDTH_A, D_MODEL), WIDTH_A ** -0.5)
    inp['b_norm'] = 1.0 + nrm(ks[13], (N_B_LAYERS, D_MODEL), 0.02)
    inp['b_w_in'] = nrm(ks[14], (N_B_LAYERS, D_MODEL, IN_B), D_MODEL ** -0.5)
    inp['b_q_norm'] = 1.0 + nrm(ks[15], (N_B_LAYERS, HEAD_DIM), 0.02)
    inp['b_k_norm'] = 1.0 + nrm(ks[16], (N_B_LAYERS, HEAD_DIM), 0.02)
    inp['b_cmp_pe'] = nrm(ks[17], (N_B_LAYERS, 2, CMP_LEN, HEAD_DIM), 0.1)
    inp['b_cmp_w1'] = nrm(ks[18], (N_B_LAYERS, 2, CMP_LEN, HEAD_DIM, CMP_HIDDEN), (CMP_LEN * HEAD_DIM) ** -0.5)
    inp['b_cmp_w2'] = nrm(ks[19], (N_B_LAYERS, 2, CMP_HIDDEN, HEAD_DIM), CMP_HIDDEN ** -0.5)
    inp['b_w_out'] = nrm(ks[20], (N_B_LAYERS, WIDTH_B, D_MODEL), WIDTH_B ** -0.5)
    return inp


def reference(x_prompt, x_sample, cache_dil_0, cache_dil_1, cache_dil_2, cache_nsa_paged, cache_nsa_win,
              page_table, a_norm, a_w_in, a_q_norm, a_k_norm, a_w_out, b_norm, b_w_in, b_q_norm,
              b_k_norm, b_cmp_pe, b_cmp_w1, b_cmp_w2, b_w_out):
    dil_caches = (cache_dil_0, cache_dil_1, cache_dil_2)
    past_len = page_table.shape[1] * PAGE_SIZE
    xp, xs = x_prompt, x_sample
    dil_p = [[] for _ in DIL_PATTERNS]
    dil_s = [[] for _ in DIL_PATTERNS]
    rows_p, rows_s, win_p, win_s = [], [], [], []
    for layer in range(DEPTH):
        j = layer // N_MIXERS
        if layer % N_MIXERS == 0:
            xp, st_p = layer_a_prompt(xp, a_norm[j], a_w_in[j], a_q_norm[j], a_k_norm[j], a_w_out[j])
            xs, st_s = layer_a_sample(xs, [c[j] for c in dil_caches], past_len, a_norm[j], a_w_in[j],
                                      a_q_norm[j], a_k_norm[j], a_w_out[j])
            for g in range(N_GROUPS_A):
                dil_p[g].append(st_p[g])
                dil_s[g].append(st_s[g])
        else:
            xp, rp, wp = layer_b_prompt(xp, b_norm[j], b_w_in[j], b_q_norm[j], b_k_norm[j], b_cmp_pe[j],
                                        b_cmp_w1[j], b_cmp_w2[j], b_w_out[j])
            xs, rs, ws = layer_b_sample(xs, cache_nsa_paged[j], page_table, cache_nsa_win[j], b_norm[j],
                                        b_w_in[j], b_q_norm[j], b_k_norm[j], b_cmp_pe[j], b_cmp_w1[j],
                                        b_cmp_w2[j], b_w_out[j])
            rows_p.append(rp)
            win_p.append(wp)
            rows_s.append(rs)
            win_s.append(ws)
    return (xp, xs,
            jnp.stack(dil_p[0]), jnp.stack(dil_p[1]), jnp.stack(dil_p[2]),
            jnp.stack(rows_p), jnp.stack(win_p),
            jnp.stack(dil_s[0]), jnp.stack(dil_s[1]), jnp.stack(dil_s[2]),
            jnp.stack(rows_s), jnp.stack(win_s))
```

```python
import functools

import jax
import jax.numpy as jnp
from jax import lax
from jax.experimental import pallas as pl
from jax.experimental.pallas import tpu as pltpu

F32 = jnp.float32
BF16 = jnp.bfloat16

HEAD_DIM = 64
SCALE = HEAD_DIM ** -0.5
ROPE_THETA = 10000.0
EPS = 1e-6
TINY = 1e-30
NEG = -1e30
DIL_PATTERNS = ((128, 1), (512, 4), (2048, 16))
N_GROUPS_A = len(DIL_PATTERNS)
HEADS_A = 8
WIDTH_A = HEADS_A * HEAD_DIM
QKV_A = 3 * N_GROUPS_A * WIDTH_A
KV_HEADS_B = 4
GQA_B = 4
HEADS_B = KV_HEADS_B * GQA_B
WIDTH_B = HEADS_B * HEAD_DIM
KVW_B = KV_HEADS_B * HEAD_DIM
CMP_LEN = 32
CMP_STRIDE = 16
CMP_RATIO = CMP_LEN // CMP_STRIDE
CMP_HIDDEN = 128
SEL_BLOCK = 64
SEL_TOPN = 16
WIN_B = 512
PAGE_SIZE = 128

LANES = 128
PROJ_TN = 256
VMEM_LIMIT = 56 * 1024 * 1024
IDX_ROWS = 24
SAMPLE_ROWS = 8

B_QC, B_QR, B_Z, B_KV, B_WIN, B_GATE = 0, 1024, 2048, 3072, 4096, 4608
B_COLS = B_GATE + 6 * LANES

EP_NONE, EP_NORM_Q, EP_ROPE_Q, EP_NORM_K, EP_ROPE_K, EP_SIG = range(6)


def _cparams(sem):
    return pltpu.CompilerParams(dimension_semantics=sem, vmem_limit_bytes=VMEM_LIMIT)


def _nt_dot(a, b):
    return lax.dot_general(a, b, (((1,), (1,)), ((), ())), preferred_element_type=F32)


def _dot(a, b):
    return jnp.dot(a, b, preferred_element_type=F32)


def _head_norm(acc, bd_ref, gain):
    msq = _dot((acc * acc).astype(BF16), bd_ref[...])
    return acc * lax.rsqrt(msq + EPS) * gain


def _rope_tile(y, cos, sin_signed):
    lane = lax.broadcasted_iota(jnp.int32, (y.shape[0], LANES), 1)
    first_half = (lane & (HEAD_DIM - 1)) < (HEAD_DIM // 2)
    outs = []
    for c in range(y.shape[1] // LANES):
        yc = y[:, c * LANES:(c + 1) * LANES]
        partner = jnp.where(first_half, pltpu.roll(yc, LANES - HEAD_DIM // 2, axis=1),
                            pltpu.roll(yc, HEAD_DIM // 2, axis=1))
        outs.append(yc * cos + partner * sin_signed)
    return jnp.concatenate(outs, axis=1)


def _proj_kernel(x_ref, g_ref, w_ref, cos_ref, sin_ref, qg_ref, kg_ref, bd_ref, o_ref, h_ref, *, kinds):
    j = pl.program_id(1)

    @pl.when(j == 0)
    def _():
        x = x_ref[...]
        r = lax.rsqrt(jnp.mean(x * x, axis=-1, keepdims=True) + EPS)
        h_ref[...] = (x * r * g_ref[...]).astype(BF16)

    acc = _dot(h_ref[...], w_ref[...])

    def ranges(kind):
        out, start = [], None
        for t, k in enumerate(list(kinds) + [None]):
            if k == kind and start is None:
                start = t
            if k != kind and start is not None:
                out.append((start, t))
                start = None
        return out

    def emit(kind, fn):
        rs = ranges(kind)
        if not rs:
            return
        cond = None
        for a, b in rs:
            c = (j >= a) & (j < b)
            cond = c if cond is None else (cond | c)

        @pl.when(cond)
        def _():
            o_ref[...] = fn()

    emit(EP_NONE, lambda: acc)
    emit(EP_SIG, lambda: jax.nn.sigmoid(acc))
    emit(EP_NORM_Q, lambda: _head_norm(acc, bd_ref, qg_ref[...]))
    emit(EP_NORM_K, lambda: _head_norm(acc, bd_ref, kg_ref[...]))
    emit(EP_ROPE_Q, lambda: _rope_tile(_head_norm(acc, bd_ref, qg_ref[...]), cos_ref[...], sin_ref[...]))
    emit(EP_ROPE_K, lambda: _rope_tile(_head_norm(acc, bd_ref, kg_ref[...]), cos_ref[...], sin_ref[...]))


def _project(x, norm_g, w_bf16, kinds, cos_t, sin_t, q_gain, k_gain):
    T, D = x.shape
    E = w_bf16.shape[1]
    tn = PROJ_TN
    assert E == len(kinds) * tn
    tm = min(T, 1024)
    assert T % tm == 0
    rep = tn // HEAD_DIM
    qg = jnp.tile(q_gain.astype(F32), rep)[None]
    kg = jnp.tile(k_gain.astype(F32), rep)[None]
    hid = jnp.arange(tn) // HEAD_DIM
    bd = ((hid[:, None] == hid[None, :]).astype(F32) / HEAD_DIM).astype(BF16)
    return pl.pallas_call(
        functools.partial(_proj_kernel, kinds=tuple(kinds)),
        grid=(T // tm, E // tn),
        in_specs=[
            pl.BlockSpec((tm, D), lambda i, j: (i, 0)),
            pl.BlockSpec((1, D), lambda i, j: (0, 0)),
            pl.BlockSpec((D, tn), lambda i, j: (0, j)),
            pl.BlockSpec((tm, LANES), lambda i, j: (i, 0)),
            pl.BlockSpec((tm, LANES), lambda i, j: (i, 0)),
            pl.BlockSpec((1, tn), lambda i, j: (0, 0)),
            pl.BlockSpec((1, tn), lambda i, j: (0, 0)),
            pl.BlockSpec((tn, tn), lambda i, j: (0, 0)),
        ],
        out_specs=pl.BlockSpec((tm, tn), lambda i, j: (i, j)),
        out_shape=jax.ShapeDtypeStruct((T, E), F32),
        scratch_shapes=[pltpu.VMEM((tm, D), BF16)],
        compiler_params=_cparams(("parallel", "arbitrary")),
    )(x, norm_g.astype(F32)[None], w_bf16, cos_t, sin_t, qg, kg, bd)


def _rope_tables(pos):
    half = HEAD_DIM // 2
    inv_freq = ROPE_THETA ** (-jnp.arange(half, dtype=F32) / half)
    ang = pos.astype(F32)[:, None] * inv_freq[None, :]
    cos, sin = jnp.cos(ang), jnp.sin(ang)
    cos_t = jnp.concatenate([cos, cos, cos, cos], axis=1)
    sin_t = jnp.concatenate([-sin, sin, -sin, sin], axis=1)
    return cos_t, sin_t


def _prep_q(q, nh, gqa, tq):
    lane = lax.broadcasted_iota(jnp.int32, (tq, LANES), 1)
    low = lane < HEAD_DIM
    parts = []
    for j in range(nh):
        col = q[:, (j // 2) * LANES:(j // 2 + 1) * LANES]
        nat, tgt = j % 2, (j // gqa) % 2
        if nat != tgt:
            col = pltpu.roll(col, HEAD_DIM, axis=1)
        keep = low if tgt == 0 else jnp.logical_not(low)
        parts.append(jnp.where(keep, col * SCALE, 0.0))
    return jnp.concatenate(parts, axis=0).astype(BF16)


def _unprep_o(o, nh, gqa, tq):
    lane = lax.broadcasted_iota(jnp.int32, (tq, LANES), 1)
    low = lane < HEAD_DIM
    cols = []
    for c in range(nh // 2):
        halves = []
        for nat in (0, 1):
            j = 2 * c + nat
            tgt = (j // gqa) % 2
            oj = o[j * tq:(j + 1) * tq]
            if tgt != nat:
                oj = pltpu.roll(oj, HEAD_DIM, axis=1)
            halves.append(oj)
        cols.append(jnp.where(low, halves[0], halves[1]))
    return jnp.concatenate(cols, axis=1)


def _apply_gates(o, g_ref, nh, tq):
    if g_ref is None:
        return o
    g = g_ref[...]
    return jnp.concatenate([o[j * tq:(j + 1) * tq] * g[:, j:j + 1] for j in range(nh)], axis=0)


def _online_update(s, vt, m_ref, l_ref, acc_ref):
    m_old = m_ref[...]
    m_new = jnp.maximum(m_old, jnp.max(s, axis=-1, keepdims=True))
    alpha = jnp.exp(m_old - m_new)
    p = jnp.exp(s - m_new)
    l_ref[...] = alpha * l_ref[...] + jnp.sum(p, axis=-1, keepdims=True)
    acc_ref[...] = alpha * acc_ref[...] + _dot(p.astype(BF16), vt)
    m_ref[...] = m_new


def _init_stats(m_ref, l_ref, acc_ref):
    m_ref[...] = jnp.full(m_ref.shape, NEG, F32)
    l_ref[...] = jnp.zeros(l_ref.shape, F32)
    acc_ref[...] = jnp.zeros(acc_ref.shape, F32)


def _band_kernel(*refs, nh, gqa, tq, tk, kw, want_lse, gated):
    q_ref, k_ref, v_ref = refs[:3]
    pos = 3
    g_ref = None
    if gated:
        g_ref = refs[pos]
        pos += 1
    o_ref = refs[pos]
    pos += 1
    lse_ref = None
    if want_lse:
        lse_ref = refs[pos]
        pos += 1
    m_ref, l_ref, acc_ref = refs[pos:pos + 3]

    i = pl.program_id(3)
    R = nh * tq
    q0 = i * tq
    Q = _prep_q(q_ref[...], nh, gqa, tq)
    _init_stats(m_ref, l_ref, acc_ref)
    row = lax.broadcasted_iota(jnp.int32, (R, tk), 0) & (tq - 1)
    col = lax.broadcasted_iota(jnp.int32, (R, tk), 1)
    rel = row - col

    def body(jj, carry):
        ks = pl.multiple_of(jj * tk, tk)
        kt = k_ref[pl.ds(ks, tk), :].astype(BF16)
        vt = v_ref[pl.ds(ks, tk), :].astype(BF16)
        s = _nt_dot(Q, kt)
        dist = rel + (q0 - ks)
        s = jnp.where((dist >= 0) & (dist <= kw), s, NEG)
        _online_update(s, vt, m_ref, l_ref, acc_ref)
        return carry

    lo_t = jnp.maximum(q0 - kw, 0) // tk
    hi_t = (q0 + tq - 1) // tk
    lax.fori_loop(lo_t, hi_t + 1, body, 0)

    l = l_ref[...]
    o = acc_ref[...] / l
    o = _apply_gates(o, g_ref, nh, tq)
    o_ref[...] = _unprep_o(o, nh, gqa, tq)
    if want_lse:
        lse = jnp.broadcast_to(m_ref[...] + jnp.log(l), (R, LANES))
        lse_ref[...] = _unprep_o(lse, nh, gqa, tq)


def _band_attention(u, *, q_off, k_off, v_off, nh, gqa, n_slabs, kw, dil, want_lse,
                    gates=None, g_off=0, tq=256, tk=256):
    B, S, W = u.shape
    n = S // dil
    qs = nh * HEAD_DIM
    tq = min(tq, n)
    tk = min(tk, n)
    uv = u.reshape(B, n, dil * W)
    wo = n_slabs * qs
    in_specs = [
        pl.BlockSpec((None, tq, qs), lambda b, r, p, i: (b, i, r * (W // qs) + q_off // qs + p)),
        pl.BlockSpec((None, n, LANES), lambda b, r, p, i: (b, 0, r * (W // LANES) + k_off // LANES + p)),
        pl.BlockSpec((None, n, LANES), lambda b, r, p, i: (b, 0, r * (W // LANES) + v_off // LANES + p)),
    ]
    args = [uv, uv, uv]
    if gates is not None:
        Wg = gates.shape[-1]
        in_specs.append(pl.BlockSpec((None, tq, LANES),
                                     lambda b, r, p, i: (b, i, r * (Wg // LANES) + g_off // LANES + p)))
        args.append(gates.reshape(B, n, dil * Wg))
    o_spec = pl.BlockSpec((None, tq, qs), lambda b, r, p, i: (b, i, r * n_slabs + p))
    o_shape = jax.ShapeDtypeStruct((B, n, dil * wo), F32)
    R = nh * tq
    res = pl.pallas_call(
        functools.partial(_band_kernel, nh=nh, gqa=gqa, tq=tq, tk=tk, kw=kw, want_lse=want_lse,
                          gated=gates is not None),
        grid=(B, dil, n_slabs, n // tq),
        in_specs=in_specs,
        out_specs=[o_spec, o_spec] if want_lse else o_spec,
        out_shape=[o_shape, o_shape] if want_lse else o_shape,
        scratch_shapes=[pltpu.VMEM((R, 1), F32), pltpu.VMEM((R, 1), F32), pltpu.VMEM((R, LANES), F32)],
        compiler_params=_cparams(("parallel", "parallel", "parallel", "arbitrary")),
    )(*args)
    if want_lse:
        return res[0].reshape(B, S, wo), res[1].reshape(B, S, wo)
    return res.reshape(B, S, wo)


def _compress_math(x_ref, w1_ref, pe_ref, w1f_ref, w2_ref, bd_ref, kg_ref, is_key, n_chunk):
    hid0 = _dot(pe_ref[...], w1f_ref[...])[0:1]
    acc = jnp.zeros((n_chunk, 4 * CMP_HIDDEN), F32)
    for j in range(CMP_STRIDE):
        xj = x_ref[pl.ds(j, n_chunk, stride=CMP_STRIDE), :].astype(BF16)
        acc = acc + _dot(xj, w1_ref[j])
    out = jnp.zeros((n_chunk, LANES), F32)
    for a in range(2):
        p0 = acc[:, (2 * a) * CMP_HIDDEN:(2 * a + 1) * CMP_HIDDEN]
        p1 = acc[:, (2 * a + 1) * CMP_HIDDEN:(2 * a + 2) * CMP_HIDDEN]
        hid = hid0 + p0 + pltpu.roll(p1, n_chunk - 1, axis=0)
        out = out + _dot(jax.nn.silu(hid).astype(BF16), w2_ref[a])
    normed = _head_norm(out, bd_ref, kg_ref[...])
    return jnp.where(is_key, normed, out)


def _compress_prompt_kernel(x_ref, w1_ref, pe_ref, w1f_ref, w2_ref, bd_ref, kg_ref, o_ref, *, n_chunk):
    is_key = pl.program_id(1) == 0
    o_ref[...] = _compress_math(x_ref, w1_ref, pe_ref, w1f_ref, w2_ref, bd_ref, kg_ref, is_key, n_chunk)


def _compress_weights(pe, w1, w2, k_gain):
    w1r = w1.reshape(2, CMP_RATIO, CMP_STRIDE, HEAD_DIM, CMP_HIDDEN)
    wj = jnp.concatenate([w1r[:, 0], w1r[:, 1]], axis=-1)
    z = jnp.zeros_like(wj)
    w1bd = jnp.concatenate([jnp.concatenate([wj, z], axis=-1),
                            jnp.concatenate([z, wj], axis=-1)], axis=-2).astype(BF16)
    pe8 = jnp.concatenate([pe.reshape(2, 1, CMP_LEN * HEAD_DIM),
                           jnp.zeros((2, 7, CMP_LEN * HEAD_DIM), F32)], axis=1).astype(BF16)
    w1f = w1.reshape(2, CMP_LEN * HEAD_DIM, CMP_HIDDEN).astype(BF16)
    z2 = jnp.zeros_like(w2)
    w2pad = jnp.stack([jnp.concatenate([w2, z2], axis=-1),
                       jnp.concatenate([z2, w2], axis=-1)], axis=1).astype(BF16)
    hid = jnp.arange(LANES) // HEAD_DIM
    bd = ((hid[:, None] == hid[None, :]).astype(F32) / HEAD_DIM).astype(BF16)
    kg = jnp.tile(k_gain.astype(F32), 2)[None]
    return w1bd, pe8, w1f, w2pad, bd, kg


def _cw_specs(nd):
    def sp(shape, fn):
        return pl.BlockSpec(shape, fn)
    if nd == 3:
        return [
            sp((None, CMP_STRIDE, LANES, 4 * CMP_HIDDEN), lambda b, t, p: (t, 0, 0, 0)),
            sp((None, 8, CMP_LEN * HEAD_DIM), lambda b, t, p: (t, 0, 0)),
            sp((None, CMP_LEN * HEAD_DIM, CMP_HIDDEN), lambda b, t, p: (t, 0, 0)),
            sp((None, 2, CMP_HIDDEN, LANES), lambda b, t, p: (t, 0, 0, 0)),
            sp((LANES, LANES), lambda b, t, p: (0, 0)),
            sp((1, LANES), lambda b, t, p: (0, 0)),
        ]
    return [
        sp((None, CMP_STRIDE, LANES, 4 * CMP_HIDDEN), lambda b, t, p, *_: (t, 0, 0, 0)),
        sp((None, 8, CMP_LEN * HEAD_DIM), lambda b, t, p, *_: (t, 0, 0)),
        sp((None, CMP_LEN * HEAD_DIM, CMP_HIDDEN), lambda b, t, p, *_: (t, 0, 0)),
        sp((None, 2, CMP_HIDDEN, LANES), lambda b, t, p, *_: (t, 0, 0, 0)),
        sp((LANES, LANES), lambda b, t, p, *_: (0, 0)),
        sp((1, LANES), lambda b, t, p, *_: (0, 0)),
    ]


def _compress_prompt(u, col0, cw):
    B, S, W = u.shape
    n_chunk = S // CMP_STRIDE
    base = col0 // LANES
    return pl.pallas_call(
        functools.partial(_compress_prompt_kernel, n_chunk=n_chunk),
        grid=(B, 2, 2),
        in_specs=[pl.BlockSpec((None, S, LANES), lambda b, t, p: (b, 0, base + 2 * t + p))] + _cw_specs(3),
        out_specs=pl.BlockSpec((None, None, n_chunk, LANES), lambda b, t, p: (b, t, 0, p)),
        out_shape=jax.ShapeDtypeStruct((B, 2, n_chunk, KVW_B), F32),
        compiler_params=_cparams(("parallel", "parallel", "parallel")),
    )(u, *cw)


def _compress_sample_kernel(pt_ref, pool_ref, w1_ref, pe_ref, w1f_ref, w2_ref, bd_ref, kg_ref, o_ref,
                            buf, sem, *, n_pages, n_chunk):
    n, t, p = pl.program_id(0), pl.program_id(1), pl.program_id(2)
    step = (n * 2 + t) * 2 + p
    n_steps = pl.num_programs(0) * 4

    def page_copy(seq, rt, pair, slot, pg):
        page = pt_ref[seq * n_pages + pg]
        lane0 = pl.multiple_of(rt * KVW_B + pair * LANES, LANES)
        return pltpu.make_async_copy(
            pool_ref.at[page, :, pl.ds(lane0, LANES)],
            buf.at[slot, pl.ds(pl.multiple_of(pg * PAGE_SIZE, PAGE_SIZE), PAGE_SIZE), :],
            sem.at[slot])

    def issue(st, slot):
        seq, rt, pair = st // 4, (st // 2) % 2, st % 2

        def go(pg, c):
            page_copy(seq, rt, pair, slot, pg).start()
            return c
        lax.fori_loop(0, n_pages, go, 0)

    @pl.when(step == 0)
    def _():
        issue(step, 0)

    @pl.when(step + 1 < n_steps)
    def _():
        issue(step + 1, (step + 1) % 2)

    slot = step % 2

    def wait(pg, c):
        page_copy(n, t, p, slot, pg).wait()
        return c
    lax.fori_loop(0, n_pages, wait, 0)

    o_ref[...] = _compress_math(buf.at[slot], w1_ref, pe_ref, w1f_ref, w2_ref, bd_ref, kg_ref,
                                t == 0, n_chunk)


def _compress_sample(pool, page_table, cw):
    N, n_pages = page_table.shape
    n_chunk = n_pages * PAGE_SIZE // CMP_STRIDE
    gs = pltpu.PrefetchScalarGridSpec(
        num_scalar_prefetch=1,
        grid=(N, 2, 2),
        in_specs=[pl.BlockSpec(memory_space=pl.ANY)] + _cw_specs(4),
        out_specs=pl.BlockSpec((None, None, n_chunk, LANES), lambda b, t, p, *_: (b, t, 0, p)),
        scratch_shapes=[pltpu.VMEM((2, n_pages * PAGE_SIZE, LANES), F32), pltpu.SemaphoreType.DMA((2,))],
    )
    return pl.pallas_call(
        functools.partial(_compress_sample_kernel, n_pages=n_pages, n_chunk=n_chunk),
        grid_spec=gs,
        out_shape=jax.ShapeDtypeStruct((N, 2, n_chunk, KVW_B), F32),
        compiler_params=_cparams(("arbitrary", "arbitrary", "arbitrary")),
    )(page_table.reshape(-1), pool, *cw)


def _cmp_kernel(q_ref, kc_ref, vc_ref, cov_ref, g_ref, o_ref, sel_ref, idx_ref, *,
                tq, n_chunk, nselp, pos_base, top_n):
    nh, gqa = 2 * GQA_B, GQA_B
    i = pl.program_id(2)
    R = nh * tq
    Q = _prep_q(q_ref[...], nh, gqa, tq)
    kc = kc_ref[...].astype(BF16)
    vc = vc_ref[...].astype(BF16)
    s = _nt_dot(Q, kc)
    pos_r = pos_base + i * tq + (lax.broadcasted_iota(jnp.int32, (R, n_chunk), 0) & (tq - 1))
    cend = lax.broadcasted_iota(jnp.int32, (R, n_chunk), 1) * CMP_STRIDE + (CMP_LEN - 1)
    s = jnp.where(cend <= pos_r, s, -jnp.inf)
    m = jnp.max(s, axis=-1, keepdims=True)
    m = jnp.where(m > -jnp.inf, m, 0.0)
    e = jnp.exp(s - m)
    den = jnp.sum(e, axis=-1, keepdims=True)
    p = e / jnp.maximum(den, TINY)
    o = _dot(p.astype(BF16), vc)
    o = _apply_gates(o, g_ref, nh, tq)
    o_ref[...] = _unprep_o(o, nh, gqa, tq)

    cov = cov_ref[...]
    blk = lax.broadcasted_iota(jnp.int32, (nselp, tq), 0)
    pos_c = pos_base + i * tq + lax.broadcasted_iota(jnp.int32, (1, tq), 1)
    cur = pos_c // SEL_BLOCK
    valid = blk <= cur
    forced = (blk == 0) | (blk == cur) | (blk == cur - 1)
    n_forced = jnp.sum(forced.astype(jnp.int32), axis=0, keepdims=True)
    zero_row = jnp.zeros((1, tq), jnp.int32)
    for a in range(2):
        ps = p[(a * gqa) * tq:(a * gqa + 1) * tq]
        for g in range(1, gqa):
            ps = ps + p[(a * gqa + g) * tq:(a * gqa + g + 1) * tq]
        hi = ps.astype(BF16)
        lo = (ps - hi.astype(F32)).astype(BF16)
        imp = _nt_dot(cov, hi) + _nt_dot(cov, lo)
        rem = jnp.where(valid & jnp.logical_not(forced), imp, -1.0)
        sel = forced
        idx_ref[a, pl.ds(0, 1), :] = zero_row
        idx_ref[a, pl.ds(1, 1), :] = cur
        idx_ref[a, pl.ds(2, 1), :] = jnp.maximum(cur - 1, 0)
        for it in range(top_n - 1):
            mx = jnp.max(rem, axis=0, keepdims=True)
            first = jnp.min(jnp.where(rem == mx, blk, nselp), axis=0, keepdims=True)
            active = (it < top_n - n_forced) & (mx >= 0.0)
            pick = (blk == first) & active
            sel = sel | pick
            rem = jnp.where(pick, -1.0, rem)
            idx_ref[a, pl.ds(3 + it, 1), :] = jnp.where(active, first, 0)
        for r in range(3 + top_n - 1, IDX_ROWS):
            idx_ref[a, pl.ds(r, 1), :] = zero_row
        sel_ref[a] = sel.astype(F32)


def _cover_t(n_chunk, n_cmp, n_sel, nselp):
    c = jnp.arange(n_chunk)[None, :]
    j = jnp.arange(nselp)[:, None]
    cov = ((c * CMP_STRIDE <= j * SEL_BLOCK + SEL_BLOCK - 1) & (c * CMP_STRIDE + CMP_LEN - 1 >= j * SEL_BLOCK)
           & (c < n_cmp) & (j < n_sel))
    return cov.astype(BF16)


def _cmp_attention(q, q_off, kvc, gates, g_off, *, tq, pos_base, n_keys):
    N, Tq, W = q.shape
    n_chunk = kvc.shape[2]
    n_cmp = (n_keys - CMP_LEN) // CMP_STRIDE + 1
    n_sel = -(-n_keys // SEL_BLOCK)
    nselp = -(-n_sel // 8) * 8
    top_n = min(SEL_TOPN, n_sel)
    qs = 2 * GQA_B * HEAD_DIM
    Wg = gates.shape[-1]
    cov = _cover_t(n_chunk, n_cmp, n_sel, nselp)
    return pl.pallas_call(
        functools.partial(_cmp_kernel, tq=tq, n_chunk=n_chunk, nselp=nselp, pos_base=pos_base, top_n=top_n),
        grid=(N, 2, Tq // tq),
        in_specs=[
            pl.BlockSpec((None, tq, qs), lambda b, p, i: (b, i, q_off // qs + p)),
            pl.BlockSpec((None, None, n_chunk, LANES), lambda b, p, i: (b, 0, 0, p)),
            pl.BlockSpec((None, None, n_chunk, LANES), lambda b, p, i: (b, 1, 0, p)),
            pl.BlockSpec((nselp, n_chunk), lambda b, p, i: (0, 0)),
            pl.BlockSpec((None, tq, LANES), lambda b, p, i: (b, i, g_off // LANES + p)),
        ],
        out_specs=[
            pl.BlockSpec((None, tq, qs), lambda b, p, i: (b, i, p)),
            pl.BlockSpec((None, 2, nselp, tq), lambda b, p, i: (b, p, 0, i)),
            pl.BlockSpec((None, 2, IDX_ROWS, tq), lambda b, p, i: (b, p, 0, i)),
        ],
        out_shape=[
            jax.ShapeDtypeStruct((N, Tq, WIDTH_B), F32),
            jax.ShapeDtypeStruct((N, KV_HEADS_B, nselp, Tq), F32),
            jax.ShapeDtypeStruct((N, KV_HEADS_B, IDX_ROWS, Tq), jnp.int32),
        ],
        compiler_params=_cparams(("parallel", "parallel", "parallel")),
    )(q, kvc, kvc, cov, gates)


def _sel_prompt_kernel(q_ref, k_ref, v_ref, sel_ref, g_ref, o_ref, m_ref, l_ref, acc_ref, *, tq, tk):
    nh, gqa = 2 * GQA_B, GQA_B
    i = pl.program_id(2)
    q0 = i * tq
    Q = _prep_q(q_ref[...], nh, gqa, tq)
    selp = sel_ref[...]
    _init_stats(m_ref, l_ref, acc_ref)
    nblk = tk // SEL_BLOCK
    er = lax.broadcasted_iota(jnp.int32, (LANES, 2 * tk), 0)
    ec = lax.broadcasted_iota(jnp.int32, (LANES, 2 * tk), 1)
    same_head = (er // SEL_BLOCK) == (ec // tk)
    eblk = (er & (SEL_BLOCK - 1)) - (ec & (tk - 1)) // SEL_BLOCK
    rel = lax.broadcasted_iota(jnp.int32, (tq, tk), 0) - lax.broadcasted_iota(jnp.int32, (tq, tk), 1)

    def body(jj, carry):
        ks = pl.multiple_of(jj * tk, tk)
        kt = k_ref[pl.ds(ks, tk), :].astype(BF16)
        vt = v_ref[pl.ds(ks, tk), :].astype(BF16)
        expand = (same_head & (eblk == jj * nblk)).astype(BF16)
        mk = _dot(selp, expand)
        causal = rel + (q0 - ks) >= 0
        biases = [jnp.where((mk[:, a * tk:(a + 1) * tk] > 0.5) & causal, 0.0, NEG) for a in range(2)]
        bias = jnp.concatenate([biases[0]] * gqa + [biases[1]] * gqa, axis=0)
        s = _nt_dot(Q, kt) + bias
        _online_update(s, vt, m_ref, l_ref, acc_ref)
        return carry

    lax.fori_loop(0, (q0 + tq - 1) // tk + 1, body, 0)
    o = acc_ref[...] / l_ref[...]
    o = _apply_gates(o, g_ref, nh, tq)
    o_ref[...] = _unprep_o(o, nh, gqa, tq)


def _sel_prompt(u, sel, *, q_off, k_off, v_off, g_off, tq=128, tk=256):
    B, S, W = u.shape
    qs = 2 * GQA_B * HEAD_DIM
    tq = min(tq, S)
    tk = min(tk, S)
    R = 2 * GQA_B * tq
    return pl.pallas_call(
        functools.partial(_sel_prompt_kernel, tq=tq, tk=tk),
        grid=(B, 2, S // tq),
        in_specs=[
            pl.BlockSpec((None, tq, qs), lambda b, p, i: (b, i, q_off // qs + p)),
            pl.BlockSpec((None, S, LANES), lambda b, p, i: (b, 0, k_off // LANES + p)),
            pl.BlockSpec((None, S, LANES), lambda b, p, i: (b, 0, v_off // LANES + p)),
            pl.BlockSpec((None, tq, LANES), lambda b, p, i: (b, i, p)),
            pl.BlockSpec((None, tq, LANES), lambda b, p, i: (b, i, g_off // LANES + p)),
        ],
        out_specs=pl.BlockSpec((None, tq, qs), lambda b, p, i: (b, i, p)),
        out_shape=jax.ShapeDtypeStruct((B, S, WIDTH_B), F32),
        scratch_shapes=[pltpu.VMEM((R, 1), F32), pltpu.VMEM((R, 1), F32), pltpu.VMEM((R, LANES), F32)],
        compiler_params=_cparams(("parallel", "parallel", "arbitrary")),
    )(u, u, u, sel, u)


def _sel_sample_kernel(pt_ref, idx_ref, q_ref, kn_ref, vn_ref, g_ref, pool_ref, o_ref, kbuf, vbuf, sem, *,
                       nt, n_pages, n_pick):
    nh, gqa, tq = 2 * GQA_B, GQA_B, SAMPLE_ROWS
    n, p = pl.program_id(0), pl.program_id(1)
    step = n * 2 + p
    n_steps = pl.num_programs(0) * 2
    per_head = nt * n_pick
    blocks_per_page = PAGE_SIZE // SEL_BLOCK

    def copies(seq, pair, slot, c):
        a = c // per_head
        t = (c // n_pick) % nt
        r = c % n_pick
        src_row = jnp.where(r == 0, 0, r + 1)
        blk = idx_ref[((seq * KV_HEADS_B + pair * 2 + a) * IDX_ROWS + src_row) * tq + t]
        page = pt_ref[seq * n_pages + blk // blocks_per_page]
        tok = pl.multiple_of((blk % blocks_per_page) * SEL_BLOCK, SEL_BLOCK)
        dst0 = pl.multiple_of(r * SEL_BLOCK, SEL_BLOCK)
        out = []
        for row_type, buf in ((2, kbuf), (3, vbuf)):
            lane0 = pl.multiple_of(row_type * KVW_B + pair * LANES, LANES)
            out.append(pltpu.make_async_copy(
                pool_ref.at[page, pl.ds(tok, SEL_BLOCK), pl.ds(lane0, LANES)],
                buf.at[slot, a, t, pl.ds(dst0, SEL_BLOCK), :],
                sem.at[slot]))
        return out

    def issue(st, slot):
        def go(c, carry):
            for cp in copies(st // 2, st % 2, slot, c):
                cp.start()
            return carry
        lax.fori_loop(0, 2 * per_head, go, 0)

    @pl.when(step == 0)
    def _():
        issue(step, 0)

    @pl.when(step + 1 < n_steps)
    def _():
        issue(step + 1, (step + 1) % 2)

    slot = step % 2

    def wait(c, carry):
        for cp in copies(n, p, slot, c):
            cp.wait()
        return carry
    lax.fori_loop(0, 2 * per_head, wait, 0)

    R = nh * tq
    Q = _prep_q(q_ref[...], nh, gqa, tq)
    pad = jnp.zeros((LANES - tq, LANES), F32)
    kn = jnp.concatenate([kn_ref[...], pad], axis=0).astype(BF16)
    vn = jnp.concatenate([vn_ref[...], pad], axis=0).astype(BF16)
    row = lax.broadcasted_iota(jnp.int32, (R, LANES), 0)
    colk = lax.broadcasted_iota(jnp.int32, (R, LANES), 1)
    trow = row & (tq - 1)
    s_new = jnp.where((colk <= trow) & (colk < nt), _nt_dot(Q, kn), NEG)
    o = jnp.zeros((R, LANES), F32)
    for a in range(2):
        for t in range(nt):
            kt = kbuf[slot, a, t].astype(BF16)
            vt = vbuf[slot, a, t].astype(BF16)
            s = _nt_dot(Q, kt)
            m = jnp.maximum(jnp.max(s, axis=-1, keepdims=True), jnp.max(s_new, axis=-1, keepdims=True))
            p1 = jnp.exp(s - m)
            p2 = jnp.exp(s_new - m)
            l = jnp.sum(p1, axis=-1, keepdims=True) + jnp.sum(p2, axis=-1, keepdims=True)
            o_at = (_dot(p1.astype(BF16), vt) + _dot(p2.astype(BF16), vn)) / l
            mine = ((row // tq) // gqa == a) & (trow == t)
            o = jnp.where(mine, o_at, o)
    o = _apply_gates(o, g_ref, nh, tq)
    o_ref[...] = _unprep_o(o, nh, gqa, tq)


def _sel_sample(us, idx, pool, page_table, *, nt, q_off, k_off, v_off, g_off):
    N, tq, W = us.shape
    n_pages = page_table.shape[1]
    n_pick = SEL_TOPN - 1
    qs = 2 * GQA_B * HEAD_DIM
    gs = pltpu.PrefetchScalarGridSpec(
        num_scalar_prefetch=2,
        grid=(N, 2),
        in_specs=[
            pl.BlockSpec((None, tq, qs), lambda b, p, *_: (b, 0, q_off // qs + p)),
            pl.BlockSpec((None, tq, LANES), lambda b, p, *_: (b, 0, k_off // LANES + p)),
            pl.BlockSpec((None, tq, LANES), lambda b, p, *_: (b, 0, v_off // LANES + p)),
            pl.BlockSpec((None, tq, LANES), lambda b, p, *_: (b, 0, g_off // LANES + p)),
            pl.BlockSpec(memory_space=pl.ANY),
        ],
        out_specs=pl.BlockSpec((None, tq, qs), lambda b, p, *_: (b, 0, p)),
        scratch_shapes=[
            pltpu.VMEM((2, 2, nt, n_pick * SEL_BLOCK, LANES), F32),
            pltpu.VMEM((2, 2, nt, n_pick * SEL_BLOCK, LANES), F32),
            pltpu.SemaphoreType.DMA((2,)),
        ],
    )
    return pl.pallas_call(
        functools.partial(_sel_sample_kernel, nt=nt, n_pages=n_pages, n_pick=n_pick),
        grid_spec=gs,
        out_shape=jax.ShapeDtypeStruct((N, tq, WIDTH_B), F32),
        compiler_params=_cparams(("arbitrary", "arbitrary")),
    )(page_table.reshape(-1), idx.reshape(-1), us, us, us, us, pool)


def _cache_attn_kernel(*refs, nh, gqa, L, dil, win, nt, want_lse, gated):
    q_ref, kc_ref, vc_ref, kn_ref, vn_ref = refs[:5]
    pos = 5
    g_ref = None
    if gated:
        g_ref = refs[pos]
        pos += 1
    o_ref = refs[pos]
    lse_ref = refs[pos + 1] if want_lse else None
    tq = SAMPLE_ROWS
    R = nh * tq
    Q = _prep_q(q_ref[...], nh, gqa, tq)
    kc = kc_ref[...].astype(BF16)
    vc = vc_ref[...].astype(BF16)
    pad = jnp.zeros((LANES - tq, LANES), F32)
    kn = jnp.concatenate([kn_ref[...], pad], axis=0).astype(BF16)
    vn = jnp.concatenate([vn_ref[...], pad], axis=0).astype(BF16)
    t1 = lax.broadcasted_iota(jnp.int32, (R, L), 0) & (tq - 1)
    d1 = L + t1 - lax.broadcasted_iota(jnp.int32, (R, L), 1)
    s1 = jnp.where(((d1 & (dil - 1)) == 0) & (d1 <= win), _nt_dot(Q, kc), NEG)
    t2 = lax.broadcasted_iota(jnp.int32, (R, LANES), 0) & (tq - 1)
    c2 = lax.broadcasted_iota(jnp.int32, (R, LANES), 1)
    d2 = t2 - c2
    s2 = jnp.where((d2 >= 0) & ((d2 & (dil - 1)) == 0) & (d2 <= win) & (c2 < nt), _nt_dot(Q, kn), NEG)
    m = jnp.maximum(jnp.max(s1, axis=-1, keepdims=True), jnp.max(s2, axis=-1, keepdims=True))
    p1 = jnp.exp(s1 - m)
    p2 = jnp.exp(s2 - m)
    l = jnp.sum(p1, axis=-1, keepdims=True) + jnp.sum(p2, axis=-1, keepdims=True)
    o = (_dot(p1.astype(BF16), vc) + _dot(p2.astype(BF16), vn)) / l
    o = _apply_gates(o, g_ref, nh, tq)
    o_ref[...] = _unprep_o(o, nh, gqa, tq)
    if want_lse:
        lse_ref[...] = _unprep_o(jnp.broadcast_to(m + jnp.log(l), (R, LANES)), nh, gqa, tq)


def _cache_attention(us, cache, *, q_off, k_off, v_off, nh, gqa, n_slabs, dil, win, nt, want_lse,
                     g_off=None):
    N, tq, W = us.shape
    L = cache.shape[1]
    qs = nh * HEAD_DIM
    wo = n_slabs * qs
    in_specs = [
        pl.BlockSpec((None, tq, qs), lambda b, p: (b, 0, q_off // qs + p)),
        pl.BlockSpec((None, L, LANES), lambda b, p: (b, 0, p)),
        pl.BlockSpec((None, L, LANES), lambda b, p: (b, 0, n_slabs + p)),
        pl.BlockSpec((None, tq, LANES), lambda b, p: (b, 0, k_off // LANES + p)),
        pl.BlockSpec((None, tq, LANES), lambda b, p: (b, 0, v_off // LANES + p)),
    ]
    args = [us, cache, cache, us, us]
    if g_off is not None:
        in_specs.append(pl.BlockSpec((None, tq, LANES), lambda b, p: (b, 0, g_off // LANES + p)))
        args.append(us)
    o_spec = pl.BlockSpec((None, tq, qs), lambda b, p: (b, 0, p))
    o_shape = jax.ShapeDtypeStruct((N, tq, wo), F32)
    return pl.pallas_call(
        functools.partial(_cache_attn_kernel, nh=nh, gqa=gqa, L=L, dil=dil, win=win, nt=nt,
                          want_lse=want_lse, gated=g_off is not None),
        grid=(N, n_slabs),
        in_specs=in_specs,
        out_specs=[o_spec, o_spec] if want_lse else o_spec,
        out_shape=[o_shape, o_shape] if want_lse else o_shape,
        compiler_params=_cparams(("parallel", "parallel")),
    )(*args)


def _merge_a_kernel(o0, o1, o2, l0, l1, l2, z_ref, x_ref, w_ref, y_ref):
    a0, a1, a2 = l0[...], l1[...], l2[...]
    mx = jnp.maximum(jnp.maximum(a0, a1), a2)
    e0, e1, e2 = jnp.exp(a0 - mx), jnp.exp(a1 - mx), jnp.exp(a2 - mx)
    den = e0 + e1 + e2
    o = (e0 / den) * o0[...] + (e1 / den) * o1[...] + (e2 / den) * o2[...]
    y_ref[...] = x_ref[...] + _dot((o * jax.nn.silu(z_ref[...])).astype(BF16), w_ref[...])


def _merge_a(outs, lses, u, z_off, x, w_out_bf16):
    T, D = x.shape
    tm = min(T, 512)
    wa = WIDTH_A
    row = pl.BlockSpec((tm, wa), lambda i: (i, 0))
    return pl.pallas_call(
        _merge_a_kernel,
        grid=(T // tm,),
        in_specs=[row] * 6 + [
            pl.BlockSpec((tm, wa), lambda i: (i, z_off // wa)),
            pl.BlockSpec((tm, D), lambda i: (i, 0)),
            pl.BlockSpec((wa, D), lambda i: (0, 0)),
        ],
        out_specs=pl.BlockSpec((tm, D), lambda i: (i, 0)),
        out_shape=jax.ShapeDtypeStruct((T, D), F32),
        compiler_params=_cparams(("parallel",)),
    )(*outs, *lses, u, x, w_out_bf16)


def _merge_b_kernel(oc, os_, ow, z_ref, x_ref, w_ref, y_ref):
    o = oc[...] + os_[...] + ow[...]
    y_ref[...] = x_ref[...] + _dot((o * jax.nn.silu(z_ref[...])).astype(BF16), w_ref[...])


def _merge_b(o_c, o_s, o_w, u, z_off, x, w_out_bf16):
    T, D = x.shape
    tm = min(T, 512)
    wb = WIDTH_B
    row = pl.BlockSpec((tm, wb), lambda i: (i, 0))
    return pl.pallas_call(
        _merge_b_kernel,
        grid=(T // tm,),
        in_specs=[row] * 3 + [
            pl.BlockSpec((tm, wb), lambda i: (i, z_off // wb)),
            pl.BlockSpec((tm, D), lambda i: (i, 0)),
            pl.BlockSpec((wb, D), lambda i: (0, 0)),
        ],
        out_specs=pl.BlockSpec((tm, D), lambda i: (i, 0)),
        out_shape=jax.ShapeDtypeStruct((T, D), F32),
        compiler_params=_cparams(("parallel",)),
    )(o_c, o_s, o_w, u, x, w_out_bf16)


A_KINDS = ([EP_ROPE_Q] * (N_GROUPS_A * WIDTH_A // PROJ_TN) + [EP_ROPE_K] * (N_GROUPS_A * WIDTH_A // PROJ_TN)
           + [EP_NONE] * (N_GROUPS_A * WIDTH_A // PROJ_TN) + [EP_NONE] * (WIDTH_A // PROJ_TN))
A_K, A_V, A_Z = N_GROUPS_A * WIDTH_A, 2 * N_GROUPS_A * WIDTH_A, QKV_A

B_KINDS = ([EP_NORM_Q] * 4 + [EP_ROPE_Q] * 4 + [EP_NONE] * 4
           + [EP_NORM_K, EP_NONE, EP_ROPE_K, EP_NONE, EP_ROPE_K, EP_NONE] + [EP_SIG] * 3)


def _b_weight(w_in):
    wq = w_in[:, :WIDTH_B]
    o1 = WIDTH_B + 6 * KVW_B
    wkv = w_in[:, WIDTH_B:o1]
    wg = w_in[:, o1:o1 + 3 * HEADS_B]
    wz = w_in[:, o1 + 3 * HEADS_B:]
    nh = 2 * GQA_B
    tiles = []
    for b in range(3):
        for p in range(2):
            g = wg[:, b * HEADS_B + p * nh: b * HEADS_B + (p + 1) * nh]
            tiles.append(jnp.pad(g, ((0, 0), (0, LANES - nh))))
    return jnp.concatenate([wq, wq, wz, wkv] + tiles, axis=1).astype(BF16)


def _pad_rows(u, N, T):
    return jnp.pad(u.reshape(N, T, -1), ((0, 0), (0, SAMPLE_ROWS - T), (0, 0)))


def _layer_a_prompt(x, norm_g, w_bf16, q_gain, k_gain, w_out_bf16, tabs):
    B, S, D = x.shape
    u = _project(x.reshape(B * S, D), norm_g, w_bf16, A_KINDS, tabs[0], tabs[1], q_gain, k_gain)
    u3 = u.reshape(B, S, -1)
    outs, lses, states = [], [], []
    for g, (win, dil) in enumerate(DIL_PATTERNS):
        o, lse = _band_attention(u3, q_off=g * WIDTH_A, k_off=A_K + g * WIDTH_A, v_off=A_V + g * WIDTH_A,
                                 nh=2, gqa=1, n_slabs=HEADS_A // 2, kw=win // dil, dil=dil, want_lse=True)
        outs.append(o.reshape(B * S, WIDTH_A))
        lses.append(lse.reshape(B * S, WIDTH_A))
        w = min(win, S)
        kg = u3[:, S - w:, A_K + g * WIDTH_A: A_K + (g + 1) * WIDTH_A]
        vg = u3[:, S - w:, A_V + g * WIDTH_A: A_V + (g + 1) * WIDTH_A]
        states.append(jnp.stack([kg, vg], axis=2).reshape(B, w, 2, HEADS_A, HEAD_DIM))
    y = _merge_a(outs, lses, u, A_Z, x.reshape(B * S, D), w_out_bf16)
    return y.reshape(B, S, D), states


def _layer_a_sample(x, bufs, norm_g, w_bf16, q_gain, k_gain, w_out_bf16, tabs):
    N, T, D = x.shape
    u = _project(x.reshape(N * T, D), norm_g, w_bf16, A_KINDS, tabs[0], tabs[1], q_gain, k_gain)
    us = _pad_rows(u, N, T)
    u3 = u.reshape(N, T, -1)
    outs, lses, states = [], [], []
    for g, (win, dil) in enumerate(DIL_PATTERNS):
        L = bufs[g].shape[1]
        cache = bufs[g].reshape(N, L, 2 * WIDTH_A)
        o, lse = _cache_attention(us, cache, q_off=g * WIDTH_A, k_off=A_K + g * WIDTH_A,
                                  v_off=A_V + g * WIDTH_A, nh=2, gqa=1, n_slabs=HEADS_A // 2,
                                  dil=dil, win=win, nt=T, want_lse=True)
        outs.append(o[:, :T].reshape(N * T, WIDTH_A))
        lses.append(lse[:, :T].reshape(N * T, WIDTH_A))
        new = jnp.stack([u3[:, :, A_K + g * WIDTH_A: A_K + (g + 1) * WIDTH_A],
                         u3[:, :, A_V + g * WIDTH_A: A_V + (g + 1) * WIDTH_A]], axis=2)
        cat = jnp.concatenate([bufs[g], new.reshape(N, T, 2, HEADS_A, HEAD_DIM)], axis=1)
        states.append(cat[:, -min(win, L + T):])
    y = _merge_a(outs, lses, u, A_Z, x.reshape(N * T, D), w_out_bf16)
    return y.reshape(N, T, D), states


def _layer_b_prompt(x, norm_g, w_bf16, q_gain, k_gain, cw, w_out_bf16, tabs):
    B, S, D = x.shape
    u = _project(x.reshape(B * S, D), norm_g, w_bf16, B_KINDS, tabs[0], tabs[1], q_gain, k_gain)
    u3 = u.reshape(B, S, -1)
    kvc = _compress_prompt(u3, B_KV, cw)
    o_c, sel_t, _ = _cmp_attention(u3, B_QC, kvc, u3, B_GATE, tq=min(128, S), pos_base=0, n_keys=S)
    nselp = sel_t.shape[2]
    sel = jnp.transpose(sel_t[:, :, :SEL_BLOCK], (0, 3, 1, 2))
    if nselp < SEL_BLOCK:
        sel = jnp.pad(sel, ((0, 0), (0, 0), (0, 0), (0, SEL_BLOCK - nselp)))
    sel = sel.reshape(B, S, KV_HEADS_B * SEL_BLOCK).astype(BF16)
    o_s = _sel_prompt(u3, sel, q_off=B_QR, k_off=B_KV + 2 * KVW_B, v_off=B_KV + 3 * KVW_B,
                      g_off=B_GATE + 2 * LANES)
    o_w = _band_attention(u3, q_off=B_QR, k_off=B_WIN, v_off=B_WIN + KVW_B, nh=2 * GQA_B, gqa=GQA_B,
                          n_slabs=2, kw=WIN_B, dil=1, want_lse=False, gates=u3, g_off=B_GATE + 4 * LANES,
                          tq=128, tk=256)
    y = _merge_b(o_c.reshape(B * S, -1), o_s.reshape(B * S, -1), o_w.reshape(B * S, -1), u, B_Z,
                 x.reshape(B * S, D), w_out_bf16)
    rows = u3[:, :, B_KV:B_KV + 4 * KVW_B].reshape(B, S, 4, KV_HEADS_B, HEAD_DIM)
    w = min(WIN_B, S)
    wrows = u3[:, S - w:, B_WIN:B_WIN + 2 * KVW_B].reshape(B, w, 2, KV_HEADS_B, HEAD_DIM)
    return y.reshape(B, S, D), rows, wrows


def _layer_b_sample(x, pool, page_table, buf, norm_g, w_bf16, q_gain, k_gain, cw, w_out_bf16, tabs):
    N, T, D = x.shape
    past_len = page_table.shape[1] * PAGE_SIZE
    assert past_len % SEL_BLOCK == 0 and T <= SEL_BLOCK and T <= SAMPLE_ROWS
    u = _project(x.reshape(N * T, D), norm_g, w_bf16, B_KINDS, tabs[0], tabs[1], q_gain, k_gain)
    us = _pad_rows(u, N, T)
    u3 = u.reshape(N, T, -1)
    pool2 = pool.reshape(pool.shape[0], PAGE_SIZE, 4 * KVW_B)
    kvc = _compress_sample(pool2, page_table, cw)
    o_c, _, idx = _cmp_attention(us, B_QC, kvc, us, B_GATE, tq=SAMPLE_ROWS, pos_base=past_len,
                                 n_keys=past_len + T)
    o_s = _sel_sample(us, idx, pool2, page_table, nt=T, q_off=B_QR, k_off=B_KV + 2 * KVW_B,
                      v_off=B_KV + 3 * KVW_B, g_off=B_GATE + 2 * LANES)
    L = buf.shape[1]
    o_w = _cache_attention(us, buf.reshape(N, L, 2 * KVW_B), q_off=B_QR, k_off=B_WIN, v_off=B_WIN + KVW_B,
                           nh=2 * GQA_B, gqa=GQA_B, n_slabs=2, dil=1, win=WIN_B, nt=T, want_lse=False,
                           g_off=B_GATE + 4 * LANES)
    y = _merge_b(o_c[:, :T].reshape(N * T, -1), o_s[:, :T].reshape(N * T, -1), o_w[:, :T].reshape(N * T, -1),
                 u, B_Z, x.reshape(N * T, D), w_out_bf16)
    rows = u3[:, :, B_KV:B_KV + 4 * KVW_B].reshape(N, T, 4, KV_HEADS_B, HEAD_DIM)
    wrows = u3[:, :, B_WIN:B_WIN + 2 * KVW_B].reshape(N, T, 2, KV_HEADS_B, HEAD_DIM)
    new_buf = jnp.concatenate([buf, wrows], axis=1)[:, -min(WIN_B, L + T):]
    return y.reshape(N, T, D), rows, new_buf


def kernel(x_prompt, x_sample, cache_dil_0, cache_dil_1, cache_dil_2, cache_nsa_paged, cache_nsa_win,
           page_table, a_norm, a_w_in, a_q_norm, a_k_norm, a_w_out, b_norm, b_w_in, b_q_norm, b_k_norm,
           b_cmp_pe, b_cmp_w1, b_cmp_w2, b_w_out):
    dil_caches = (cache_dil_0, cache_dil_1, cache_dil_2)
    B, S, _ = x_prompt.shape
    N, T, _ = x_sample.shape
    past_len = page_table.shape[1] * PAGE_SIZE
    depth = a_norm.shape[0] + b_norm.shape[0]
    tabs_p = _rope_tables(jnp.tile(jnp.arange(S), B))
    tabs_s = _rope_tables(jnp.tile(past_len + jnp.arange(T), N))
    xp, xs = x_prompt, x_sample
    dil_p = [[] for _ in DIL_PATTERNS]
    dil_s = [[] for _ in DIL_PATTERNS]
    rows_p, rows_s, win_p, win_s = [], [], [], []
    for layer in range(depth):
        j = layer // 2
        if layer % 2 == 0:
            w = a_w_in[j].astype(BF16)
            wo = a_w_out[j].astype(BF16)
            xp, st_p = _layer_a_prompt(xp, a_norm[j], w, a_q_norm[j], a_k_norm[j], wo, tabs_p)
            xs, st_s = _layer_a_sample(xs, [c[j] for c in dil_caches], a_norm[j], w, a_q_norm[j],
                                       a_k_norm[j], wo, tabs_s)
            for g in range(N_GROUPS_A):
                dil_p[g].append(st_p[g])
                dil_s[g].append(st_s[g])
        else:
            w = _b_weight(b_w_in[j])
            wo = b_w_out[j].astype(BF16)
            cw = _compress_weights(b_cmp_pe[j], b_cmp_w1[j], b_cmp_w2[j], b_k_norm[j])
            xp, rp, wp = _layer_b_prompt(xp, b_norm[j], w, b_q_norm[j], b_k_norm[j], cw, wo, tabs_p)
            xs, rs, ws = _layer_b_sample(xs, cache_nsa_paged[j], page_table, cache_nsa_win[j], b_norm[j], w,
                                         b_q_norm[j], b_k_norm[j], cw, wo, tabs_s)
            rows_p.append(rp)
            win_p.append(wp)
            rows_s.append(rs)
            win_s.append(ws)
    return (xp, xs,
            jnp.stack(dil_p[0]), jnp.stack(dil_p[1]), jnp.stack(dil_p[2]),
            jnp.stack(rows_p), jnp.stack(win_p),
            jnp.stack(dil_s[0]), jnp.stack(dil_s[1]), jnp.stack(dil_s[2]),
            jnp.stack(rows_s), jnp.stack(win_s))
```

```python
import functools

import jax
import jax.numpy as jnp
from jax import lax
from jax.experimental import pallas as pl
from jax.experimental.pallas import tpu as pltpu

F32 = jnp.float32
BF16 = jnp.bfloat16

HEAD_DIM = 64
SCALE = HEAD_DIM ** -0.5
ROPE_THETA = 10000.0
EPS = 1e-6
TINY = 1e-30
NEG = -1e30
DIL_PATTERNS = ((128, 1), (512, 4), (2048, 16))
N_GROUPS_A = len(DIL_PATTERNS)
HEADS_A = 8
WIDTH_A = HEADS_A * HEAD_DIM
QKV_A = 3 * N_GROUPS_A * WIDTH_A
KV_HEADS_B = 4
GQA_B = 4
HEADS_B = KV_HEADS_B * GQA_B
WIDTH_B = HEADS_B * HEAD_DIM
KVW_B = KV_HEADS_B * HEAD_DIM
CMP_LEN = 32
CMP_STRIDE = 16
CMP_RATIO = CMP_LEN // CMP_STRIDE
CMP_HIDDEN = 128
SEL_BLOCK = 64
SEL_TOPN = 16
WIN_B = 512
PAGE_SIZE = 128

LANES = 128
PROJ_TN = 256
VMEM_LIMIT = 56 * 1024 * 1024
IDX_ROWS = 24
SAMPLE_ROWS = 8

B_QC, B_QR, B_Z, B_KV, B_WIN, B_GATE = 0, 1024, 2048, 3072, 4096, 4608
B_COLS = B_GATE + 6 * LANES

EP_NONE, EP_NORM_Q, EP_ROPE_Q, EP_NORM_K, EP_ROPE_K, EP_SIG = range(6)


def _cparams(sem):
    return pltpu.CompilerParams(dimension_semantics=sem, vmem_limit_bytes=VMEM_LIMIT)


def _nt_dot(a, b):
    return lax.dot_general(a, b, (((1,), (1,)), ((), ())), preferred_element_type=F32)


def _dot(a, b):
    return jnp.dot(a, b, preferred_element_type=F32)


def _head_norm(acc, bd_ref, gain):
    msq = _dot((acc * acc).astype(BF16), bd_ref[...])
    return acc * lax.rsqrt(msq + EPS) * gain


def _rope_tile(y, cos, sin_signed):
    lane = lax.broadcasted_iota(jnp.int32, (y.shape[0], LANES), 1)
    first_half = (lane & (HEAD_DIM - 1)) < (HEAD_DIM // 2)
    outs = []
    for c in range(y.shape[1] // LANES):
        yc = y[:, c * LANES:(c + 1) * LANES]
        partner = jnp.where(first_half, pltpu.roll(yc, LANES - HEAD_DIM // 2, axis=1),
                            pltpu.roll(yc, HEAD_DIM // 2, axis=1))
        outs.append(yc * cos + partner * sin_signed)
    return jnp.concatenate(outs, axis=1)


def _proj_kernel(x_ref, g_ref, w_ref, cos_ref, sin_ref, qg_ref, kg_ref, bd_ref, o_ref, h_ref, *, kinds):
    j = pl.program_id(1)

    @pl.when(j == 0)
    def _():
        x = x_ref[...]
        r = lax.rsqrt(jnp.mean(x * x, axis=-1, keepdims=True) + EPS)
        h_ref[...] = (x * r * g_ref[...]).astype(BF16)

    acc = _dot(h_ref[...], w_ref[...])

    def ranges(kind):
        out, start = [], None
        for t, k in enumerate(list(kinds) + [None]):
            if k == kind and start is None:
                start = t
            if k != kind and start is not None:
                out.append((start, t))
                start = None
        return out

    def emit(kind, fn):
        rs = ranges(kind)
        if not rs:
            return
        cond = None
        for a, b in rs:
            c = (j >= a) & (j < b)
            cond = c if cond is None else (cond | c)

        @pl.when(cond)
        def _():
            o_ref[...] = fn()

    emit(EP_NONE, lambda: acc)
    emit(EP_SIG, lambda: jax.nn.sigmoid(acc))
    emit(EP_NORM_Q, lambda: _head_norm(acc, bd_ref, qg_ref[...]))
    emit(EP_NORM_K, lambda: _head_norm(acc, bd_ref, kg_ref[...]))
    emit(EP_ROPE_Q, lambda: _rope_tile(_head_norm(acc, bd_ref, qg_ref[...]), cos_ref[...], sin_ref[...]))
    emit(EP_ROPE_K, lambda: _rope_tile(_head_norm(acc, bd_ref, kg_ref[...]), cos_ref[...], sin_ref[...]))


def _project(x, norm_g, w_bf16, kinds, cos_t, sin_t, q_gain, k_gain):
    T, D = x.shape
    E = w_bf16.shape[1]
    tn = PROJ_TN
    assert E == len(kinds) * tn
    tm = min(T, 1024)
    assert T % tm == 0
    rep = tn // HEAD_DIM
    qg = jnp.tile(q_gain.astype(F32), rep)[None]
    kg = jnp.tile(k_gain.astype(F32), rep)[None]
    hid = jnp.arange(tn) // HEAD_DIM
    bd = ((hid[:, None] == hid[None, :]).astype(F32) / HEAD_DIM).astype(BF16)
    return pl.pallas_call(
        functools.partial(_proj_kernel, kinds=tuple(kinds)),
        grid=(T // tm, E // tn),
        in_specs=[
            pl.BlockSpec((tm, D), lambda i, j: (i, 0)),
            pl.BlockSpec((1, D), lambda i, j: (0, 0)),
            pl.BlockSpec((D, tn), lambda i, j: (0, j)),
            pl.BlockSpec((tm, LANES), lambda i, j: (i, 0)),
            pl.BlockSpec((tm, LANES), lambda i, j: (i, 0)),
            pl.BlockSpec((1, tn), lambda i, j: (0, 0)),
            pl.BlockSpec((1, tn), lambda i, j: (0, 0)),
            pl.BlockSpec((tn, tn), lambda i, j: (0, 0)),
        ],
        out_specs=pl.BlockSpec((tm, tn), lambda i, j: (i, j)),
        out_shape=jax.ShapeDtypeStruct((T, E), F32),
        scratch_shapes=[pltpu.VMEM((tm, D), BF16)],
        compiler_params=_cparams(("parallel", "arbitrary")),
    )(x, norm_g.astype(F32)[None], w_bf16, cos_t, sin_t, qg, kg, bd)


def _rope_tables(pos):
    half = HEAD_DIM // 2
    inv_freq = ROPE_THETA ** (-jnp.arange(half, dtype=F32) / half)
    ang = pos.astype(F32)[:, None] * inv_freq[None, :]
    cos, sin = jnp.cos(ang), jnp.sin(ang)
    cos_t = jnp.concatenate([cos, cos, cos, cos], axis=1)
    sin_t = jnp.concatenate([-sin, sin, -sin, sin], axis=1)
    return cos_t, sin_t


def _prep_q(q, nh, gqa, tq):
    lane = lax.broadcasted_iota(jnp.int32, (tq, LANES), 1)
    low = lane < HEAD_DIM
    parts = []
    for j in range(nh):
        col = q[:, (j // 2) * LANES:(j // 2 + 1) * LANES]
        nat, tgt = j % 2, (j // gqa) % 2
        if nat != tgt:
            col = pltpu.roll(col, HEAD_DIM, axis=1)
        keep = low if tgt == 0 else jnp.logical_not(low)
        parts.append(jnp.where(keep, col * SCALE, 0.0))
    return jnp.concatenate(parts, axis=0).astype(BF16)


def _unprep_o(o, nh, gqa, tq):
    lane = lax.broadcasted_iota(jnp.int32, (tq, LANES), 1)
    low = lane < HEAD_DIM
    cols = []
    for c in range(nh // 2):
        halves = []
        for nat in (0, 1):
            j = 2 * c + nat
            tgt = (j // gqa) % 2
            oj = o[j * tq:(j + 1) * tq]
            if tgt != nat:
                oj = pltpu.roll(oj, HEAD_DIM, axis=1)
            halves.append(oj)
        cols.append(jnp.where(low, halves[0], halves[1]))
    return jnp.concatenate(cols, axis=1)


def _apply_gates(o, g_ref, nh, tq):
    if g_ref is None:
        return o
    g = g_ref[...]
    return jnp.concatenate([o[j * tq:(j + 1) * tq] * g[:, j:j + 1] for j in range(nh)], axis=0)


def _q_transposed(q, nh, gqa, tq):
    qt = (q * SCALE).T
    zero = jnp.zeros((HEAD_DIM, tq), F32)
    cols = []
    for j in range(nh):
        h = qt[j * HEAD_DIM:(j + 1) * HEAD_DIM]
        cols.append(jnp.concatenate([h, zero] if (j // gqa) % 2 == 0 else [zero, h], axis=0))
    return jnp.concatenate(cols, axis=1).astype(BF16)


def _online_update_t(s, vt_t, m_ref, l_ref, acc_ref):
    m_old = m_ref[...]
    m_new = jnp.maximum(m_old, jnp.max(s, axis=0, keepdims=True))
    alpha = jnp.exp(m_old - m_new)
    p = jnp.exp(s - m_new)
    l_ref[...] = alpha * l_ref[...] + jnp.sum(p, axis=0, keepdims=True)
    acc_ref[...] = alpha * acc_ref[...] + _dot(vt_t, p.astype(BF16))
    m_ref[...] = m_new


def _init_stats(m_ref, l_ref, acc_ref):
    m_ref[...] = jnp.full(m_ref.shape, NEG, F32)
    l_ref[...] = jnp.zeros(l_ref.shape, F32)
    acc_ref[...] = jnp.zeros(acc_ref.shape, F32)


def _finish_t(o_ref, lse_ref, g_ref, m_ref, l_ref, acc_ref, nh, gqa, tq):
    l = l_ref[...]
    inv = 1.0 / l
    g_t = g_ref[...].T if g_ref is not None else None
    parts = []
    for j in range(nh):
        half = (j // gqa) % 2
        sc = inv[:, j * tq:(j + 1) * tq]
        if g_t is not None:
            sc = sc * g_t[j:j + 1, :]
        parts.append(acc_ref[half * HEAD_DIM:(half + 1) * HEAD_DIM, j * tq:(j + 1) * tq] * sc)
    o_ref[...] = jnp.concatenate(parts, axis=0).T
    if lse_ref is not None:
        lse = m_ref[...] + jnp.log(l)
        rows = [jnp.broadcast_to(lse[:, j * tq:(j + 1) * tq], (HEAD_DIM, tq)) for j in range(nh)]
        lse_ref[...] = jnp.concatenate(rows, axis=0).T


def _band_kernel(*refs, nh, gqa, tq, tk, kw, want_lse, gated):
    q_ref, k_ref, vt_ref = refs[:3]
    pos = 3
    g_ref = None
    if gated:
        g_ref = refs[pos]
        pos += 1
    o_ref = refs[pos]
    pos += 1
    lse_ref = None
    if want_lse:
        lse_ref = refs[pos]
        pos += 1
    m_ref, l_ref, acc_ref = refs[pos:pos + 3]

    i = pl.program_id(3)
    R = nh * tq
    q0 = i * tq
    qt = _q_transposed(q_ref[...], nh, gqa, tq)
    _init_stats(m_ref, l_ref, acc_ref)
    rel = ((lax.broadcasted_iota(jnp.int32, (tk, R), 1) & (tq - 1))
           - lax.broadcasted_iota(jnp.int32, (tk, R), 0))

    def body(jj, carry):
        ks = pl.multiple_of(jj * tk, tk)
        kt = k_ref[pl.ds(ks, tk), :].astype(BF16)
        vt_t = vt_ref[:, pl.ds(ks, tk)].astype(BF16)
        dist = rel + (q0 - ks)
        s = jnp.where((dist >= 0) & (dist <= kw), _dot(kt, qt), NEG)
        _online_update_t(s, vt_t, m_ref, l_ref, acc_ref)
        return carry

    lo_t = jnp.maximum(q0 - kw, 0) // tk
    hi_t = (q0 + tq - 1) // tk
    lax.fori_loop(lo_t, hi_t + 1, body, 0)
    _finish_t(o_ref, lse_ref, g_ref, m_ref, l_ref, acc_ref, nh, gqa, tq)


def _band_attention(u, vt, *, q_off, k_off, nh, gqa, n_slabs, kw, dil, want_lse,
                    gates=None, g_off=0, tq=256, tk=256):
    B, S, W = u.shape
    n = S // dil
    qs = nh * HEAD_DIM
    tq = min(tq, n)
    tk = min(tk, n)
    uv = u.reshape(B, n, dil * W)
    wo = n_slabs * qs
    in_specs = [
        pl.BlockSpec((None, tq, qs), lambda b, r, p, i: (b, i, r * (W // qs) + q_off // qs + p)),
        pl.BlockSpec((None, n, LANES), lambda b, r, p, i: (b, 0, r * (W // LANES) + k_off // LANES + p)),
        pl.BlockSpec((None, None, LANES, n), lambda b, r, p, i: (b, r, p, 0)),
    ]
    args = [uv, uv, vt]
    if gates is not None:
        Wg = gates.shape[-1]
        in_specs.append(pl.BlockSpec((None, tq, LANES),
                                     lambda b, r, p, i: (b, i, r * (Wg // LANES) + g_off // LANES + p)))
        args.append(gates.reshape(B, n, dil * Wg))
    o_spec = pl.BlockSpec((None, tq, qs), lambda b, r, p, i: (b, i, r * n_slabs + p))
    o_shape = jax.ShapeDtypeStruct((B, n, dil * wo), F32)
    R = nh * tq
    res = pl.pallas_call(
        functools.partial(_band_kernel, nh=nh, gqa=gqa, tq=tq, tk=tk, kw=kw, want_lse=want_lse,
                          gated=gates is not None),
        grid=(B, dil, n_slabs, n // tq),
        in_specs=in_specs,
        out_specs=[o_spec, o_spec] if want_lse else o_spec,
        out_shape=[o_shape, o_shape] if want_lse else o_shape,
        scratch_shapes=[pltpu.VMEM((1, R), F32), pltpu.VMEM((1, R), F32), pltpu.VMEM((LANES, R), F32)],
        compiler_params=_cparams(("parallel", "parallel", "parallel", "arbitrary")),
    )(*args)
    if want_lse:
        return res[0].reshape(B, S, wo), res[1].reshape(B, S, wo)
    return res.reshape(B, S, wo)


def _values_transposed(u, v_off, width, dil):
    B, S, _ = u.shape
    v = u[:, :, v_off:v_off + width].reshape(B, S // dil, dil, width)
    return jnp.transpose(v, (0, 2, 3, 1))


def _compress_math(x_ref, w1_ref, pe_ref, w1f_ref, w2_ref, bd_ref, kg_ref, is_key, n_chunk):
    hid0 = _dot(pe_ref[...], w1f_ref[...])[0:1]
    acc = jnp.zeros((n_chunk, 4 * CMP_HIDDEN), F32)
    for j in range(CMP_STRIDE):
        xj = x_ref[pl.ds(j, n_chunk, stride=CMP_STRIDE), :].astype(BF16)
        acc = acc + _dot(xj, w1_ref[j])
    out = jnp.zeros((n_chunk, LANES), F32)
    for a in range(2):
        p0 = acc[:, (2 * a) * CMP_HIDDEN:(2 * a + 1) * CMP_HIDDEN]
        p1 = acc[:, (2 * a + 1) * CMP_HIDDEN:(2 * a + 2) * CMP_HIDDEN]
        hid = hid0 + p0 + pltpu.roll(p1, n_chunk - 1, axis=0)
        out = out + _dot(jax.nn.silu(hid).astype(BF16), w2_ref[a])
    normed = _head_norm(out, bd_ref, kg_ref[...])
    return jnp.where(is_key, normed, out)


def _compress_prompt_kernel(x_ref, w1_ref, pe_ref, w1f_ref, w2_ref, bd_ref, kg_ref, o_ref, *, n_chunk):
    is_key = pl.program_id(1) == 0
    o_ref[...] = _compress_math(x_ref, w1_ref, pe_ref, w1f_ref, w2_ref, bd_ref, kg_ref, is_key, n_chunk)


def _compress_weights(pe, w1, w2, k_gain):
    w1r = w1.reshape(2, CMP_RATIO, CMP_STRIDE, HEAD_DIM, CMP_HIDDEN)
    wj = jnp.concatenate([w1r[:, 0], w1r[:, 1]], axis=-1)
    z = jnp.zeros_like(wj)
    w1bd = jnp.concatenate([jnp.concatenate([wj, z], axis=-1),
                            jnp.concatenate([z, wj], axis=-1)], axis=-2).astype(BF16)
    pe8 = jnp.concatenate([pe.reshape(2, 1, CMP_LEN * HEAD_DIM),
                           jnp.zeros((2, 7, CMP_LEN * HEAD_DIM), F32)], axis=1).astype(BF16)
    w1f = w1.reshape(2, CMP_LEN * HEAD_DIM, CMP_HIDDEN).astype(BF16)
    z2 = jnp.zeros_like(w2)
    w2pad = jnp.stack([jnp.concatenate([w2, z2], axis=-1),
                       jnp.concatenate([z2, w2], axis=-1)], axis=1).astype(BF16)
    hid = jnp.arange(LANES) // HEAD_DIM
    bd = ((hid[:, None] == hid[None, :]).astype(F32) / HEAD_DIM).astype(BF16)
    kg = jnp.tile(k_gain.astype(F32), 2)[None]
    return w1bd, pe8, w1f, w2pad, bd, kg


def _cw_specs(nd):
    def sp(shape, fn):
        return pl.BlockSpec(shape, fn)
    if nd == 3:
        return [
            sp((None, CMP_STRIDE, LANES, 4 * CMP_HIDDEN), lambda b, t, p: (t, 0, 0, 0)),
            sp((None, 8, CMP_LEN * HEAD_DIM), lambda b, t, p: (t, 0, 0)),
            sp((None, CMP_LEN * HEAD_DIM, CMP_HIDDEN), lambda b, t, p: (t, 0, 0)),
            sp((None, 2, CMP_HIDDEN, LANES), lambda b, t, p: (t, 0, 0, 0)),
            sp((LANES, LANES), lambda b, t, p: (0, 0)),
            sp((1, LANES), lambda b, t, p: (0, 0)),
        ]
    return [
        sp((None, CMP_STRIDE, LANES, 4 * CMP_HIDDEN), lambda b, t, p, *_: (t, 0, 0, 0)),
        sp((None, 8, CMP_LEN * HEAD_DIM), lambda b, t, p, *_: (t, 0, 0)),
        sp((None, CMP_LEN * HEAD_DIM, CMP_HIDDEN), lambda b, t, p, *_: (t, 0, 0)),
        sp((None, 2, CMP_HIDDEN, LANES), lambda b, t, p, *_: (t, 0, 0, 0)),
        sp((LANES, LANES), lambda b, t, p, *_: (0, 0)),
        sp((1, LANES), lambda b, t, p, *_: (0, 0)),
    ]


def _compress_prompt(u, col0, cw):
    B, S, W = u.shape
    n_chunk = S // CMP_STRIDE
    base = col0 // LANES
    return pl.pallas_call(
        functools.partial(_compress_prompt_kernel, n_chunk=n_chunk),
        grid=(B, 2, 2),
        in_specs=[pl.BlockSpec((None, S, LANES), lambda b, t, p: (b, 0, base + 2 * t + p))] + _cw_specs(3),
        out_specs=pl.BlockSpec((None, None, n_chunk, LANES), lambda b, t, p: (b, t, 0, p)),
        out_shape=jax.ShapeDtypeStruct((B, 2, n_chunk, KVW_B), F32),
        compiler_params=_cparams(("parallel", "parallel", "parallel")),
    )(u, *cw)


def _compress_sample_kernel(pt_ref, pool_ref, w1_ref, pe_ref, w1f_ref, w2_ref, bd_ref, kg_ref, o_ref,
                            raw, xbuf, sem, *, n_pages, n_chunk):
    n, t, p = pl.program_id(0), pl.program_id(1), pl.program_id(2)
    step = (n * 2 + t) * 2 + p
    n_steps = pl.num_programs(0) * 4

    def page_copy(seq, rt, pair, slot, pg):
        page = pt_ref[seq * n_pages + pg]
        row0 = pl.multiple_of(rt * KVW_B + pair * LANES, LANES)
        return pltpu.make_async_copy(pool_ref.at[page, pl.ds(row0, LANES), :], raw.at[slot, pg], sem.at[slot])

    def issue(st, slot):
        seq, rt, pair = st // 4, (st // 2) % 2, st % 2

        def go(pg, c):
            page_copy(seq, rt, pair, slot, pg).start()
            return c
        lax.fori_loop(0, n_pages, go, 0)

    @pl.when(step == 0)
    def _():
        issue(step, 0)

    @pl.when(step + 1 < n_steps)
    def _():
        issue(step + 1, (step + 1) % 2)

    slot = step % 2

    def wait(pg, c):
        page_copy(n, t, p, slot, pg).wait()
        return c
    lax.fori_loop(0, n_pages, wait, 0)

    def to_token_major(pg, c):
        xbuf[pl.ds(pl.multiple_of(pg * PAGE_SIZE, PAGE_SIZE), PAGE_SIZE), :] = raw[slot, pg].T
        return c
    lax.fori_loop(0, n_pages, to_token_major, 0)

    o_ref[...] = _compress_math(xbuf, w1_ref, pe_ref, w1f_ref, w2_ref, bd_ref, kg_ref, t == 0, n_chunk)


def _compress_sample(pool, page_table, cw):
    N, n_pages = page_table.shape
    n_chunk = n_pages * PAGE_SIZE // CMP_STRIDE
    gs = pltpu.PrefetchScalarGridSpec(
        num_scalar_prefetch=1,
        grid=(N, 2, 2),
        in_specs=[pl.BlockSpec(memory_space=pl.ANY)] + _cw_specs(4),
        out_specs=pl.BlockSpec((None, None, n_chunk, LANES), lambda b, t, p, *_: (b, t, 0, p)),
        scratch_shapes=[pltpu.VMEM((2, n_pages, LANES, PAGE_SIZE), F32),
                        pltpu.VMEM((n_pages * PAGE_SIZE, LANES), F32), pltpu.SemaphoreType.DMA((2,))],
    )
    return pl.pallas_call(
        functools.partial(_compress_sample_kernel, n_pages=n_pages, n_chunk=n_chunk),
        grid_spec=gs,
        out_shape=jax.ShapeDtypeStruct((N, 2, n_chunk, KVW_B), F32),
        compiler_params=_cparams(("arbitrary", "arbitrary", "arbitrary")),
    )(page_table.reshape(-1), pool, *cw)


def _cmp_kernel(q_ref, kc_ref, vc_ref, cov_ref, g_ref, o_ref, sel_ref, idx_ref, *,
                tq, n_chunk, nselp, pos_base, top_n):
    nh, gqa = 2 * GQA_B, GQA_B
    i = pl.program_id(2)
    R = nh * tq
    Q = _prep_q(q_ref[...], nh, gqa, tq)
    kc = kc_ref[...].astype(BF16)
    vc = vc_ref[...].astype(BF16)
    s = _nt_dot(Q, kc)
    pos_r = pos_base + i * tq + (lax.broadcasted_iota(jnp.int32, (R, n_chunk), 0) & (tq - 1))
    cend = lax.broadcasted_iota(jnp.int32, (R, n_chunk), 1) * CMP_STRIDE + (CMP_LEN - 1)
    s = jnp.where(cend <= pos_r, s, -jnp.inf)
    m = jnp.max(s, axis=-1, keepdims=True)
    m = jnp.where(m > -jnp.inf, m, 0.0)
    e = jnp.exp(s - m)
    den = jnp.sum(e, axis=-1, keepdims=True)
    p = e / jnp.maximum(den, TINY)
    o = _dot(p.astype(BF16), vc)
    o = _apply_gates(o, g_ref, nh, tq)
    o_ref[...] = _unprep_o(o, nh, gqa, tq)

    cov = cov_ref[...]
    blk = lax.broadcasted_iota(jnp.int32, (nselp, tq), 0)
    pos_c = pos_base + i * tq + lax.broadcasted_iota(jnp.int32, (1, tq), 1)
    cur = pos_c // SEL_BLOCK
    valid = blk <= cur
    forced = (blk == 0) | (blk == cur) | (blk == cur - 1)
    n_forced = jnp.sum(forced.astype(jnp.int32), axis=0, keepdims=True)
    zero_row = jnp.zeros((1, tq), jnp.int32)
    for a in range(2):
        ps = p[(a * gqa) * tq:(a * gqa + 1) * tq]
        for g in range(1, gqa):
            ps = ps + p[(a * gqa + g) * tq:(a * gqa + g + 1) * tq]
        hi = ps.astype(BF16)
        lo = (ps - hi.astype(F32)).astype(BF16)
        imp = _nt_dot(cov, hi) + _nt_dot(cov, lo)
        rem = jnp.where(valid & jnp.logical_not(forced), imp, -1.0)
        sel = forced
        idx_ref[a, pl.ds(0, 1), :] = zero_row
        idx_ref[a, pl.ds(1, 1), :] = cur
        idx_ref[a, pl.ds(2, 1), :] = jnp.maximum(cur - 1, 0)
        for it in range(top_n - 1):
            mx = jnp.max(rem, axis=0, keepdims=True)
            first = jnp.min(jnp.where(rem == mx, blk, nselp), axis=0, keepdims=True)
            active = (it < top_n - n_forced) & (mx >= 0.0)
            pick = (blk == first) & active
            sel = sel | pick
            rem = jnp.where(pick, -1.0, rem)
            idx_ref[a, pl.ds(3 + it, 1), :] = jnp.where(active, first, 0)
        for r in range(3 + top_n - 1, IDX_ROWS):
            idx_ref[a, pl.ds(r, 1), :] = zero_row
        sel_ref[a] = sel.astype(F32)


def _cover_t(n_chunk, n_cmp, n_sel, nselp):
    c = jnp.arange(n_chunk)[None, :]
    j = jnp.arange(nselp)[:, None]
    cov = ((c * CMP_STRIDE <= j * SEL_BLOCK + SEL_BLOCK - 1) & (c * CMP_STRIDE + CMP_LEN - 1 >= j * SEL_BLOCK)
           & (c < n_cmp) & (j < n_sel))
    return cov.astype(BF16)


def _cmp_attention(q, q_off, kvc, gates, g_off, *, tq, pos_base, n_keys):
    N, Tq, W = q.shape
    n_chunk = kvc.shape[2]
    n_cmp = (n_keys - CMP_LEN) // CMP_STRIDE + 1
    n_sel = -(-n_keys // SEL_BLOCK)
    nselp = -(-n_sel // 8) * 8
    top_n = min(SEL_TOPN, n_sel)
    qs = 2 * GQA_B * HEAD_DIM
    Wg = gates.shape[-1]
    cov = _cover_t(n_chunk, n_cmp, n_sel, nselp)
    return pl.pallas_call(
        functools.partial(_cmp_kernel, tq=tq, n_chunk=n_chunk, nselp=nselp, pos_base=pos_base, top_n=top_n),
        grid=(N, 2, Tq // tq),
        in_specs=[
            pl.BlockSpec((None, tq, qs), lambda b, p, i: (b, i, q_off // qs + p)),
            pl.BlockSpec((None, None, n_chunk, LANES), lambda b, p, i: (b, 0, 0, p)),
            pl.BlockSpec((None, None, n_chunk, LANES), lambda b, p, i: (b, 1, 0, p)),
            pl.BlockSpec((nselp, n_chunk), lambda b, p, i: (0, 0)),
            pl.BlockSpec((None, tq, LANES), lambda b, p, i: (b, i, g_off // LANES + p)),
        ],
        out_specs=[
            pl.BlockSpec((None, tq, qs), lambda b, p, i: (b, i, p)),
            pl.BlockSpec((None, 2, nselp, tq), lambda b, p, i: (b, p, 0, i)),
            pl.BlockSpec((None, 2, IDX_ROWS, tq), lambda b, p, i: (b, p, 0, i)),
        ],
        out_shape=[
            jax.ShapeDtypeStruct((N, Tq, WIDTH_B), F32),
            jax.ShapeDtypeStruct((N, KV_HEADS_B, nselp, Tq), F32),
            jax.ShapeDtypeStruct((N, KV_HEADS_B, IDX_ROWS, Tq), jnp.int32),
        ],
        compiler_params=_cparams(("parallel", "parallel", "parallel")),
    )(q, kvc, kvc, cov, gates)


def _sel_prompt_kernel(q_ref, k_ref, vt_ref, sel_ref, g_ref, o_ref, m_ref, l_ref, acc_ref, *, tq, tk):
    nh, gqa = 2 * GQA_B, GQA_B
    i = pl.program_id(2)
    q0 = i * tq
    qt = _q_transposed(q_ref[...], nh, gqa, tq)
    _init_stats(m_ref, l_ref, acc_ref)
    nblk = tk // SEL_BLOCK
    rel = lax.broadcasted_iota(jnp.int32, (tk, tq), 1) - lax.broadcasted_iota(jnp.int32, (tk, tq), 0)

    def body(jj, carry):
        ks = pl.multiple_of(jj * tk, tk)
        kt = k_ref[pl.ds(ks, tk), :].astype(BF16)
        vt_t = vt_ref[:, pl.ds(ks, tk)].astype(BF16)
        causal = rel + (q0 - ks) >= 0
        biases = []
        for a in range(2):
            rows = sel_ref[a, pl.ds(pl.multiple_of(jj * nblk, nblk), nblk), :]
            chosen = jnp.concatenate(
                [jnp.broadcast_to(rows[c:c + 1], (SEL_BLOCK, tq)) for c in range(nblk)], axis=0)
            biases.append(jnp.where((chosen > 0.5) & causal, 0.0, NEG))
        bias = jnp.concatenate([biases[0]] * gqa + [biases[1]] * gqa, axis=1)
        s = _dot(kt, qt) + bias
        _online_update_t(s, vt_t, m_ref, l_ref, acc_ref)
        return carry

    lax.fori_loop(0, (q0 + tq - 1) // tk + 1, body, 0)
    _finish_t(o_ref, None, g_ref, m_ref, l_ref, acc_ref, nh, gqa, tq)


def _sel_prompt(u, vt, sel_t, *, q_off, k_off, g_off, tq=128, tk=512):
    B, S, W = u.shape
    qs = 2 * GQA_B * HEAD_DIM
    nselp = sel_t.shape[2]
    tq = min(tq, S)
    tk = min(tk, S)
    assert tk % SEL_BLOCK == 0 and (tk // SEL_BLOCK) % 8 == 0 and S % tk == 0 and nselp * SEL_BLOCK >= S
    R = 2 * GQA_B * tq
    return pl.pallas_call(
        functools.partial(_sel_prompt_kernel, tq=tq, tk=tk),
        grid=(B, 2, S // tq),
        in_specs=[
            pl.BlockSpec((None, tq, qs), lambda b, p, i: (b, i, q_off // qs + p)),
            pl.BlockSpec((None, S, LANES), lambda b, p, i: (b, 0, k_off // LANES + p)),
            pl.BlockSpec((None, None, LANES, S), lambda b, p, i: (b, 0, p, 0)),
            pl.BlockSpec((None, 2, nselp, tq), lambda b, p, i: (b, p, 0, i)),
            pl.BlockSpec((None, tq, LANES), lambda b, p, i: (b, i, g_off // LANES + p)),
        ],
        out_specs=pl.BlockSpec((None, tq, qs), lambda b, p, i: (b, i, p)),
        out_shape=jax.ShapeDtypeStruct((B, S, WIDTH_B), F32),
        scratch_shapes=[pltpu.VMEM((1, R), F32), pltpu.VMEM((1, R), F32), pltpu.VMEM((LANES, R), F32)],
        compiler_params=_cparams(("parallel", "parallel", "arbitrary")),
    )(u, u, vt, sel_t, u)


def _sel_sample_kernel(pt_ref, idx_ref, q_ref, kn_ref, vn_ref, g_ref, pool_ref, o_ref, kbuf, vbuf, sem, *,
                       nt, n_pages, n_pick):
    nh, gqa, tq = 2 * GQA_B, GQA_B, SAMPLE_ROWS
    n, p, a = pl.program_id(0), pl.program_id(1), pl.program_id(2)
    step = (n * 2 + p) * 2 + a
    n_steps = pl.num_programs(0) * 4
    blocks_per_page = PAGE_SIZE // SEL_BLOCK
    width = n_pick * PAGE_SIZE

    def picked_block(seq, pair, head, t, r):
        src_row = jnp.where(r == 0, 0, r + 1)
        return idx_ref[((seq * KV_HEADS_B + pair * 2 + head) * IDX_ROWS + src_row) * tq + t]

    def copies(st, slot, c):
        seq, pair, head = st // 4, (st // 2) % 2, st % 2
        t, r = c // n_pick, c % n_pick
        page = pt_ref[seq * n_pages + picked_block(seq, pair, head, t, r) // blocks_per_page]
        dst0 = pl.multiple_of(r * PAGE_SIZE, PAGE_SIZE)
        out = []
        for row_type, buf in ((2, kbuf), (3, vbuf)):
            row0 = pl.multiple_of(row_type * KVW_B + pair * LANES, LANES)
            out.append(pltpu.make_async_copy(pool_ref.at[page, pl.ds(row0, LANES), :],
                                             buf.at[slot, t, :, pl.ds(dst0, PAGE_SIZE)], sem.at[slot]))
        return out

    def issue(st, slot):
        def go(c, carry):
            for cp in copies(st, slot, c):
                cp.start()
            return carry
        lax.fori_loop(0, nt * n_pick, go, 0)

    @pl.when(step == 0)
    def _():
        issue(step, 0)

    @pl.when(step + 1 < n_steps)
    def _():
        issue(step + 1, (step + 1) % 2)

    slot = step % 2

    def wait(c, carry):
        for cp in copies(step, slot, c):
            cp.wait()
        return carry
    lax.fori_loop(0, nt * n_pick, wait, 0)

    R = nh * tq
    Q = _prep_q(q_ref[...], nh, gqa, tq)
    pad = jnp.zeros((LANES - tq, LANES), F32)
    kn = jnp.concatenate([kn_ref[...], pad], axis=0).astype(BF16)
    vn = jnp.concatenate([vn_ref[...], pad], axis=0).astype(BF16)
    row = lax.broadcasted_iota(jnp.int32, (R, LANES), 0)
    colk = lax.broadcasted_iota(jnp.int32, (R, LANES), 1)
    trow = row & (tq - 1)
    s_new = jnp.where((colk <= trow) & (colk < nt), _nt_dot(Q, kn), NEG)
    lane = lax.broadcasted_iota(jnp.int32, (1, width), 1)
    lane_pick = lane // PAGE_SIZE
    lane_half = (lane // SEL_BLOCK) % blocks_per_page
    o = jnp.zeros((R, LANES), F32)
    for t in range(nt):
        chosen = lane < 0
        for r in range(n_pick):
            half = picked_block(n, p, a, t, r) % blocks_per_page
            chosen = chosen | ((lane_pick == r) & (lane_half == half))
        s = jnp.where(chosen, _dot(Q, kbuf[slot, t].astype(BF16)), NEG)
        m = jnp.maximum(jnp.max(s, axis=-1, keepdims=True), jnp.max(s_new, axis=-1, keepdims=True))
        p1 = jnp.exp(s - m)
        p2 = jnp.exp(s_new - m)
        l = jnp.sum(p1, axis=-1, keepdims=True) + jnp.sum(p2, axis=-1, keepdims=True)
        o_t = (_nt_dot(p1.astype(BF16), vbuf[slot, t].astype(BF16)) + _dot(p2.astype(BF16), vn)) / l
        o = jnp.where(trow == t, o_t, o)

    g = g_ref[...]
    low = lax.broadcasted_iota(jnp.int32, (tq, LANES), 1) < HEAD_DIM
    first = a == 0
    placed = []
    for jj in range(gqa):
        oj = jnp.where(first, o[jj * tq:(jj + 1) * tq], o[(gqa + jj) * tq:(gqa + jj + 1) * tq])
        gj = jnp.where(first, g[:, jj:jj + 1], g[:, gqa + jj:gqa + jj + 1])
        oj = oj * gj
        placed.append(jnp.where(a == jj % 2, oj, pltpu.roll(oj, HEAD_DIM, axis=1)))
    o_ref[...] = jnp.concatenate([jnp.where(low, placed[0], placed[1]),
                                  jnp.where(low, placed[2], placed[3])], axis=1)


def _sel_sample(us, idx, pool, page_table, *, nt, q_off, k_off, v_off, g_off):
    N, tq, W = us.shape
    n_pages = page_table.shape[1]
    n_pick = SEL_TOPN - 1
    qs = 2 * GQA_B * HEAD_DIM
    hw = GQA_B * HEAD_DIM
    gs = pltpu.PrefetchScalarGridSpec(
        num_scalar_prefetch=2,
        grid=(N, 2, 2),
        in_specs=[
            pl.BlockSpec((None, tq, qs), lambda b, p, a, *_: (b, 0, q_off // qs + p)),
            pl.BlockSpec((None, tq, LANES), lambda b, p, a, *_: (b, 0, k_off // LANES + p)),
            pl.BlockSpec((None, tq, LANES), lambda b, p, a, *_: (b, 0, v_off // LANES + p)),
            pl.BlockSpec((None, tq, LANES), lambda b, p, a, *_: (b, 0, g_off // LANES + p)),
            pl.BlockSpec(memory_space=pl.ANY),
        ],
        out_specs=pl.BlockSpec((None, tq, hw), lambda b, p, a, *_: (b, 0, 2 * p + a)),
        scratch_shapes=[
            pltpu.VMEM((2, nt, LANES, n_pick * PAGE_SIZE), F32),
            pltpu.VMEM((2, nt, LANES, n_pick * PAGE_SIZE), F32),
            pltpu.SemaphoreType.DMA((2,)),
        ],
    )
    return pl.pallas_call(
        functools.partial(_sel_sample_kernel, nt=nt, n_pages=n_pages, n_pick=n_pick),
        grid_spec=gs,
        out_shape=jax.ShapeDtypeStruct((N, tq, WIDTH_B), F32),
        compiler_params=_cparams(("arbitrary", "arbitrary", "arbitrary")),
    )(page_table.reshape(-1), idx.reshape(-1), us, us, us, us, pool)


def _cache_attn_kernel(*refs, nh, gqa, L, dil, win, nt, want_lse, gated):
    q_ref, kc_ref, vc_ref, kn_ref, vn_ref = refs[:5]
    pos = 5
    g_ref = None
    if gated:
        g_ref = refs[pos]
        pos += 1
    o_ref = refs[pos]
    lse_ref = refs[pos + 1] if want_lse else None
    tq = SAMPLE_ROWS
    R = nh * tq
    Q = _prep_q(q_ref[...], nh, gqa, tq)
    kc_t = kc_ref[...].astype(BF16)
    vc_t = vc_ref[...].astype(BF16)
    pad = jnp.zeros((LANES - tq, LANES), F32)
    kn = jnp.concatenate([kn_ref[...], pad], axis=0).astype(BF16)
    vn = jnp.concatenate([vn_ref[...], pad], axis=0).astype(BF16)
    t1 = lax.broadcasted_iota(jnp.int32, (R, L), 0) & (tq - 1)
    d1 = L + t1 - lax.broadcasted_iota(jnp.int32, (R, L), 1)
    s1 = jnp.where(((d1 & (dil - 1)) == 0) & (d1 <= win), _dot(Q, kc_t), NEG)
    t2 = lax.broadcasted_iota(jnp.int32, (R, LANES), 0) & (tq - 1)
    c2 = lax.broadcasted_iota(jnp.int32, (R, LANES), 1)
    d2 = t2 - c2
    s2 = jnp.where((d2 >= 0) & ((d2 & (dil - 1)) == 0) & (d2 <= win) & (c2 < nt), _nt_dot(Q, kn), NEG)
    m = jnp.maximum(jnp.max(s1, axis=-1, keepdims=True), jnp.max(s2, axis=-1, keepdims=True))
    p1 = jnp.exp(s1 - m)
    p2 = jnp.exp(s2 - m)
    l = jnp.sum(p1, axis=-1, keepdims=True) + jnp.sum(p2, axis=-1, keepdims=True)
    o = (_nt_dot(p1.astype(BF16), vc_t) + _dot(p2.astype(BF16), vn)) / l
    o = _apply_gates(o, g_ref, nh, tq)
    o_ref[...] = _unprep_o(o, nh, gqa, tq)
    if want_lse:
        lse_ref[...] = _unprep_o(jnp.broadcast_to(m + jnp.log(l), (R, LANES)), nh, gqa, tq)


def _feature_major(x):
    nd = x.ndim
    xt = jnp.transpose(x, (0, 1) + tuple(range(3, nd)) + (2,))
    return xt.reshape(x.shape[0], x.shape[1], -1, x.shape[2])


def _cache_attention(us, cache_t, layer, *, q_off, k_off, v_off, nh, gqa, n_slabs, dil, win, nt, want_lse,
                     g_off=None):
    N, tq, W = us.shape
    L = cache_t.shape[3]
    cache = cache_t
    qs = nh * HEAD_DIM
    wo = n_slabs * qs
    in_specs = [
        pl.BlockSpec((None, tq, qs), lambda b, p: (b, 0, q_off // qs + p)),
        pl.BlockSpec((None, None, LANES, L), lambda b, p: (layer, b, p, 0)),
        pl.BlockSpec((None, None, LANES, L), lambda b, p: (layer, b, n_slabs + p, 0)),
        pl.BlockSpec((None, tq, LANES), lambda b, p: (b, 0, k_off // LANES + p)),
        pl.BlockSpec((None, tq, LANES), lambda b, p: (b, 0, v_off // LANES + p)),
    ]
    args = [us, cache, cache, us, us]
    if g_off is not None:
        in_specs.append(pl.BlockSpec((None, tq, LANES), lambda b, p: (b, 0, g_off // LANES + p)))
        args.append(us)
    o_spec = pl.BlockSpec((None, tq, qs), lambda b, p: (b, 0, p))
    o_shape = jax.ShapeDtypeStruct((N, tq, wo), F32)
    return pl.pallas_call(
        functools.partial(_cache_attn_kernel, nh=nh, gqa=gqa, L=L, dil=dil, win=win, nt=nt,
                          want_lse=want_lse, gated=g_off is not None),
        grid=(N, n_slabs),
        in_specs=in_specs,
        out_specs=[o_spec, o_spec] if want_lse else o_spec,
        out_shape=[o_shape, o_shape] if want_lse else o_shape,
        compiler_params=_cparams(("parallel", "parallel")),
    )(*args)


def _merge_a_kernel(o0, o1, o2, l0, l1, l2, z_ref, x_ref, w_ref, y_ref):
    a0, a1, a2 = l0[...], l1[...], l2[...]
    mx = jnp.maximum(jnp.maximum(a0, a1), a2)
    e0, e1, e2 = jnp.exp(a0 - mx), jnp.exp(a1 - mx), jnp.exp(a2 - mx)
    den = e0 + e1 + e2
    o = (e0 / den) * o0[...] + (e1 / den) * o1[...] + (e2 / den) * o2[...]
    y_ref[...] = x_ref[...] + _dot((o * jax.nn.silu(z_ref[...])).astype(BF16), w_ref[...])


def _merge_a(outs, lses, u, z_off, x, w_out_bf16):
    T, D = x.shape
    tm = min(T, 512)
    wa = WIDTH_A
    row = pl.BlockSpec((tm, wa), lambda i: (i, 0))
    return pl.pallas_call(
        _merge_a_kernel,
        grid=(T // tm,),
        in_specs=[row] * 6 + [
            pl.BlockSpec((tm, wa), lambda i: (i, z_off // wa)),
            pl.BlockSpec((tm, D), lambda i: (i, 0)),
            pl.BlockSpec((wa, D), lambda i: (0, 0)),
        ],
        out_specs=pl.BlockSpec((tm, D), lambda i: (i, 0)),
        out_shape=jax.ShapeDtypeStruct((T, D), F32),
        compiler_params=_cparams(("parallel",)),
    )(*outs, *lses, u, x, w_out_bf16)


def _merge_b_kernel(oc, os_, ow, z_ref, x_ref, w_ref, y_ref):
    o = oc[...] + os_[...] + ow[...]
    y_ref[...] = x_ref[...] + _dot((o * jax.nn.silu(z_ref[...])).astype(BF16), w_ref[...])


def _merge_b(o_c, o_s, o_w, u, z_off, x, w_out_bf16):
    T, D = x.shape
    tm = min(T, 512)
    wb = WIDTH_B
    row = pl.BlockSpec((tm, wb), lambda i: (i, 0))
    return pl.pallas_call(
        _merge_b_kernel,
        grid=(T // tm,),
        in_specs=[row] * 3 + [
            pl.BlockSpec((tm, wb), lambda i: (i, z_off // wb)),
            pl.BlockSpec((tm, D), lambda i: (i, 0)),
            pl.BlockSpec((wb, D), lambda i: (0, 0)),
        ],
        out_specs=pl.BlockSpec((tm, D), lambda i: (i, 0)),
        out_shape=jax.ShapeDtypeStruct((T, D), F32),
        compiler_params=_cparams(("parallel",)),
    )(o_c, o_s, o_w, u, x, w_out_bf16)


A_KINDS = ([EP_ROPE_Q] * (N_GROUPS_A * WIDTH_A // PROJ_TN) + [EP_ROPE_K] * (N_GROUPS_A * WIDTH_A // PROJ_TN)
           + [EP_NONE] * (N_GROUPS_A * WIDTH_A // PROJ_TN) + [EP_NONE] * (WIDTH_A // PROJ_TN))
A_K, A_V, A_Z = N_GROUPS_A * WIDTH_A, 2 * N_GROUPS_A * WIDTH_A, QKV_A

B_KINDS = ([EP_NORM_Q] * 4 + [EP_ROPE_Q] * 4 + [EP_NONE] * 4
           + [EP_NORM_K, EP_NONE, EP_ROPE_K, EP_NONE, EP_ROPE_K, EP_NONE] + [EP_SIG] * 3)


def _b_weight(w_in):
    wq = w_in[:, :WIDTH_B]
    o1 = WIDTH_B + 6 * KVW_B
    wkv = w_in[:, WIDTH_B:o1]
    wg = w_in[:, o1:o1 + 3 * HEADS_B]
    wz = w_in[:, o1 + 3 * HEADS_B:]
    nh = 2 * GQA_B
    tiles = []
    for b in range(3):
        for p in range(2):
            g = wg[:, b * HEADS_B + p * nh: b * HEADS_B + (p + 1) * nh]
            tiles.append(jnp.pad(g, ((0, 0), (0, LANES - nh))))
    return jnp.concatenate([wq, wq, wz, wkv] + tiles, axis=1).astype(BF16)


def _pad_rows(u, N, T):
    return jnp.pad(u.reshape(N, T, -1), ((0, 0), (0, SAMPLE_ROWS - T), (0, 0)))


def _layer_a_prompt(x, norm_g, w_bf16, q_gain, k_gain, w_out_bf16, tabs):
    B, S, D = x.shape
    u = _project(x.reshape(B * S, D), norm_g, w_bf16, A_KINDS, tabs[0], tabs[1], q_gain, k_gain)
    u3 = u.reshape(B, S, -1)
    outs, lses, states = [], [], []
    for g, (win, dil) in enumerate(DIL_PATTERNS):
        vt = _values_transposed(u3, A_V + g * WIDTH_A, WIDTH_A, dil)
        if dil == 1:
            qk, q_off, k_off = u3, g * WIDTH_A, A_K + g * WIDTH_A
        else:
            qk = jnp.concatenate([u3[:, :, g * WIDTH_A:(g + 1) * WIDTH_A],
                                  u3[:, :, A_K + g * WIDTH_A:A_K + (g + 1) * WIDTH_A]], axis=2)
            q_off, k_off = 0, WIDTH_A
        o, lse = _band_attention(qk, vt, q_off=q_off, k_off=k_off,
                                 nh=2, gqa=1, n_slabs=HEADS_A // 2, kw=win // dil, dil=dil, want_lse=True)
        outs.append(o.reshape(B * S, WIDTH_A))
        lses.append(lse.reshape(B * S, WIDTH_A))
        w = min(win, S)
        kg = u3[:, S - w:, A_K + g * WIDTH_A: A_K + (g + 1) * WIDTH_A]
        vg = u3[:, S - w:, A_V + g * WIDTH_A: A_V + (g + 1) * WIDTH_A]
        states.append(jnp.stack([kg, vg], axis=2).reshape(B, w, 2, HEADS_A, HEAD_DIM))
    y = _merge_a(outs, lses, u, A_Z, x.reshape(B * S, D), w_out_bf16)
    return y.reshape(B, S, D), states


def _layer_a_sample(x, caches_t, layer, norm_g, w_bf16, q_gain, k_gain, w_out_bf16, tabs):
    N, T, D = x.shape
    u = _project(x.reshape(N * T, D), norm_g, w_bf16, A_KINDS, tabs[0], tabs[1], q_gain, k_gain)
    us = _pad_rows(u, N, T)
    outs, lses, new_rows = [], [], []
    for g, (win, dil) in enumerate(DIL_PATTERNS):
        o, lse = _cache_attention(us, caches_t[g], layer, q_off=g * WIDTH_A, k_off=A_K + g * WIDTH_A,
                                  v_off=A_V + g * WIDTH_A, nh=2, gqa=1, n_slabs=HEADS_A // 2,
                                  dil=dil, win=win, nt=T, want_lse=True)
        outs.append(o[:, :T].reshape(N * T, WIDTH_A))
        lses.append(lse[:, :T].reshape(N * T, WIDTH_A))
        new_rows.append(jnp.concatenate([us[:, :, A_K + g * WIDTH_A: A_K + (g + 1) * WIDTH_A],
                                         us[:, :, A_V + g * WIDTH_A: A_V + (g + 1) * WIDTH_A]], axis=2))
    y = _merge_a(outs, lses, u, A_Z, x.reshape(N * T, D), w_out_bf16)
    return y.reshape(N, T, D), new_rows


def _layer_b_prompt(x, norm_g, w_bf16, q_gain, k_gain, cw, w_out_bf16, tabs):
    B, S, D = x.shape
    u = _project(x.reshape(B * S, D), norm_g, w_bf16, B_KINDS, tabs[0], tabs[1], q_gain, k_gain)
    u3 = u.reshape(B, S, -1)
    kvc = _compress_prompt(u3, B_KV, cw)
    o_c, sel_t, _ = _cmp_attention(u3, B_QC, kvc, u3, B_GATE, tq=min(128, S), pos_base=0, n_keys=S)
    o_s = _sel_prompt(u3, _values_transposed(u3, B_KV + 3 * KVW_B, KVW_B, 1), sel_t, q_off=B_QR,
                      k_off=B_KV + 2 * KVW_B, g_off=B_GATE + 2 * LANES)
    o_w = _band_attention(u3, _values_transposed(u3, B_WIN + KVW_B, KVW_B, 1), q_off=B_QR, k_off=B_WIN,
                          nh=2 * GQA_B, gqa=GQA_B, n_slabs=2, kw=WIN_B, dil=1, want_lse=False, gates=u3,
                          g_off=B_GATE + 4 * LANES, tq=128, tk=256)
    y = _merge_b(o_c.reshape(B * S, -1), o_s.reshape(B * S, -1), o_w.reshape(B * S, -1), u, B_Z,
                 x.reshape(B * S, D), w_out_bf16)
    rows = u3[:, :, B_KV:B_KV + 4 * KVW_B].reshape(B, S, 4, KV_HEADS_B, HEAD_DIM)
    w = min(WIN_B, S)
    wrows = u3[:, S - w:, B_WIN:B_WIN + 2 * KVW_B].reshape(B, w, 2, KV_HEADS_B, HEAD_DIM)
    return y.reshape(B, S, D), rows, wrows


def _layer_b_sample(x, pool_t, page_table, win_t, layer, norm_g, w_bf16, q_gain, k_gain, cw, w_out_bf16, tabs):
    N, T, D = x.shape
    past_len = page_table.shape[1] * PAGE_SIZE
    assert past_len % SEL_BLOCK == 0 and T <= SEL_BLOCK and T <= SAMPLE_ROWS
    u = _project(x.reshape(N * T, D), norm_g, w_bf16, B_KINDS, tabs[0], tabs[1], q_gain, k_gain)
    us = _pad_rows(u, N, T)
    u3 = u.reshape(N, T, -1)
    kvc = _compress_sample(pool_t, page_table, cw)
    o_c, _, idx = _cmp_attention(us, B_QC, kvc, us, B_GATE, tq=SAMPLE_ROWS, pos_base=past_len,
                                 n_keys=past_len + T)
    o_s = _sel_sample(us, idx, pool_t, page_table, nt=T, q_off=B_QR, k_off=B_KV + 2 * KVW_B,
                      v_off=B_KV + 3 * KVW_B, g_off=B_GATE + 2 * LANES)
    o_w = _cache_attention(us, win_t, layer, q_off=B_QR, k_off=B_WIN, v_off=B_WIN + KVW_B,
                           nh=2 * GQA_B, gqa=GQA_B, n_slabs=2, dil=1, win=WIN_B, nt=T, want_lse=False,
                           g_off=B_GATE + 4 * LANES)
    y = _merge_b(o_c[:, :T].reshape(N * T, -1), o_s[:, :T].reshape(N * T, -1), o_w[:, :T].reshape(N * T, -1),
                 u, B_Z, x.reshape(N * T, D), w_out_bf16)
    rows = u3[:, :, B_KV:B_KV + 4 * KVW_B].reshape(N, T, 4, KV_HEADS_B, HEAD_DIM)
    return y.reshape(N, T, D), rows, us[:, :, B_WIN:B_WIN + 2 * KVW_B]


def _append_kernel(c_ref, n_ref, o_ref, *, L, nt):
    rolled = pltpu.roll(c_ref[...], L - nt, axis=1)
    pad = jnp.zeros((LANES - SAMPLE_ROWS, LANES), F32)
    new_t = pltpu.roll(jnp.concatenate([n_ref[...], pad], axis=0).T, LANES - nt, axis=1)
    lane = lax.broadcasted_iota(jnp.int32, (LANES, LANES), 1)
    if L > LANES:
        o_ref[:, :L - LANES] = rolled[:, :L - LANES]
    o_ref[:, L - LANES:] = jnp.where(lane >= LANES - nt, new_t, rolled[:, L - LANES:])


def _append_cache(cache, cache_t, new_rows, nt, win):
    J, N, F, L = cache_t.shape
    tail = cache.shape[3:]
    if L + nt <= win or L % LANES:
        new = new_rows[:, :, :nt].reshape((J, N, nt) + tail)
        return jnp.concatenate([cache, new], axis=2)[:, :, -min(win, L + nt):]
    assert L == win
    out_t = pl.pallas_call(
        functools.partial(_append_kernel, L=L, nt=nt),
        grid=(J, N, F // LANES),
        in_specs=[pl.BlockSpec((None, None, LANES, L), lambda j, n, f: (j, n, f, 0)),
                  pl.BlockSpec((None, None, SAMPLE_ROWS, LANES), lambda j, n, f: (j, n, 0, f))],
        out_specs=pl.BlockSpec((None, None, LANES, L), lambda j, n, f: (j, n, f, 0)),
        out_shape=jax.ShapeDtypeStruct((J, N, F, L), F32),
        compiler_params=_cparams(("parallel", "parallel", "parallel")),
    )(cache_t, new_rows)
    nd = len(tail)
    out = out_t.reshape((J, N) + tail + (L,))
    return jnp.transpose(out, (0, 1, nd + 2) + tuple(range(2, nd + 2)))


def kernel(x_prompt, x_sample, cache_dil_0, cache_dil_1, cache_dil_2, cache_nsa_paged, cache_nsa_win,
           page_table, a_norm, a_w_in, a_q_norm, a_k_norm, a_w_out, b_norm, b_w_in, b_q_norm, b_k_norm,
           b_cmp_pe, b_cmp_w1, b_cmp_w2, b_w_out):
    dil_caches = (cache_dil_0, cache_dil_1, cache_dil_2)
    B, S, _ = x_prompt.shape
    N, T, _ = x_sample.shape
    past_len = page_table.shape[1] * PAGE_SIZE
    depth = a_norm.shape[0] + b_norm.shape[0]
    n_pool = cache_nsa_paged.shape[1]
    tabs_p = _rope_tables(jnp.tile(jnp.arange(S), B))
    tabs_s = _rope_tables(jnp.tile(past_len + jnp.arange(T), N))
    dil_t = [_feature_major(c) for c in dil_caches]
    win_t = _feature_major(cache_nsa_win)
    pool_t = _feature_major(cache_nsa_paged)
    pool_t = pool_t.reshape((-1,) + pool_t.shape[2:])
    xp, xs = x_prompt, x_sample
    dil_p = [[] for _ in DIL_PATTERNS]
    dil_new = [[] for _ in DIL_PATTERNS]
    rows_p, rows_s, win_p, win_new = [], [], [], []
    for layer in range(depth):
        j = layer // 2
        if layer % 2 == 0:
            w = a_w_in[j].astype(BF16)
            wo = a_w_out[j].astype(BF16)
            xp, st_p = _layer_a_prompt(xp, a_norm[j], w, a_q_norm[j], a_k_norm[j], wo, tabs_p)
            xs, new_s = _layer_a_sample(xs, dil_t, j, a_norm[j], w, a_q_norm[j], a_k_norm[j], wo, tabs_s)
            for g in range(N_GROUPS_A):
                dil_p[g].append(st_p[g])
                dil_new[g].append(new_s[g])
        else:
            w = _b_weight(b_w_in[j])
            wo = b_w_out[j].astype(BF16)
            cw = _compress_weights(b_cmp_pe[j], b_cmp_w1[j], b_cmp_w2[j], b_k_norm[j])
            xp, rp, wp = _layer_b_prompt(xp, b_norm[j], w, b_q_norm[j], b_k_norm[j], cw, wo, tabs_p)
            xs, rs, wn = _layer_b_sample(xs, pool_t, page_table + j * n_pool, win_t, j, b_norm[j], w,
                                         b_q_norm[j], b_k_norm[j], cw, wo, tabs_s)
            rows_p.append(rp)
            win_p.append(wp)
            rows_s.append(rs)
            win_new.append(wn)
    dil_s = [_append_cache(dil_caches[g], dil_t[g], jnp.stack(dil_new[g]), T, DIL_PATTERNS[g][0])
             for g in range(N_GROUPS_A)]
    win_s = _append_cache(cache_nsa_win, win_t, jnp.stack(win_new), T, WIN_B)
    return (xp, xs,
            jnp.stack(dil_p[0]), jnp.stack(dil_p[1]), jnp.stack(dil_p[2]),
            jnp.stack(rows_p), jnp.stack(win_p),
            dil_s[0], dil_s[1], dil_s[2],
            jnp.stack(rows_s), win_s)
```

```python
import functools

import jax
import jax.numpy as jnp
from jax import lax
from jax.experimental import pallas as pl
from jax.experimental.pallas import tpu as pltpu

F32 = jnp.float32
BF16 = jnp.bfloat16

HEAD_DIM = 64
SCALE = HEAD_DIM ** -0.5
ROPE_THETA = 10000.0
EPS = 1e-6
TINY = 1e-30
NEG = -1e30
DIL_PATTERNS = ((128, 1), (512, 4), (2048, 16))
N_GROUPS_A = len(DIL_PATTERNS)
HEADS_A = 8
WIDTH_A = HEADS_A * HEAD_DIM
QKV_A = 3 * N_GROUPS_A * WIDTH_A
KV_HEADS_B = 4
GQA_B = 4
HEADS_B = KV_HEADS_B * GQA_B
WIDTH_B = HEADS_B * HEAD_DIM
KVW_B = KV_HEADS_B * HEAD_DIM
CMP_LEN = 32
CMP_STRIDE = 16
CMP_RATIO = CMP_LEN // CMP_STRIDE
CMP_HIDDEN = 128
SEL_BLOCK = 64
SEL_TOPN = 16
WIN_B = 512
PAGE_SIZE = 128

LANES = 128
PROJ_TN = 256
VMEM_LIMIT = 56 * 1024 * 1024
IDX_ROWS = 24
SAMPLE_ROWS = 8

B_QC, B_QR, B_Z, B_KV, B_WIN, B_GATE = 0, 1024, 2048, 3072, 4096, 4608
B_COLS = B_GATE + 6 * LANES

EP_NONE, EP_NORM_Q, EP_ROPE_Q, EP_NORM_K, EP_ROPE_K, EP_SIG = range(6)


def _cparams(sem):
    return pltpu.CompilerParams(dimension_semantics=sem, vmem_limit_bytes=VMEM_LIMIT)


def _nt_dot(a, b):
    return lax.dot_general(a, b, (((1,), (1,)), ((), ())), preferred_element_type=F32)


def _dot(a, b):
    return jnp.dot(a, b, preferred_element_type=F32)


def _head_norm(acc, bd_ref, gain):
    msq = _dot((acc * acc).astype(BF16), bd_ref[...])
    return acc * lax.rsqrt(msq + EPS) * gain


def _rope_tile(y, cos, sin_signed):
    lane = lax.broadcasted_iota(jnp.int32, (y.shape[0], LANES), 1)
    first_half = (lane & (HEAD_DIM - 1)) < (HEAD_DIM // 2)
    outs = []
    for c in range(y.shape[1] // LANES):
        yc = y[:, c * LANES:(c + 1) * LANES]
        partner = jnp.where(first_half, pltpu.roll(yc, LANES - HEAD_DIM // 2, axis=1),
                            pltpu.roll(yc, HEAD_DIM // 2, axis=1))
        outs.append(yc * cos + partner * sin_signed)
    return jnp.concatenate(outs, axis=1)


def _proj_kernel(x_ref, g_ref, w_ref, cos_ref, sin_ref, qg_ref, kg_ref, bd_ref, o_ref, h_ref, *, kinds):
    j = pl.program_id(1)

    @pl.when(j == 0)
    def _():
        x = x_ref[...]
        r = lax.rsqrt(jnp.mean(x * x, axis=-1, keepdims=True) + EPS)
        h_ref[...] = (x * r * g_ref[...]).astype(BF16)

    acc = _dot(h_ref[...], w_ref[...])

    def ranges(kind):
        out, start = [], None
        for t, k in enumerate(list(kinds) + [None]):
            if k == kind and start is None:
                start = t
            if k != kind and start is not None:
                out.append((start, t))
                start = None
        return out

    def emit(kind, fn):
        rs = ranges(kind)
        if not rs:
            return
        cond = None
        for a, b in rs:
            c = (j >= a) & (j < b)
            cond = c if cond is None else (cond | c)

        @pl.when(cond)
        def _():
            o_ref[...] = fn()

    emit(EP_NONE, lambda: acc)
    emit(EP_SIG, lambda: jax.nn.sigmoid(acc))
    emit(EP_NORM_Q, lambda: _head_norm(acc, bd_ref, qg_ref[...]))
    emit(EP_NORM_K, lambda: _head_norm(acc, bd_ref, kg_ref[...]))
    emit(EP_ROPE_Q, lambda: _rope_tile(_head_norm(acc, bd_ref, qg_ref[...]), cos_ref[...], sin_ref[...]))
    emit(EP_ROPE_K, lambda: _rope_tile(_head_norm(acc, bd_ref, kg_ref[...]), cos_ref[...], sin_ref[...]))


def _project(x, norm_g, w_bf16, kinds, cos_t, sin_t, q_gain, k_gain):
    T, D = x.shape
    E = w_bf16.shape[1]
    tn = PROJ_TN
    assert E == len(kinds) * tn
    tm = min(T, 1024)
    assert T % tm == 0
    rep = tn // HEAD_DIM
    qg = jnp.tile(q_gain.astype(F32), rep)[None]
    kg = jnp.tile(k_gain.astype(F32), rep)[None]
    hid = jnp.arange(tn) // HEAD_DIM
    bd = ((hid[:, None] == hid[None, :]).astype(F32) / HEAD_DIM).astype(BF16)
    return pl.pallas_call(
        functools.partial(_proj_kernel, kinds=tuple(kinds)),
        grid=(T // tm, E // tn),
        in_specs=[
            pl.BlockSpec((tm, D), lambda i, j: (i, 0)),
            pl.BlockSpec((1, D), lambda i, j: (0, 0)),
            pl.BlockSpec((D, tn), lambda i, j: (0, j)),
            pl.BlockSpec((tm, LANES), lambda i, j: (i, 0)),
            pl.BlockSpec((tm, LANES), lambda i, j: (i, 0)),
            pl.BlockSpec((1, tn), lambda i, j: (0, 0)),
            pl.BlockSpec((1, tn), lambda i, j: (0, 0)),
            pl.BlockSpec((tn, tn), lambda i, j: (0, 0)),
        ],
        out_specs=pl.BlockSpec((tm, tn), lambda i, j: (i, j)),
        out_shape=jax.ShapeDtypeStruct((T, E), F32),
        scratch_shapes=[pltpu.VMEM((tm, D), BF16)],
        compiler_params=_cparams(("parallel", "arbitrary")),
    )(x, norm_g.astype(F32)[None], w_bf16, cos_t, sin_t, qg, kg, bd)


def _rope_tables(pos):
    half = HEAD_DIM // 2
    inv_freq = ROPE_THETA ** (-jnp.arange(half, dtype=F32) / half)
    ang = pos.astype(F32)[:, None] * inv_freq[None, :]
    cos, sin = jnp.cos(ang), jnp.sin(ang)
    cos_t = jnp.concatenate([cos, cos, cos, cos], axis=1)
    sin_t = jnp.concatenate([-sin, sin, -sin, sin], axis=1)
    return cos_t, sin_t


def _prep_q(q, nh, gqa, tq):
    lane = lax.broadcasted_iota(jnp.int32, (tq, LANES), 1)
    low = lane < HEAD_DIM
    parts = []
    for j in range(nh):
        col = q[:, (j // 2) * LANES:(j // 2 + 1) * LANES]
        nat, tgt = j % 2, (j // gqa) % 2
        if nat != tgt:
            col = pltpu.roll(col, HEAD_DIM, axis=1)
        keep = low if tgt == 0 else jnp.logical_not(low)
        parts.append(jnp.where(keep, col * SCALE, 0.0))
    return jnp.concatenate(parts, axis=0).astype(BF16)


def _unprep_o(o, nh, gqa, tq):
    lane = lax.broadcasted_iota(jnp.int32, (tq, LANES), 1)
    low = lane < HEAD_DIM
    cols = []
    for c in range(nh // 2):
        halves = []
        for nat in (0, 1):
            j = 2 * c + nat
            tgt = (j // gqa) % 2
            oj = o[j * tq:(j + 1) * tq]
            if tgt != nat:
                oj = pltpu.roll(oj, HEAD_DIM, axis=1)
            halves.append(oj)
        cols.append(jnp.where(low, halves[0], halves[1]))
    return jnp.concatenate(cols, axis=1)


def _apply_gates(o, g_ref, nh, tq):
    if g_ref is None:
        return o
    g = g_ref[...]
    return jnp.concatenate([o[j * tq:(j + 1) * tq] * g[:, j:j + 1] for j in range(nh)], axis=0)


def _q_transposed(q, nh, gqa, tq):
    qt = (q * SCALE).T
    zero = jnp.zeros((HEAD_DIM, tq), F32)
    cols = []
    for j in range(nh):
        h = qt[j * HEAD_DIM:(j + 1) * HEAD_DIM]
        cols.append(jnp.concatenate([h, zero] if (j // gqa) % 2 == 0 else [zero, h], axis=0))
    return jnp.concatenate(cols, axis=1).astype(BF16)


def _masked_heads(s, masks, tq):
    return jnp.concatenate([jnp.where(mk, s[:, j * tq:(j + 1) * tq], NEG) for j, mk in enumerate(masks)], axis=1)


def _attend_tile_t(kt, vt_t, qt, masks, tq, m_ref, l_ref, acc_ref):
    s = _masked_heads(_dot(kt, qt), masks, tq)
    m_old = m_ref[...]
    m_new = jnp.maximum(m_old, jnp.max(s, axis=0, keepdims=True))
    alpha = jnp.exp(m_old - m_new)
    p = jnp.exp(s - m_new)
    l_ref[...] = alpha * l_ref[...] + jnp.sum(p, axis=0, keepdims=True)
    acc_ref[...] = alpha * acc_ref[...] + _dot(vt_t, p.astype(BF16))
    m_ref[...] = m_new


def _init_stats(m_ref, l_ref, acc_ref):
    m_ref[...] = jnp.full(m_ref.shape, NEG, F32)
    l_ref[...] = jnp.zeros(l_ref.shape, F32)
    acc_ref[...] = jnp.zeros(acc_ref.shape, F32)


def _finish_t(o_ref, lse_ref, g_ref, m_ref, l_ref, acc_ref, nh, gqa, tq):
    l = l_ref[...]
    inv = 1.0 / l
    g_t = g_ref[...].T if g_ref is not None else None
    parts = []
    for j in range(nh):
        half = (j // gqa) % 2
        sc = inv[:, j * tq:(j + 1) * tq]
        if g_t is not None:
            sc = sc * g_t[j:j + 1, :]
        parts.append(acc_ref[half * HEAD_DIM:(half + 1) * HEAD_DIM, j * tq:(j + 1) * tq] * sc)
    o_ref[...] = jnp.concatenate(parts, axis=0).T
    if lse_ref is not None:
        lse = m_ref[...] + jnp.log(l)
        rows = [jnp.broadcast_to(lse[:, j * tq:(j + 1) * tq], (HEAD_DIM, tq)) for j in range(nh)]
        lse_ref[...] = jnp.concatenate(rows, axis=0).T


def _band_kernel(*refs, nh, gqa, tq, tk, kw, want_lse, gated):
    q_ref, k_ref, vt_ref = refs[:3]
    pos = 3
    g_ref = None
    if gated:
        g_ref = refs[pos]
        pos += 1
    o_ref = refs[pos]
    pos += 1
    lse_ref = None
    if want_lse:
        lse_ref = refs[pos]
        pos += 1
    m_ref, l_ref, acc_ref = refs[pos:pos + 3]

    i = pl.program_id(3)
    R = nh * tq
    q0 = i * tq
    qt = _q_transposed(q_ref[...], nh, gqa, tq)
    _init_stats(m_ref, l_ref, acc_ref)
    rel = lax.broadcasted_iota(jnp.int32, (tk, tq), 1) - lax.broadcasted_iota(jnp.int32, (tk, tq), 0)

    def body(jj, carry):
        ks = pl.multiple_of(jj * tk, tk)
        kt = k_ref[pl.ds(ks, tk), :].astype(BF16)
        vt_t = vt_ref[:, pl.ds(ks, tk)].astype(BF16)
        dist = rel + (q0 - ks)
        _attend_tile_t(kt, vt_t, qt, [(dist >= 0) & (dist <= kw)] * nh, tq, m_ref, l_ref, acc_ref)
        return carry

    lo_t = jnp.maximum(q0 - kw, 0) // tk
    hi_t = (q0 + tq - 1) // tk
    lax.fori_loop(lo_t, hi_t + 1, body, 0)
    _finish_t(o_ref, lse_ref, g_ref, m_ref, l_ref, acc_ref, nh, gqa, tq)


def _band_attention(u, vt, *, q_off, k_off, nh, gqa, n_slabs, kw, dil, want_lse,
                    gates=None, g_off=0, tq=256, tk=256):
    B, S, W = u.shape
    n = S // dil
    qs = nh * HEAD_DIM
    tq = min(tq, n)
    tk = min(tk, n)
    uv = u.reshape(B, n, dil * W)
    wo = n_slabs * qs
    in_specs = [
        pl.BlockSpec((None, tq, qs), lambda b, r, p, i: (b, i, r * (W // qs) + q_off // qs + p)),
        pl.BlockSpec((None, n, LANES), lambda b, r, p, i: (b, 0, r * (W // LANES) + k_off // LANES + p)),
        pl.BlockSpec((None, None, LANES, n), lambda b, r, p, i: (b, r, p, 0)),
    ]
    args = [uv, uv, vt]
    if gates is not None:
        Wg = gates.shape[-1]
        in_specs.append(pl.BlockSpec((None, tq, LANES),
                                     lambda b, r, p, i: (b, i, r * (Wg // LANES) + g_off // LANES + p)))
        args.append(gates.reshape(B, n, dil * Wg))
    o_spec = pl.BlockSpec((None, tq, qs), lambda b, r, p, i: (b, i, r * n_slabs + p))
    o_shape = jax.ShapeDtypeStruct((B, n, dil * wo), F32)
    R = nh * tq
    res = pl.pallas_call(
        functools.partial(_band_kernel, nh=nh, gqa=gqa, tq=tq, tk=tk, kw=kw, want_lse=want_lse,
                          gated=gates is not None),
        grid=(B, dil, n_slabs, n // tq),
        in_specs=in_specs,
        out_specs=[o_spec, o_spec] if want_lse else o_spec,
        out_shape=[o_shape, o_shape] if want_lse else o_shape,
        scratch_shapes=[pltpu.VMEM((1, R), F32), pltpu.VMEM((1, R), F32), pltpu.VMEM((LANES, R), F32)],
        compiler_params=_cparams(("parallel", "parallel", "parallel", "arbitrary")),
    )(*args)
    if want_lse:
        return res[0].reshape(B, S, wo), res[1].reshape(B, S, wo)
    return res.reshape(B, S, wo)


def _values_transposed(u, v_off, width, dil):
    B, S, _ = u.shape
    v = u[:, :, v_off:v_off + width].reshape(B, S // dil, dil, width)
    return jnp.transpose(v, (0, 2, 3, 1))


def _compress_math(x_ref, w1_ref, pe_ref, w1f_ref, w2_ref, bd_ref, kg_ref, is_key, n_chunk):
    hid0 = _dot(pe_ref[...], w1f_ref[...])[0:1]
    acc = jnp.zeros((n_chunk, 4 * CMP_HIDDEN), F32)
    for j in range(CMP_STRIDE):
        xj = x_ref[pl.ds(j, n_chunk, stride=CMP_STRIDE), :].astype(BF16)
        acc = acc + _dot(xj, w1_ref[j])
    out = jnp.zeros((n_chunk, LANES), F32)
    for a in range(2):
        p0 = acc[:, (2 * a) * CMP_HIDDEN:(2 * a + 1) * CMP_HIDDEN]
        p1 = acc[:, (2 * a + 1) * CMP_HIDDEN:(2 * a + 2) * CMP_HIDDEN]
        hid = hid0 + p0 + pltpu.roll(p1, n_chunk - 1, axis=0)
        out = out + _dot(jax.nn.silu(hid).astype(BF16), w2_ref[a])
    normed = _head_norm(out, bd_ref, kg_ref[...])
    return jnp.where(is_key, normed, out)


def _compress_prompt_kernel(x_ref, w1_ref, pe_ref, w1f_ref, w2_ref, bd_ref, kg_ref, o_ref, ot_ref, *, n_chunk):
    is_key = pl.program_id(1) == 0
    out = _compress_math(x_ref, w1_ref, pe_ref, w1f_ref, w2_ref, bd_ref, kg_ref, is_key, n_chunk)
    o_ref[...] = out
    ot_ref[...] = out.T


def _compress_weights(pe, w1, w2, k_gain):
    w1r = w1.reshape(2, CMP_RATIO, CMP_STRIDE, HEAD_DIM, CMP_HIDDEN)
    wj = jnp.concatenate([w1r[:, 0], w1r[:, 1]], axis=-1)
    z = jnp.zeros_like(wj)
    w1bd = jnp.concatenate([jnp.concatenate([wj, z], axis=-1),
                            jnp.concatenate([z, wj], axis=-1)], axis=-2).astype(BF16)
    pe8 = jnp.concatenate([pe.reshape(2, 1, CMP_LEN * HEAD_DIM),
                           jnp.zeros((2, 7, CMP_LEN * HEAD_DIM), F32)], axis=1).astype(BF16)
    w1f = w1.reshape(2, CMP_LEN * HEAD_DIM, CMP_HIDDEN).astype(BF16)
    z2 = jnp.zeros_like(w2)
    w2pad = jnp.stack([jnp.concatenate([w2, z2], axis=-1),
                       jnp.concatenate([z2, w2], axis=-1)], axis=1).astype(BF16)
    hid = jnp.arange(LANES) // HEAD_DIM
    bd = ((hid[:, None] == hid[None, :]).astype(F32) / HEAD_DIM).astype(BF16)
    kg = jnp.tile(k_gain.astype(F32), 2)[None]
    return w1bd, pe8, w1f, w2pad, bd, kg


def _cw_specs(nd):
    def sp(shape, fn):
        return pl.BlockSpec(shape, fn)
    if nd == 3:
        return [
            sp((None, CMP_STRIDE, LANES, 4 * CMP_HIDDEN), lambda b, t, p: (t, 0, 0, 0)),
            sp((None, 8, CMP_LEN * HEAD_DIM), lambda b, t, p: (t, 0, 0)),
            sp((None, CMP_LEN * HEAD_DIM, CMP_HIDDEN), lambda b, t, p: (t, 0, 0)),
            sp((None, 2, CMP_HIDDEN, LANES), lambda b, t, p: (t, 0, 0, 0)),
            sp((LANES, LANES), lambda b, t, p: (0, 0)),
            sp((1, LANES), lambda b, t, p: (0, 0)),
        ]
    return [
        sp((None, CMP_STRIDE, LANES, 4 * CMP_HIDDEN), lambda b, t, p, *_: (t, 0, 0, 0)),
        sp((None, 8, CMP_LEN * HEAD_DIM), lambda b, t, p, *_: (t, 0, 0)),
        sp((None, CMP_LEN * HEAD_DIM, CMP_HIDDEN), lambda b, t, p, *_: (t, 0, 0)),
        sp((None, 2, CMP_HIDDEN, LANES), lambda b, t, p, *_: (t, 0, 0, 0)),
        sp((LANES, LANES), lambda b, t, p, *_: (0, 0)),
        sp((1, LANES), lambda b, t, p, *_: (0, 0)),
    ]


def _compress_prompt(u, col0, cw):
    B, S, W = u.shape
    n_chunk = S // CMP_STRIDE
    base = col0 // LANES
    return pl.pallas_call(
        functools.partial(_compress_prompt_kernel, n_chunk=n_chunk),
        grid=(B, 2, 2),
        in_specs=[pl.BlockSpec((None, S, LANES), lambda b, t, p: (b, 0, base + 2 * t + p))] + _cw_specs(3),
        out_specs=[pl.BlockSpec((None, None, n_chunk, LANES), lambda b, t, p: (b, t, 0, p)),
                   pl.BlockSpec((None, None, LANES, n_chunk), lambda b, t, p: (b, t, p, 0))],
        out_shape=[jax.ShapeDtypeStruct((B, 2, n_chunk, KVW_B), F32),
                   jax.ShapeDtypeStruct((B, 2, KVW_B, n_chunk), F32)],
        compiler_params=_cparams(("parallel", "parallel", "parallel")),
    )(u, *cw)


def _compress_sample_kernel(pt_ref, pool_ref, w1_ref, pe_ref, w1f_ref, w2_ref, bd_ref, kg_ref, o_ref,
                            raw, xbuf, sem, *, n_pages, n_chunk):
    n, t, p = pl.program_id(0), pl.program_id(1), pl.program_id(2)
    step = (n * 2 + t) * 2 + p
    n_steps = pl.num_programs(0) * 4

    def page_copy(seq, rt, pair, slot, pg):
        page = pt_ref[seq * n_pages + pg]
        row0 = pl.multiple_of(rt * KVW_B + pair * LANES, LANES)
        return pltpu.make_async_copy(pool_ref.at[page, pl.ds(row0, LANES), :], raw.at[slot, pg], sem.at[slot])

    def issue(st, slot):
        seq, rt, pair = st // 4, (st // 2) % 2, st % 2

        def go(pg, c):
            page_copy(seq, rt, pair, slot, pg).start()
            return c
        lax.fori_loop(0, n_pages, go, 0)

    @pl.when(step == 0)
    def _():
        issue(step, 0)

    @pl.when(step + 1 < n_steps)
    def _():
        issue(step + 1, (step + 1) % 2)

    slot = step % 2

    def wait(pg, c):
        page_copy(n, t, p, slot, pg).wait()
        return c
    lax.fori_loop(0, n_pages, wait, 0)

    def to_token_major(pg, c):
        xbuf[pl.ds(pl.multiple_of(pg * PAGE_SIZE, PAGE_SIZE), PAGE_SIZE), :] = raw[slot, pg].T
        return c
    lax.fori_loop(0, n_pages, to_token_major, 0, unroll=8)

    o_ref[...] = _compress_math(xbuf, w1_ref, pe_ref, w1f_ref, w2_ref, bd_ref, kg_ref, t == 0, n_chunk)


def _compress_sample(pool, page_table, cw):
    N, n_pages = page_table.shape
    n_chunk = n_pages * PAGE_SIZE // CMP_STRIDE
    gs = pltpu.PrefetchScalarGridSpec(
        num_scalar_prefetch=1,
        grid=(N, 2, 2),
        in_specs=[pl.BlockSpec(memory_space=pl.ANY)] + _cw_specs(4),
        out_specs=pl.BlockSpec((None, None, n_chunk, LANES), lambda b, t, p, *_: (b, t, 0, p)),
        scratch_shapes=[pltpu.VMEM((2, n_pages, LANES, PAGE_SIZE), F32),
                        pltpu.VMEM((n_pages * PAGE_SIZE, LANES), F32), pltpu.SemaphoreType.DMA((2,))],
    )
    return pl.pallas_call(
        functools.partial(_compress_sample_kernel, n_pages=n_pages, n_chunk=n_chunk),
        grid_spec=gs,
        out_shape=jax.ShapeDtypeStruct((N, 2, n_chunk, KVW_B), F32),
        compiler_params=_cparams(("arbitrary", "arbitrary", "arbitrary")),
    )(page_table.reshape(-1), pool, *cw)


def _cmp_kernel(q_ref, kc_ref, vc_ref, cov_ref, g_ref, o_ref, sel_ref, idx_ref, *,
                tq, n_chunk, nselp, pos_base, top_n):
    nh, gqa = 2 * GQA_B, GQA_B
    i = pl.program_id(2)
    R = nh * tq
    Q = _prep_q(q_ref[...], nh, gqa, tq)
    kc = kc_ref[...].astype(BF16)
    vc = vc_ref[...].astype(BF16)
    s = _nt_dot(Q, kc)
    pos_r = pos_base + i * tq + (lax.broadcasted_iota(jnp.int32, (R, n_chunk), 0) & (tq - 1))
    cend = lax.broadcasted_iota(jnp.int32, (R, n_chunk), 1) * CMP_STRIDE + (CMP_LEN - 1)
    s = jnp.where(cend <= pos_r, s, -jnp.inf)
    m = jnp.max(s, axis=-1, keepdims=True)
    m = jnp.where(m > -jnp.inf, m, 0.0)
    e = jnp.exp(s - m)
    den = jnp.sum(e, axis=-1, keepdims=True)
    p = e / jnp.maximum(den, TINY)
    o = _dot(p.astype(BF16), vc)
    o = _apply_gates(o, g_ref, nh, tq)
    o_ref[...] = _unprep_o(o, nh, gqa, tq)

    imps = []
    for a in range(2):
        ps = p[(a * gqa) * tq:(a * gqa + 1) * tq]
        for g in range(1, gqa):
            ps = ps + p[(a * gqa + g) * tq:(a * gqa + g + 1) * tq]
        hi = ps.astype(BF16)
        lo = (ps - hi.astype(F32)).astype(BF16)
        imps.append(_nt_dot(cov_ref[...], hi) + _nt_dot(cov_ref[...], lo))
    _select_blocks(imps, sel_ref, idx_ref, pos_base + i * tq, tq, nselp, top_n)


def _select_blocks(imps, sel_ref, idx_ref, pos0, tq, nselp, top_n):
    blk = lax.broadcasted_iota(jnp.int32, (nselp, tq), 0)
    cur = (pos0 + lax.broadcasted_iota(jnp.int32, (1, tq), 1)) // SEL_BLOCK
    valid = blk <= cur
    forced = (blk == 0) | (blk == cur) | (blk == cur - 1)
    n_forced = jnp.sum(forced.astype(jnp.int32), axis=0, keepdims=True)
    zero_row = jnp.zeros((1, tq), jnp.int32)
    for a, imp in enumerate(imps):
        rem = jnp.where(valid & jnp.logical_not(forced), imp, -1.0)
        sel = forced
        idx_ref[a, pl.ds(0, 1), :] = zero_row
        idx_ref[a, pl.ds(1, 1), :] = cur
        idx_ref[a, pl.ds(2, 1), :] = jnp.maximum(cur - 1, 0)
        for it in range(top_n - 1):
            mx = jnp.max(rem, axis=0, keepdims=True)
            first = jnp.min(jnp.where(rem == mx, blk, nselp), axis=0, keepdims=True)
            active = (it < top_n - n_forced) & (mx >= 0.0)
            pick = (blk == first) & active
            sel = sel | pick
            rem = jnp.where(pick, -1.0, rem)
            idx_ref[a, pl.ds(3 + it, 1), :] = jnp.where(active, first, 0)
        for r in range(3 + top_n - 1, IDX_ROWS):
            idx_ref[a, pl.ds(r, 1), :] = zero_row
        sel_ref[a] = sel.astype(F32)


def _cmp_t_kernel(q_ref, kc_ref, vct_ref, cov_ref, g_ref, o_ref, sel_ref, idx_ref, *,
                  tq, n_chunk, nselp, pos_base, top_n):
    nh, gqa = 2 * GQA_B, GQA_B
    i = pl.program_id(2)
    qt = _q_transposed(q_ref[...], nh, gqa, tq)
    s = _dot(kc_ref[...].astype(BF16), qt)
    pos_l = pos_base + i * tq + lax.broadcasted_iota(jnp.int32, (n_chunk, tq), 1)
    cend = lax.broadcasted_iota(jnp.int32, (n_chunk, tq), 0) * CMP_STRIDE + (CMP_LEN - 1)
    ok = cend <= pos_l
    s = jnp.concatenate([jnp.where(ok, s[:, j * tq:(j + 1) * tq], -jnp.inf) for j in range(nh)], axis=1)
    m = jnp.max(s, axis=0, keepdims=True)
    m = jnp.where(m > -jnp.inf, m, 0.0)
    e = jnp.exp(s - m)
    den = jnp.sum(e, axis=0, keepdims=True)
    p = e / jnp.maximum(den, TINY)
    acc = _dot(vct_ref[...].astype(BF16), p.astype(BF16))
    g_t = g_ref[...].T
    parts = []
    for j in range(nh):
        half = (j // gqa) % 2
        parts.append(acc[half * HEAD_DIM:(half + 1) * HEAD_DIM, j * tq:(j + 1) * tq] * g_t[j:j + 1, :])
    o_ref[...] = jnp.concatenate(parts, axis=0).T

    imps = []
    for a in range(2):
        ps = p[:, (a * gqa) * tq:(a * gqa + 1) * tq]
        for g in range(1, gqa):
            ps = ps + p[:, (a * gqa + g) * tq:(a * gqa + g + 1) * tq]
        hi = ps.astype(BF16)
        lo = (ps - hi.astype(F32)).astype(BF16)
        imps.append(_dot(cov_ref[...], hi) + _dot(cov_ref[...], lo))
    _select_blocks(imps, sel_ref, idx_ref, pos_base + i * tq, tq, nselp, top_n)


def _cover_t(n_chunk, n_cmp, n_sel, nselp):
    c = jnp.arange(n_chunk)[None, :]
    j = jnp.arange(nselp)[:, None]
    cov = ((c * CMP_STRIDE <= j * SEL_BLOCK + SEL_BLOCK - 1) & (c * CMP_STRIDE + CMP_LEN - 1 >= j * SEL_BLOCK)
           & (c < n_cmp) & (j < n_sel))
    return cov.astype(BF16)


def _cmp_attention(q, q_off, kvc, gates, g_off, *, tq, pos_base, n_keys, kvc_t=None):
    N, Tq, W = q.shape
    n_chunk = kvc.shape[2]
    n_cmp = (n_keys - CMP_LEN) // CMP_STRIDE + 1
    n_sel = -(-n_keys // SEL_BLOCK)
    nselp = -(-n_sel // 8) * 8
    top_n = min(SEL_TOPN, n_sel)
    qs = 2 * GQA_B * HEAD_DIM
    Wg = gates.shape[-1]
    cov = _cover_t(n_chunk, n_cmp, n_sel, nselp)
    if kvc_t is None:
        body, values = _cmp_kernel, kvc
        v_spec = pl.BlockSpec((None, None, n_chunk, LANES), lambda b, p, i: (b, 1, 0, p))
    else:
        body, values = _cmp_t_kernel, kvc_t
        v_spec = pl.BlockSpec((None, None, LANES, n_chunk), lambda b, p, i: (b, 1, p, 0))
    return pl.pallas_call(
        functools.partial(body, tq=tq, n_chunk=n_chunk, nselp=nselp, pos_base=pos_base, top_n=top_n),
        grid=(N, 2, Tq // tq),
        in_specs=[
            pl.BlockSpec((None, tq, qs), lambda b, p, i: (b, i, q_off // qs + p)),
            pl.BlockSpec((None, None, n_chunk, LANES), lambda b, p, i: (b, 0, 0, p)),
            v_spec,
            pl.BlockSpec((nselp, n_chunk), lambda b, p, i: (0, 0)),
            pl.BlockSpec((None, tq, LANES), lambda b, p, i: (b, i, g_off // LANES + p)),
        ],
        out_specs=[
            pl.BlockSpec((None, tq, qs), lambda b, p, i: (b, i, p)),
            pl.BlockSpec((None, 2, nselp, tq), lambda b, p, i: (b, p, 0, i)),
            pl.BlockSpec((None, 2, IDX_ROWS, tq), lambda b, p, i: (b, p, 0, i)),
        ],
        out_shape=[
            jax.ShapeDtypeStruct((N, Tq, WIDTH_B), F32),
            jax.ShapeDtypeStruct((N, KV_HEADS_B, nselp, Tq), F32),
            jax.ShapeDtypeStruct((N, KV_HEADS_B, IDX_ROWS, Tq), jnp.int32),
        ],
        compiler_params=_cparams(("parallel", "parallel", "parallel")),
    )(q, kvc, values, cov, gates)


def _sel_prompt_kernel(q_ref, k_ref, vt_ref, sel_ref, g_ref, o_ref, m_ref, l_ref, acc_ref, *, tq, tk):
    nh, gqa = 2 * GQA_B, GQA_B
    i = pl.program_id(2)
    q0 = i * tq
    qt = _q_transposed(q_ref[...], nh, gqa, tq)
    _init_stats(m_ref, l_ref, acc_ref)
    nblk = tk // SEL_BLOCK
    rel = lax.broadcasted_iota(jnp.int32, (tk, tq), 1) - lax.broadcasted_iota(jnp.int32, (tk, tq), 0)

    def body(jj, carry):
        ks = pl.multiple_of(jj * tk, tk)
        kt = k_ref[pl.ds(ks, tk), :].astype(BF16)
        vt_t = vt_ref[:, pl.ds(ks, tk)].astype(BF16)
        causal = rel + (q0 - ks) >= 0
        masks = []
        for a in range(2):
            rows = sel_ref[a, pl.ds(pl.multiple_of(jj * nblk, nblk), nblk), :]
            chosen = jnp.concatenate(
                [jnp.broadcast_to(rows[c:c + 1], (SEL_BLOCK, tq)) for c in range(nblk)], axis=0)
            masks += [(chosen > 0.5) & causal] * gqa
        _attend_tile_t(kt, vt_t, qt, masks, tq, m_ref, l_ref, acc_ref)
        return carry

    lax.fori_loop(0, (q0 + tq - 1) // tk + 1, body, 0)
    _finish_t(o_ref, None, g_ref, m_ref, l_ref, acc_ref, nh, gqa, tq)


def _sel_prompt(u, vt, sel_t, *, q_off, k_off, g_off, tq=128, tk=512):
    B, S, W = u.shape
    qs = 2 * GQA_B * HEAD_DIM
    nselp = sel_t.shape[2]
    tq = min(tq, S)
    tk = min(tk, S)
    assert tk % SEL_BLOCK == 0 and (tk // SEL_BLOCK) % 8 == 0 and S % tk == 0 and nselp * SEL_BLOCK >= S
    R = 2 * GQA_B * tq
    return pl.pallas_call(
        functools.partial(_sel_prompt_kernel, tq=tq, tk=tk),
        grid=(B, 2, S // tq),
        in_specs=[
            pl.BlockSpec((None, tq, qs), lambda b, p, i: (b, i, q_off // qs + p)),
            pl.BlockSpec((None, S, LANES), lambda b, p, i: (b, 0, k_off // LANES + p)),
            pl.BlockSpec((None, None, LANES, S), lambda b, p, i: (b, 0, p, 0)),
            pl.BlockSpec((None, 2, nselp, tq), lambda b, p, i: (b, p, 0, i)),
            pl.BlockSpec((None, tq, LANES), lambda b, p, i: (b, i, g_off // LANES + p)),
        ],
        out_specs=pl.BlockSpec((None, tq, qs), lambda b, p, i: (b, i, p)),
        out_shape=jax.ShapeDtypeStruct((B, S, WIDTH_B), F32),
        scratch_shapes=[pltpu.VMEM((1, R), F32), pltpu.VMEM((1, R), F32), pltpu.VMEM((LANES, R), F32)],
        compiler_params=_cparams(("parallel", "parallel", "arbitrary")),
    )(u, u, vt, sel_t, u)


def _sel_sample_kernel(pt_ref, idx_ref, q_ref, kn_ref, vn_ref, g_ref, pool_ref, o_ref, kbuf, vbuf, sem, *,
                       nt, n_pages, n_pick):
    nh, gqa, tq = 2 * GQA_B, GQA_B, SAMPLE_ROWS
    n, p, a = pl.program_id(0), pl.program_id(1), pl.program_id(2)
    step = (n * 2 + p) * 2 + a
    n_steps = pl.num_programs(0) * 4
    blocks_per_page = PAGE_SIZE // SEL_BLOCK
    width = n_pick * PAGE_SIZE

    def picked_block(seq, pair, head, t, r):
        src_row = jnp.where(r == 0, 0, r + 1)
        return idx_ref[((seq * KV_HEADS_B + pair * 2 + head) * IDX_ROWS + src_row) * tq + t]

    def copies(st, c):
        seq, pair, head = st // 4, (st // 2) % 2, st % 2
        t, r = c // n_pick, c % n_pick
        page = pt_ref[seq * n_pages + picked_block(seq, pair, head, t, r) // blocks_per_page]
        dst0 = pl.multiple_of(r * PAGE_SIZE, PAGE_SIZE)
        half0 = pl.multiple_of(head * HEAD_DIM, HEAD_DIM)
        out = []
        for row_type, buf in ((2, kbuf), (3, vbuf)):
            row0 = pl.multiple_of(row_type * KVW_B + pair * LANES + head * HEAD_DIM, HEAD_DIM)
            out.append(pltpu.make_async_copy(
                pool_ref.at[page, pl.ds(row0, HEAD_DIM), :],
                buf.at[head, t, pl.ds(half0, HEAD_DIM), pl.ds(dst0, PAGE_SIZE)], sem.at[head]))
        return out

    def issue(st):
        def go(c, carry):
            for cp in copies(st, c):
                cp.start()
            return carry
        lax.fori_loop(0, nt * n_pick, go, 0)

    @pl.when(step == 0)
    def _():
        for buf in (kbuf, vbuf):
            buf[0, :, HEAD_DIM:, :] = jnp.zeros((nt, HEAD_DIM, width), F32)
            buf[1, :, :HEAD_DIM, :] = jnp.zeros((nt, HEAD_DIM, width), F32)
        issue(step)

    @pl.when(step + 1 < n_steps)
    def _():
        issue(step + 1)

    slot = a

    def wait(c, carry):
        for cp in copies(step, c):
            cp.wait()
        return carry
    lax.fori_loop(0, nt * n_pick, wait, 0)

    R = nh * tq
    Q = _prep_q(q_ref[...], nh, gqa, tq)
    pad = jnp.zeros((LANES - tq, LANES), F32)
    kn = jnp.concatenate([kn_ref[...], pad], axis=0).astype(BF16)
    vn = jnp.concatenate([vn_ref[...], pad], axis=0).astype(BF16)
    row = lax.broadcasted_iota(jnp.int32, (R, LANES), 0)
    colk = lax.broadcasted_iota(jnp.int32, (R, LANES), 1)
    trow = row & (tq - 1)
    s_new = jnp.where((colk <= trow) & (colk < nt), _nt_dot(Q, kn), NEG)
    lane = lax.broadcasted_iota(jnp.int32, (1, width), 1)
    lane_pick = lane // PAGE_SIZE
    lane_half = (lane // SEL_BLOCK) % blocks_per_page
    o = jnp.zeros((R, LANES), F32)
    for t in range(nt):
        chosen = lane < 0
        for r in range(n_pick):
            half = picked_block(n, p, a, t, r) % blocks_per_page
            chosen = chosen | ((lane_pick == r) & (lane_half == half))
        s = jnp.where(chosen, _dot(Q, kbuf[slot, t].astype(BF16)), NEG)
        m = jnp.maximum(jnp.max(s, axis=-1, keepdims=True), jnp.max(s_new, axis=-1, keepdims=True))
        p1 = jnp.exp(s - m)
        p2 = jnp.exp(s_new - m)
        l = jnp.sum(p1, axis=-1, keepdims=True) + jnp.sum(p2, axis=-1, keepdims=True)
        o_t = (_nt_dot(p1.astype(BF16), vbuf[slot, t].astype(BF16)) + _dot(p2.astype(BF16), vn)) / l
        o = jnp.where(trow == t, o_t, o)

    g = g_ref[...]
    low = lax.broadcasted_iota(jnp.int32, (tq, LANES), 1) < HEAD_DIM
    first = a == 0
    placed = []
    for jj in range(gqa):
        oj = jnp.where(first, o[jj * tq:(jj + 1) * tq], o[(gqa + jj) * tq:(gqa + jj + 1) * tq])
        gj = jnp.where(first, g[:, jj:jj + 1], g[:, gqa + jj:gqa + jj + 1])
        oj = oj * gj
        placed.append(jnp.where(a == jj % 2, oj, pltpu.roll(oj, HEAD_DIM, axis=1)))
    o_ref[...] = jnp.concatenate([jnp.where(low, placed[0], placed[1]),
                                  jnp.where(low, placed[2], placed[3])], axis=1)


def _sel_sample(us, idx, pool, page_table, *, nt, q_off, k_off, v_off, g_off):
    N, tq, W = us.shape
    n_pages = page_table.shape[1]
    n_pick = SEL_TOPN - 1
    qs = 2 * GQA_B * HEAD_DIM
    hw = GQA_B * HEAD_DIM
    gs = pltpu.PrefetchScalarGridSpec(
        num_scalar_prefetch=2,
        grid=(N, 2, 2),
        in_specs=[
            pl.BlockSpec((None, tq, qs), lambda b, p, a, *_: (b, 0, q_off // qs + p)),
            pl.BlockSpec((None, tq, LANES), lambda b, p, a, *_: (b, 0, k_off // LANES + p)),
            pl.BlockSpec((None, tq, LANES), lambda b, p, a, *_: (b, 0, v_off // LANES + p)),
            pl.BlockSpec((None, tq, LANES), lambda b, p, a, *_: (b, 0, g_off // LANES + p)),
            pl.BlockSpec(memory_space=pl.ANY),
        ],
        out_specs=pl.BlockSpec((None, tq, hw), lambda b, p, a, *_: (b, 0, 2 * p + a)),
        scratch_shapes=[
            pltpu.VMEM((2, nt, LANES, n_pick * PAGE_SIZE), F32),
            pltpu.VMEM((2, nt, LANES, n_pick * PAGE_SIZE), F32),
            pltpu.SemaphoreType.DMA((2,)),
        ],
    )
    return pl.pallas_call(
        functools.partial(_sel_sample_kernel, nt=nt, n_pages=n_pages, n_pick=n_pick),
        grid_spec=gs,
        out_shape=jax.ShapeDtypeStruct((N, tq, WIDTH_B), F32),
        compiler_params=_cparams(("arbitrary", "arbitrary", "arbitrary")),
    )(page_table.reshape(-1), idx.reshape(-1), us, us, us, us, pool)


def _cache_attn_kernel(*refs, nh, gqa, L, dil, win, nt, want_lse, gated):
    q_ref, kc_ref, vc_ref, kn_ref, vn_ref = refs[:5]
    pos = 5
    g_ref = None
    if gated:
        g_ref = refs[pos]
        pos += 1
    o_ref = refs[pos]
    lse_ref = refs[pos + 1] if want_lse else None
    tq = SAMPLE_ROWS
    R = nh * tq
    Q = _prep_q(q_ref[...], nh, gqa, tq)
    kc_t = kc_ref[...].astype(BF16)
    vc_t = vc_ref[...].astype(BF16)
    pad = jnp.zeros((LANES - tq, LANES), F32)
    kn = jnp.concatenate([kn_ref[...], pad], axis=0).astype(BF16)
    vn = jnp.concatenate([vn_ref[...], pad], axis=0).astype(BF16)
    t1 = lax.broadcasted_iota(jnp.int32, (R, L), 0) & (tq - 1)
    d1 = L + t1 - lax.broadcasted_iota(jnp.int32, (R, L), 1)
    s1 = jnp.where(((d1 & (dil - 1)) == 0) & (d1 <= win), _dot(Q, kc_t), NEG)
    t2 = lax.broadcasted_iota(jnp.int32, (R, LANES), 0) & (tq - 1)
    c2 = lax.broadcasted_iota(jnp.int32, (R, LANES), 1)
    d2 = t2 - c2
    s2 = jnp.where((d2 >= 0) & ((d2 & (dil - 1)) == 0) & (d2 <= win) & (c2 < nt), _nt_dot(Q, kn), NEG)
    m = jnp.maximum(jnp.max(s1, axis=-1, keepdims=True), jnp.max(s2, axis=-1, keepdims=True))
    p1 = jnp.exp(s1 - m)
    p2 = jnp.exp(s2 - m)
    l = jnp.sum(p1, axis=-1, keepdims=True) + jnp.sum(p2, axis=-1, keepdims=True)
    o = (_nt_dot(p1.astype(BF16), vc_t) + _dot(p2.astype(BF16), vn)) / l
    o = _apply_gates(o, g_ref, nh, tq)
    o_ref[...] = _unprep_o(o, nh, gqa, tq)
    if want_lse:
        lse_ref[...] = _unprep_o(jnp.broadcast_to(m + jnp.log(l), (R, LANES)), nh, gqa, tq)


def _feature_major(x):
    nd = x.ndim
    xt = jnp.transpose(x, (0, 1) + tuple(range(3, nd)) + (2,))
    return xt.reshape(x.shape[0], x.shape[1], -1, x.shape[2])


def _cache_attention(us, cache_t, layer, *, q_off, k_off, v_off, nh, gqa, n_slabs, dil, win, nt, want_lse,
                     g_off=None):
    N, tq, W = us.shape
    L = cache_t.shape[3]
    cache = cache_t
    qs = nh * HEAD_DIM
    wo = n_slabs * qs
    in_specs = [
        pl.BlockSpec((None, tq, qs), lambda b, p: (b, 0, q_off // qs + p)),
        pl.BlockSpec((None, None, LANES, L), lambda b, p: (layer, b, p, 0)),
        pl.BlockSpec((None, None, LANES, L), lambda b, p: (layer, b, n_slabs + p, 0)),
        pl.BlockSpec((None, tq, LANES), lambda b, p: (b, 0, k_off // LANES + p)),
        pl.BlockSpec((None, tq, LANES), lambda b, p: (b, 0, v_off // LANES + p)),
    ]
    args = [us, cache, cache, us, us]
    if g_off is not None:
        in_specs.append(pl.BlockSpec((None, tq, LANES), lambda b, p: (b, 0, g_off // LANES + p)))
        args.append(us)
    o_spec = pl.BlockSpec((None, tq, qs), lambda b, p: (b, 0, p))
    o_shape = jax.ShapeDtypeStruct((N, tq, wo), F32)
    return pl.pallas_call(
        functools.partial(_cache_attn_kernel, nh=nh, gqa=gqa, L=L, dil=dil, win=win, nt=nt,
                          want_lse=want_lse, gated=g_off is not None),
        grid=(N, n_slabs),
        in_specs=in_specs,
        out_specs=[o_spec, o_spec] if want_lse else o_spec,
        out_shape=[o_shape, o_shape] if want_lse else o_shape,
        compiler_params=_cparams(("parallel", "parallel")),
    )(*args)


def _merge_a_kernel(o0, o1, o2, l0, l1, l2, z_ref, x_ref, w_ref, y_ref):
    a0, a1, a2 = l0[...], l1[...], l2[...]
    mx = jnp.maximum(jnp.maximum(a0, a1), a2)
    e0, e1, e2 = jnp.exp(a0 - mx), jnp.exp(a1 - mx), jnp.exp(a2 - mx)
    den = e0 + e1 + e2
    o = (e0 / den) * o0[...] + (e1 / den) * o1[...] + (e2 / den) * o2[...]
    y_ref[...] = x_ref[...] + _dot((o * jax.nn.silu(z_ref[...])).astype(BF16), w_ref[...])


def _merge_a(outs, lses, u, z_off, x, w_out_bf16):
    T, D = x.shape
    tm = min(T, 512)
    wa = WIDTH_A
    row = pl.BlockSpec((tm, wa), lambda i: (i, 0))
    return pl.pallas_call(
        _merge_a_kernel,
        grid=(T // tm,),
        in_specs=[row] * 6 + [
            pl.BlockSpec((tm, wa), lambda i: (i, z_off // wa)),
            pl.BlockSpec((tm, D), lambda i: (i, 0)),
            pl.BlockSpec((wa, D), lambda i: (0, 0)),
        ],
        out_specs=pl.BlockSpec((tm, D), lambda i: (i, 0)),
        out_shape=jax.ShapeDtypeStruct((T, D), F32),
        compiler_params=_cparams(("parallel",)),
    )(*outs, *lses, u, x, w_out_bf16)


def _merge_b_kernel(oc, os_, ow, z_ref, x_ref, w_ref, y_ref):
    o = oc[...] + os_[...] + ow[...]
    y_ref[...] = x_ref[...] + _dot((o * jax.nn.silu(z_ref[...])).astype(BF16), w_ref[...])


def _merge_b(o_c, o_s, o_w, u, z_off, x, w_out_bf16):
    T, D = x.shape
    tm = min(T, 512)
    wb = WIDTH_B
    row = pl.BlockSpec((tm, wb), lambda i: (i, 0))
    return pl.pallas_call(
        _merge_b_kernel,
        grid=(T // tm,),
        in_specs=[row] * 3 + [
            pl.BlockSpec((tm, wb), lambda i: (i, z_off // wb)),
            pl.BlockSpec((tm, D), lambda i: (i, 0)),
            pl.BlockSpec((wb, D), lambda i: (0, 0)),
        ],
        out_specs=pl.BlockSpec((tm, D), lambda i: (i, 0)),
        out_shape=jax.ShapeDtypeStruct((T, D), F32),
        compiler_params=_cparams(("parallel",)),
    )(o_c, o_s, o_w, u, x, w_out_bf16)


A_KINDS = ([EP_ROPE_Q] * (N_GROUPS_A * WIDTH_A // PROJ_TN) + [EP_ROPE_K] * (N_GROUPS_A * WIDTH_A // PROJ_TN)
           + [EP_NONE] * (N_GROUPS_A * WIDTH_A // PROJ_TN) + [EP_NONE] * (WIDTH_A // PROJ_TN))
A_K, A_V, A_Z = N_GROUPS_A * WIDTH_A, 2 * N_GROUPS_A * WIDTH_A, QKV_A

B_KINDS = ([EP_NORM_Q] * 4 + [EP_ROPE_Q] * 4 + [EP_NONE] * 4
           + [EP_NORM_K, EP_NONE, EP_ROPE_K, EP_NONE, EP_ROPE_K, EP_NONE] + [EP_SIG] * 3)


def _b_weight(w_in):
    wq = w_in[:, :WIDTH_B]
    o1 = WIDTH_B + 6 * KVW_B
    wkv = w_in[:, WIDTH_B:o1]
    wg = w_in[:, o1:o1 + 3 * HEADS_B]
    wz = w_in[:, o1 + 3 * HEADS_B:]
    nh = 2 * GQA_B
    tiles = []
    for b in range(3):
        for p in range(2):
            g = wg[:, b * HEADS_B + p * nh: b * HEADS_B + (p + 1) * nh]
            tiles.append(jnp.pad(g, ((0, 0), (0, LANES - nh))))
    return jnp.concatenate([wq, wq, wz, wkv] + tiles, axis=1).astype(BF16)


def _pad_rows(u, N, T):
    return jnp.pad(u.reshape(N, T, -1), ((0, 0), (0, SAMPLE_ROWS - T), (0, 0)))


def _layer_a_prompt(x, norm_g, w_bf16, q_gain, k_gain, w_out_bf16, tabs):
    B, S, D = x.shape
    u = _project(x.reshape(B * S, D), norm_g, w_bf16, A_KINDS, tabs[0], tabs[1], q_gain, k_gain)
    u3 = u.reshape(B, S, -1)
    outs, lses, states = [], [], []
    for g, (win, dil) in enumerate(DIL_PATTERNS):
        vt = _values_transposed(u3, A_V + g * WIDTH_A, WIDTH_A, dil)
        if dil == 1:
            qk, q_off, k_off = u3, g * WIDTH_A, A_K + g * WIDTH_A
        else:
            qk = jnp.concatenate([u3[:, :, g * WIDTH_A:(g + 1) * WIDTH_A],
                                  u3[:, :, A_K + g * WIDTH_A:A_K + (g + 1) * WIDTH_A]], axis=2)
            q_off, k_off = 0, WIDTH_A
        o, lse = _band_attention(qk, vt, q_off=q_off, k_off=k_off,
                                 nh=2, gqa=1, n_slabs=HEADS_A // 2, kw=win // dil, dil=dil, want_lse=True)
        outs.append(o.reshape(B * S, WIDTH_A))
        lses.append(lse.reshape(B * S, WIDTH_A))
        w = min(win, S)
        kg = u3[:, S - w:, A_K + g * WIDTH_A: A_K + (g + 1) * WIDTH_A]
        vg = u3[:, S - w:, A_V + g * WIDTH_A: A_V + (g + 1) * WIDTH_A]
        states.append(jnp.stack([kg, vg], axis=2).reshape(B, w, 2, HEADS_A, HEAD_DIM))
    y = _merge_a(outs, lses, u, A_Z, x.reshape(B * S, D), w_out_bf16)
    return y.reshape(B, S, D), states


def _layer_a_sample(x, caches_t, layer, norm_g, w_bf16, q_gain, k_gain, w_out_bf16, tabs):
    N, T, D = x.shape
    u = _project(x.reshape(N * T, D), norm_g, w_bf16, A_KINDS, tabs[0], tabs[1], q_gain, k_gain)
    us = _pad_rows(u, N, T)
    outs, lses, new_rows = [], [], []
    for g, (win, dil) in enumerate(DIL_PATTERNS):
        o, lse = _cache_attention(us, caches_t[g], layer, q_off=g * WIDTH_A, k_off=A_K + g * WIDTH_A,
                                  v_off=A_V + g * WIDTH_A, nh=2, gqa=1, n_slabs=HEADS_A // 2,
                                  dil=dil, win=win, nt=T, want_lse=True)
        outs.append(o[:, :T].reshape(N * T, WIDTH_A))
        lses.append(lse[:, :T].reshape(N * T, WIDTH_A))
        new_rows.append(jnp.concatenate([us[:, :, A_K + g * WIDTH_A: A_K + (g + 1) * WIDTH_A],
                                         us[:, :, A_V + g * WIDTH_A: A_V + (g + 1) * WIDTH_A]], axis=2))
    y = _merge_a(outs, lses, u, A_Z, x.reshape(N * T, D), w_out_bf16)
    return y.reshape(N, T, D), new_rows


def _layer_b_prompt(x, norm_g, w_bf16, q_gain, k_gain, cw, w_out_bf16, tabs):
    B, S, D = x.shape
    u = _project(x.reshape(B * S, D), norm_g, w_bf16, B_KINDS, tabs[0], tabs[1], q_gain, k_gain)
    u3 = u.reshape(B, S, -1)
    kvc, kvc_t = _compress_prompt(u3, B_KV, cw)
    o_c, sel_t, _ = _cmp_attention(u3, B_QC, kvc, u3, B_GATE, tq=min(256, S), pos_base=0, n_keys=S,
                                   kvc_t=kvc_t)
    o_s = _sel_prompt(u3, _values_transposed(u3, B_KV + 3 * KVW_B, KVW_B, 1), sel_t, q_off=B_QR,
                      k_off=B_KV + 2 * KVW_B, g_off=B_GATE + 2 * LANES)
    o_w = _band_attention(u3, _values_transposed(u3, B_WIN + KVW_B, KVW_B, 1), q_off=B_QR, k_off=B_WIN,
                          nh=2 * GQA_B, gqa=GQA_B, n_slabs=2, kw=WIN_B, dil=1, want_lse=False, gates=u3,
                          g_off=B_GATE + 4 * LANES, tq=128, tk=256)
    y = _merge_b(o_c.reshape(B * S, -1), o_s.reshape(B * S, -1), o_w.reshape(B * S, -1), u, B_Z,
                 x.reshape(B * S, D), w_out_bf16)
    rows = u3[:, :, B_KV:B_KV + 4 * KVW_B].reshape(B, S, 4, KV_HEADS_B, HEAD_DIM)
    w = min(WIN_B, S)
    wrows = u3[:, S - w:, B_WIN:B_WIN + 2 * KVW_B].reshape(B, w, 2, KV_HEADS_B, HEAD_DIM)
    return y.reshape(B, S, D), rows, wrows


def _layer_b_sample(x, pool_t, page_table, win_t, layer, norm_g, w_bf16, q_gain, k_gain, cw, w_out_bf16, tabs):
    N, T, D = x.shape
    past_len = page_table.shape[1] * PAGE_SIZE
    assert past_len % SEL_BLOCK == 0 and T <= SEL_BLOCK and T <= SAMPLE_ROWS
    u = _project(x.reshape(N * T, D), norm_g, w_bf16, B_KINDS, tabs[0], tabs[1], q_gain, k_gain)
    us = _pad_rows(u, N, T)
    u3 = u.reshape(N, T, -1)
    kvc = _compress_sample(pool_t, page_table, cw)
    o_c, _, idx = _cmp_attention(us, B_QC, kvc, us, B_GATE, tq=SAMPLE_ROWS, pos_base=past_len,
                                 n_keys=past_len + T)
    o_s = _sel_sample(us, idx, pool_t, page_table, nt=T, q_off=B_QR, k_off=B_KV + 2 * KVW_B,
                      v_off=B_KV + 3 * KVW_B, g_off=B_GATE + 2 * LANES)
    o_w = _cache_attention(us, win_t, layer, q_off=B_QR, k_off=B_WIN, v_off=B_WIN + KVW_B,
                           nh=2 * GQA_B, gqa=GQA_B, n_slabs=2, dil=1, win=WIN_B, nt=T, want_lse=False,
                           g_off=B_GATE + 4 * LANES)
    y = _merge_b(o_c[:, :T].reshape(N * T, -1), o_s[:, :T].reshape(N * T, -1), o_w[:, :T].reshape(N * T, -1),
                 u, B_Z, x.reshape(N * T, D), w_out_bf16)
    rows = u3[:, :, B_KV:B_KV + 4 * KVW_B].reshape(N, T, 4, KV_HEADS_B, HEAD_DIM)
    return y.reshape(N, T, D), rows, us[:, :, B_WIN:B_WIN + 2 * KVW_B]


def _append_kernel(c_ref, n_ref, o_ref, *, L, nt, fb):
    rolled = pltpu.roll(c_ref[...], L - nt, axis=1)
    pad = jnp.zeros((LANES - SAMPLE_ROWS, LANES), F32)
    new = n_ref[...]
    new_t = jnp.concatenate(
        [jnp.concatenate([new[:, c * LANES:(c + 1) * LANES], pad], axis=0).T for c in range(fb // LANES)], axis=0)
    new_t = pltpu.roll(new_t, LANES - nt, axis=1)
    lane = lax.broadcasted_iota(jnp.int32, (fb, LANES), 1)
    if L > LANES:
        o_ref[:, :L - LANES] = rolled[:, :L - LANES]
    o_ref[:, L - LANES:] = jnp.where(lane >= LANES - nt, new_t, rolled[:, L - LANES:])


def _append_cache(cache, cache_t, new_rows, nt, win):
    J, N, F, L = cache_t.shape
    tail = cache.shape[3:]
    if L + nt <= win or L % LANES:
        new = new_rows[:, :, :nt].reshape((J, N, nt) + tail)
        return jnp.concatenate([cache, new], axis=2)[:, :, -min(win, L + nt):]
    assert L == win
    fb = min(F, LANES * max(1, 2048 // L))
    assert F % fb == 0
    out_t = pl.pallas_call(
        functools.partial(_append_kernel, L=L, nt=nt, fb=fb),
        grid=(J, N, F // fb),
        in_specs=[pl.BlockSpec((None, None, fb, L), lambda j, n, f: (j, n, f, 0)),
                  pl.BlockSpec((None, None, SAMPLE_ROWS, fb), lambda j, n, f: (j, n, 0, f))],
        out_specs=pl.BlockSpec((None, None, fb, L), lambda j, n, f: (j, n, f, 0)),
        out_shape=jax.ShapeDtypeStruct((J, N, F, L), F32),
        compiler_params=_cparams(("parallel", "parallel", "parallel")),
    )(cache_t, new_rows)
    nd = len(tail)
    out = out_t.reshape((J, N) + tail + (L,))
    return jnp.transpose(out, (0, 1, nd + 2) + tuple(range(2, nd + 2)))


def kernel(x_prompt, x_sample, cache_dil_0, cache_dil_1, cache_dil_2, cache_nsa_paged, cache_nsa_win,
           page_table, a_norm, a_w_in, a_q_norm, a_k_norm, a_w_out, b_norm, b_w_in, b_q_norm, b_k_norm,
           b_cmp_pe, b_cmp_w1, b_cmp_w2, b_w_out):
    dil_caches = (cache_dil_0, cache_dil_1, cache_dil_2)
    B, S, _ = x_prompt.shape
    N, T, _ = x_sample.shape
    past_len = page_table.shape[1] * PAGE_SIZE
    depth = a_norm.shape[0] + b_norm.shape[0]
    n_pool = cache_nsa_paged.shape[1]
    tabs_p = _rope_tables(jnp.tile(jnp.arange(S), B))
    tabs_s = _rope_tables(jnp.tile(past_len + jnp.arange(T), N))
    dil_t = [_feature_major(c) for c in dil_caches]
    win_t = _feature_major(cache_nsa_win)
    pool_t = _feature_major(cache_nsa_paged)
    pool_t = pool_t.reshape((-1,) + pool_t.shape[2:])
    xp, xs = x_prompt, x_sample
    dil_p = [[] for _ in DIL_PATTERNS]
    dil_new = [[] for _ in DIL_PATTERNS]
    rows_p, rows_s, win_p, win_new = [], [], [], []
    for layer in range(depth):
        j = layer // 2
        if layer % 2 == 0:
            w = a_w_in[j].astype(BF16)
            wo = a_w_out[j].astype(BF16)
            xp, st_p = _layer_a_prompt(xp, a_norm[j], w, a_q_norm[j], a_k_norm[j], wo, tabs_p)
            xs, new_s = _layer_a_sample(xs, dil_t, j, a_norm[j], w, a_q_norm[j], a_k_norm[j], wo, tabs_s)
            for g in range(N_GROUPS_A):
                dil_p[g].append(st_p[g])
                dil_new[g].append(new_s[g])
        else:
            w = _b_weight(b_w_in[j])
            wo = b_w_out[j].astype(BF16)
            cw = _compress_weights(b_cmp_pe[j], b_cmp_w1[j], b_cmp_w2[j], b_k_norm[j])
            xp, rp, wp = _layer_b_prompt(xp, b_norm[j], w, b_q_norm[j], b_k_norm[j], cw, wo, tabs_p)
            xs, rs, wn = _layer_b_sample(xs, pool_t, page_table + j * n_pool, win_t, j, b_norm[j], w,
                                         b_q_norm[j], b_k_norm[j], cw, wo, tabs_s)
            rows_p.append(rp)
            win_p.append(wp)
            rows_s.append(rs)
            win_new.append(wn)
    dil_s = [_append_cache(dil_caches[g], dil_t[g], jnp.stack(dil_new[g]), T, DIL_PATTERNS[g][0])
             for g in range(N_GROUPS_A)]
    win_s = _append_cache(cache_nsa_win, win_t, jnp.stack(win_new), T, WIN_B)
    return (xp, xs,
            jnp.stack(dil_p[0]), jnp.stack(dil_p[1]), jnp.stack(dil_p[2]),
            jnp.stack(rows_p), jnp.stack(win_p),
            dil_s[0], dil_s[1], dil_s[2],
            jnp.stack(rows_s), win_s)
```

```python
import functools

import jax
import jax.numpy as jnp
from jax import lax
from jax.experimental import pallas as pl
from jax.experimental.pallas import tpu as pltpu

F32 = jnp.float32
BF16 = jnp.bfloat16

HEAD_DIM = 64
SCALE = HEAD_DIM ** -0.5
ROPE_THETA = 10000.0
EPS = 1e-6
TINY = 1e-30
NEG = -1e30
DIL_PATTERNS = ((128, 1), (512, 4), (2048, 16))
N_GROUPS_A = len(DIL_PATTERNS)
HEADS_A = 8
WIDTH_A = HEADS_A * HEAD_DIM
QKV_A = 3 * N_GROUPS_A * WIDTH_A
KV_HEADS_B = 4
GQA_B = 4
HEADS_B = KV_HEADS_B * GQA_B
WIDTH_B = HEADS_B * HEAD_DIM
KVW_B = KV_HEADS_B * HEAD_DIM
CMP_LEN = 32
CMP_STRIDE = 16
CMP_RATIO = CMP_LEN // CMP_STRIDE
CMP_HIDDEN = 128
SEL_BLOCK = 64
SEL_TOPN = 16
WIN_B = 512
PAGE_SIZE = 128

LANES = 128
PROJ_TN = 256
VMEM_LIMIT = 56 * 1024 * 1024
IDX_ROWS = 24
SAMPLE_ROWS = 8

B_QC, B_QR, B_Z, B_KV, B_WIN, B_GATE = 0, 1024, 2048, 3072, 4096, 4608
B_COLS = B_GATE + 6 * LANES

EP_NONE, EP_NORM_Q, EP_ROPE_Q, EP_NORM_K, EP_ROPE_K, EP_SIG = range(6)


def _cparams(sem):
    return pltpu.CompilerParams(dimension_semantics=sem, vmem_limit_bytes=VMEM_LIMIT)


def _nt_dot(a, b):
    return lax.dot_general(a, b, (((1,), (1,)), ((), ())), preferred_element_type=F32)


def _dot(a, b):
    return jnp.dot(a, b, preferred_element_type=F32)


def _head_norm(acc, bd_ref, gain):
    msq = _dot((acc * acc).astype(BF16), bd_ref[...])
    return acc * lax.rsqrt(msq + EPS) * gain


def _rope_tile(y, cos, sin_signed):
    lane = lax.broadcasted_iota(jnp.int32, (y.shape[0], LANES), 1)
    first_half = (lane & (HEAD_DIM - 1)) < (HEAD_DIM // 2)
    outs = []
    for c in range(y.shape[1] // LANES):
        yc = y[:, c * LANES:(c + 1) * LANES]
        partner = jnp.where(first_half, pltpu.roll(yc, LANES - HEAD_DIM // 2, axis=1),
                            pltpu.roll(yc, HEAD_DIM // 2, axis=1))
        outs.append(yc * cos + partner * sin_signed)
    return jnp.concatenate(outs, axis=1)


def _proj_kernel(x_ref, g_ref, w_ref, cos_ref, sin_ref, qg_ref, kg_ref, bd_ref, o_ref, h_ref, *, kinds):
    j = pl.program_id(1)

    @pl.when(j == 0)
    def _():
        x = x_ref[...]
        r = lax.rsqrt(jnp.mean(x * x, axis=-1, keepdims=True) + EPS)
        h_ref[...] = (x * r * g_ref[...]).astype(BF16)

    acc = _dot(h_ref[...], w_ref[...])

    def ranges(kind):
        out, start = [], None
        for t, k in enumerate(list(kinds) + [None]):
            if k == kind and start is None:
                start = t
            if k != kind and start is not None:
                out.append((start, t))
                start = None
        return out

    def emit(kind, fn):
        rs = ranges(kind)
        if not rs:
            return
        cond = None
        for a, b in rs:
            c = (j >= a) & (j < b)
            cond = c if cond is None else (cond | c)

        @pl.when(cond)
        def _():
            o_ref[...] = fn()

    emit(EP_NONE, lambda: acc)
    emit(EP_SIG, lambda: jax.nn.sigmoid(acc))
    emit(EP_NORM_Q, lambda: _head_norm(acc, bd_ref, qg_ref[...]))
    emit(EP_NORM_K, lambda: _head_norm(acc, bd_ref, kg_ref[...]))
    emit(EP_ROPE_Q, lambda: _rope_tile(_head_norm(acc, bd_ref, qg_ref[...]), cos_ref[...], sin_ref[...]))
    emit(EP_ROPE_K, lambda: _rope_tile(_head_norm(acc, bd_ref, kg_ref[...]), cos_ref[...], sin_ref[...]))


def _project(x, norm_g, w_bf16, kinds, cos_t, sin_t, q_gain, k_gain):
    T, D = x.shape
    E = w_bf16.shape[1]
    tn = PROJ_TN
    assert E == len(kinds) * tn
    tm = min(T, 1024)
    assert T % tm == 0
    rep = tn // HEAD_DIM
    qg = jnp.tile(q_gain.astype(F32), rep)[None]
    kg = jnp.tile(k_gain.astype(F32), rep)[None]
    hid = jnp.arange(tn) // HEAD_DIM
    bd = ((hid[:, None] == hid[None, :]).astype(F32) / HEAD_DIM).astype(BF16)
    return pl.pallas_call(
        functools.partial(_proj_kernel, kinds=tuple(kinds)),
        grid=(T // tm, E // tn),
        in_specs=[
            pl.BlockSpec((tm, D), lambda i, j: (i, 0)),
            pl.BlockSpec((1, D), lambda i, j: (0, 0)),
            pl.BlockSpec((D, tn), lambda i, j: (0, j)),
            pl.BlockSpec((tm, LANES), lambda i, j: (i, 0)),
            pl.BlockSpec((tm, LANES), lambda i, j: (i, 0)),
            pl.BlockSpec((1, tn), lambda i, j: (0, 0)),
            pl.BlockSpec((1, tn), lambda i, j: (0, 0)),
            pl.BlockSpec((tn, tn), lambda i, j: (0, 0)),
        ],
        out_specs=pl.BlockSpec((tm, tn), lambda i, j: (i, j)),
        out_shape=jax.ShapeDtypeStruct((T, E), F32),
        scratch_shapes=[pltpu.VMEM((tm, D), BF16)],
        compiler_params=_cparams(("parallel", "arbitrary")),
    )(x, norm_g.astype(F32)[None], w_bf16, cos_t, sin_t, qg, kg, bd)


def _rope_tables(pos):
    half = HEAD_DIM // 2
    inv_freq = ROPE_THETA ** (-jnp.arange(half, dtype=F32) / half)
    ang = pos.astype(F32)[:, None] * inv_freq[None, :]
    cos, sin = jnp.cos(ang), jnp.sin(ang)
    cos_t = jnp.concatenate([cos, cos, cos, cos], axis=1)
    sin_t = jnp.concatenate([-sin, sin, -sin, sin], axis=1)
    return cos_t, sin_t


def _prep_q(q, nh, gqa, tq):
    lane = lax.broadcasted_iota(jnp.int32, (tq, LANES), 1)
    low = lane < HEAD_DIM
    parts = []
    for j in range(nh):
        col = q[:, (j // 2) * LANES:(j // 2 + 1) * LANES]
        nat, tgt = j % 2, (j // gqa) % 2
        if nat != tgt:
            col = pltpu.roll(col, HEAD_DIM, axis=1)
        keep = low if tgt == 0 else jnp.logical_not(low)
        parts.append(jnp.where(keep, col * SCALE, 0.0))
    return jnp.concatenate(parts, axis=0).astype(BF16)


def _unprep_o(o, nh, gqa, tq):
    lane = lax.broadcasted_iota(jnp.int32, (tq, LANES), 1)
    low = lane < HEAD_DIM
    cols = []
    for c in range(nh // 2):
        halves = []
        for nat in (0, 1):
            j = 2 * c + nat
            tgt = (j // gqa) % 2
            oj = o[j * tq:(j + 1) * tq]
            if tgt != nat:
                oj = pltpu.roll(oj, HEAD_DIM, axis=1)
            halves.append(oj)
        cols.append(jnp.where(low, halves[0], halves[1]))
    return jnp.concatenate(cols, axis=1)


def _apply_gates(o, g_ref, nh, tq):
    if g_ref is None:
        return o
    g = g_ref[...]
    return jnp.concatenate([o[j * tq:(j + 1) * tq] * g[:, j:j + 1] for j in range(nh)], axis=0)


def _q_transposed(q, nh, gqa, tq):
    qt = (q * SCALE).T
    zero = jnp.zeros((HEAD_DIM, tq), F32)
    cols = []
    for j in range(nh):
        h = qt[j * HEAD_DIM:(j + 1) * HEAD_DIM]
        cols.append(jnp.concatenate([h, zero] if (j // gqa) % 2 == 0 else [zero, h], axis=0))
    return jnp.concatenate(cols, axis=1).astype(BF16)


def _masked_heads(s, masks, tq):
    return jnp.concatenate([jnp.where(mk, s[:, j * tq:(j + 1) * tq], NEG) for j, mk in enumerate(masks)], axis=1)


ACC_ROWS = LANES + 16


def _attend_tile_t(kt, vt_t, qt, masks, tq, m_ref, acc_ref):
    s = _masked_heads(_dot(kt, qt), masks, tq)
    m_old = m_ref[...]
    m_new = jnp.maximum(m_old, jnp.max(s, axis=0, keepdims=True))
    alpha = jnp.exp(m_old - m_new)
    p = jnp.exp((s - m_new).astype(BF16))
    vt1 = jnp.concatenate([vt_t, jnp.ones((ACC_ROWS - LANES, vt_t.shape[1]), BF16)], axis=0)
    acc_ref[...] = alpha * acc_ref[...] + _dot(vt1, p)
    m_ref[...] = m_new


def _init_stats(m_ref, acc_ref):
    m_ref[...] = jnp.full(m_ref.shape, NEG, F32)
    acc_ref[...] = jnp.zeros(acc_ref.shape, F32)


def _finish_t(o_ref, lse_ref, g_ref, m_ref, acc_ref, nh, gqa, tq):
    l = acc_ref[LANES:LANES + 1, :]
    inv = 1.0 / l
    g_t = g_ref[...].T if g_ref is not None else None
    parts = []
    for j in range(nh):
        half = (j // gqa) % 2
        sc = inv[:, j * tq:(j + 1) * tq]
        if g_t is not None:
            sc = sc * g_t[j:j + 1, :]
        parts.append(acc_ref[half * HEAD_DIM:(half + 1) * HEAD_DIM, j * tq:(j + 1) * tq] * sc)
    o_ref[...] = jnp.concatenate(parts, axis=0).T
    if lse_ref is not None:
        lse = m_ref[...] + jnp.log(l)
        rows = [jnp.broadcast_to(lse[:, j * tq:(j + 1) * tq], (HEAD_DIM, tq)) for j in range(nh)]
        lse_ref[...] = jnp.concatenate(rows, axis=0).T


def _band_kernel(*refs, nh, gqa, tq, tk, kw, want_lse, gated):
    q_ref, k_ref, vt_ref = refs[:3]
    pos = 3
    g_ref = None
    if gated:
        g_ref = refs[pos]
        pos += 1
    o_ref = refs[pos]
    pos += 1
    lse_ref = None
    if want_lse:
        lse_ref = refs[pos]
        pos += 1
    m_ref, acc_ref = refs[pos:pos + 2]

    i = pl.program_id(3)
    R = nh * tq
    q0 = i * tq
    qt = _q_transposed(q_ref[...], nh, gqa, tq)
    _init_stats(m_ref, acc_ref)
    rel = lax.broadcasted_iota(jnp.int32, (tk, tq), 1) - lax.broadcasted_iota(jnp.int32, (tk, tq), 0)

    def body(jj, carry):
        ks = pl.multiple_of(jj * tk, tk)
        kt = k_ref[pl.ds(ks, tk), :].astype(BF16)
        vt_t = vt_ref[:, pl.ds(ks, tk)].astype(BF16)
        dist = rel + (q0 - ks)
        _attend_tile_t(kt, vt_t, qt, [(dist >= 0) & (dist <= kw)] * nh, tq, m_ref, acc_ref)
        return carry

    lo_t = jnp.maximum(q0 - kw, 0) // tk
    hi_t = (q0 + tq - 1) // tk
    lax.fori_loop(lo_t, hi_t + 1, body, 0)
    _finish_t(o_ref, lse_ref, g_ref, m_ref, acc_ref, nh, gqa, tq)


def _band_attention(u, vt, *, q_off, k_off, nh, gqa, n_slabs, kw, dil, want_lse,
                    gates=None, g_off=0, tq=256, tk=256):
    B, S, W = u.shape
    n = S // dil
    qs = nh * HEAD_DIM
    tq = min(tq, n)
    tk = min(tk, n)
    uv = u.reshape(B, n, dil * W)
    wo = n_slabs * qs
    in_specs = [
        pl.BlockSpec((None, tq, qs), lambda b, r, p, i: (b, i, r * (W // qs) + q_off // qs + p)),
        pl.BlockSpec((None, n, LANES), lambda b, r, p, i: (b, 0, r * (W // LANES) + k_off // LANES + p)),
        pl.BlockSpec((None, None, LANES, n), lambda b, r, p, i: (b, r, p, 0)),
    ]
    args = [uv, uv, vt]
    if gates is not None:
        Wg = gates.shape[-1]
        in_specs.append(pl.BlockSpec((None, tq, LANES),
                                     lambda b, r, p, i: (b, i, r * (Wg // LANES) + g_off // LANES + p)))
        args.append(gates.reshape(B, n, dil * Wg))
    o_spec = pl.BlockSpec((None, tq, qs), lambda b, r, p, i: (b, i, r * n_slabs + p))
    o_shape = jax.ShapeDtypeStruct((B, n, dil * wo), F32)
    R = nh * tq
    res = pl.pallas_call(
        functools.partial(_band_kernel, nh=nh, gqa=gqa, tq=tq, tk=tk, kw=kw, want_lse=want_lse,
                          gated=gates is not None),
        grid=(B, dil, n_slabs, n // tq),
        in_specs=in_specs,
        out_specs=[o_spec, o_spec] if want_lse else o_spec,
        out_shape=[o_shape, o_shape] if want_lse else o_shape,
        scratch_shapes=[pltpu.VMEM((1, R), F32), pltpu.VMEM((ACC_ROWS, R), F32)],
        compiler_params=_cparams(("parallel", "parallel", "parallel", "arbitrary")),
    )(*args)
    if want_lse:
        return res[0].reshape(B, S, wo), res[1].reshape(B, S, wo)
    return res.reshape(B, S, wo)


def _values_transposed(u, v_off, width, dil):
    B, S, _ = u.shape
    v = u[:, :, v_off:v_off + width].reshape(B, S // dil, dil, width)
    return jnp.transpose(v, (0, 2, 3, 1))


def _compress_math(x_ref, w1_ref, pe_ref, w1f_ref, w2_ref, bd_ref, kg_ref, is_key, n_chunk):
    hid0 = _dot(pe_ref[...], w1f_ref[...])[0:1]
    acc = jnp.zeros((n_chunk, 4 * CMP_HIDDEN), F32)
    for j in range(CMP_STRIDE):
        xj = x_ref[pl.ds(j, n_chunk, stride=CMP_STRIDE), :].astype(BF16)
        acc = acc + _dot(xj, w1_ref[j])
    out = jnp.zeros((n_chunk, LANES), F32)
    for a in range(2):
        p0 = acc[:, (2 * a) * CMP_HIDDEN:(2 * a + 1) * CMP_HIDDEN]
        p1 = acc[:, (2 * a + 1) * CMP_HIDDEN:(2 * a + 2) * CMP_HIDDEN]
        hid = hid0 + p0 + pltpu.roll(p1, n_chunk - 1, axis=0)
        out = out + _dot(jax.nn.silu(hid).astype(BF16), w2_ref[a])
    normed = _head_norm(out, bd_ref, kg_ref[...])
    return jnp.where(is_key, normed, out)


def _compress_prompt_kernel(x_ref, w1_ref, pe_ref, w1f_ref, w2_ref, bd_ref, kg_ref, o_ref, ot_ref, *, n_chunk):
    is_key = pl.program_id(1) == 0
    out = _compress_math(x_ref, w1_ref, pe_ref, w1f_ref, w2_ref, bd_ref, kg_ref, is_key, n_chunk)
    o_ref[...] = out
    ot_ref[...] = out.T


def _compress_weights(pe, w1, w2, k_gain):
    w1r = w1.reshape(2, CMP_RATIO, CMP_STRIDE, HEAD_DIM, CMP_HIDDEN)
    wj = jnp.concatenate([w1r[:, 0], w1r[:, 1]], axis=-1)
    z = jnp.zeros_like(wj)
    w1bd = jnp.concatenate([jnp.concatenate([wj, z], axis=-1),
                            jnp.concatenate([z, wj], axis=-1)], axis=-2).astype(BF16)
    pe8 = jnp.concatenate([pe.reshape(2, 1, CMP_LEN * HEAD_DIM),
                           jnp.zeros((2, 7, CMP_LEN * HEAD_DIM), F32)], axis=1).astype(BF16)
    w1f = w1.reshape(2, CMP_LEN * HEAD_DIM, CMP_HIDDEN).astype(BF16)
    z2 = jnp.zeros_like(w2)
    w2pad = jnp.stack([jnp.concatenate([w2, z2], axis=-1),
                       jnp.concatenate([z2, w2], axis=-1)], axis=1).astype(BF16)
    hid = jnp.arange(LANES) // HEAD_DIM
    bd = ((hid[:, None] == hid[None, :]).astype(F32) / HEAD_DIM).astype(BF16)
    kg = jnp.tile(k_gain.astype(F32), 2)[None]
    return w1bd, pe8, w1f, w2pad, bd, kg


def _cw_specs(nd):
    def sp(shape, fn):
        return pl.BlockSpec(shape, fn)
    if nd == 3:
        return [
            sp((None, CMP_STRIDE, LANES, 4 * CMP_HIDDEN), lambda b, t, p: (t, 0, 0, 0)),
            sp((None, 8, CMP_LEN * HEAD_DIM), lambda b, t, p: (t, 0, 0)),
            sp((None, CMP_LEN * HEAD_DIM, CMP_HIDDEN), lambda b, t, p: (t, 0, 0)),
            sp((None, 2, CMP_HIDDEN, LANES), lambda b, t, p: (t, 0, 0, 0)),
            sp((LANES, LANES), lambda b, t, p: (0, 0)),
            sp((1, LANES), lambda b, t, p: (0, 0)),
        ]
    return [
        sp((None, CMP_STRIDE, LANES, 4 * CMP_HIDDEN), lambda b, t, p, *_: (t, 0, 0, 0)),
        sp((None, 8, CMP_LEN * HEAD_DIM), lambda b, t, p, *_: (t, 0, 0)),
        sp((None, CMP_LEN * HEAD_DIM, CMP_HIDDEN), lambda b, t, p, *_: (t, 0, 0)),
        sp((None, 2, CMP_HIDDEN, LANES), lambda b, t, p, *_: (t, 0, 0, 0)),
        sp((LANES, LANES), lambda b, t, p, *_: (0, 0)),
        sp((1, LANES), lambda b, t, p, *_: (0, 0)),
    ]


def _compress_prompt(u, col0, cw):
    B, S, W = u.shape
    n_chunk = S // CMP_STRIDE
    base = col0 // LANES
    return pl.pallas_call(
        functools.partial(_compress_prompt_kernel, n_chunk=n_chunk),
        grid=(B, 2, 2),
        in_specs=[pl.BlockSpec((None, S, LANES), lambda b, t, p: (b, 0, base + 2 * t + p))] + _cw_specs(3),
        out_specs=[pl.BlockSpec((None, None, n_chunk, LANES), lambda b, t, p: (b, t, 0, p)),
                   pl.BlockSpec((None, None, LANES, n_chunk), lambda b, t, p: (b, t, p, 0))],
        out_shape=[jax.ShapeDtypeStruct((B, 2, n_chunk, KVW_B), F32),
                   jax.ShapeDtypeStruct((B, 2, KVW_B, n_chunk), F32)],
        compiler_params=_cparams(("parallel", "parallel", "parallel")),
    )(u, *cw)


def _compress_sample_kernel(pt_ref, pool_ref, w1_ref, pe_ref, w1f_ref, w2_ref, bd_ref, kg_ref, o_ref,
                            raw, xbuf, sem, *, n_pages, n_chunk):
    n, t, p = pl.program_id(0), pl.program_id(1), pl.program_id(2)
    step = (n * 2 + t) * 2 + p
    n_steps = pl.num_programs(0) * 4

    def page_copy(page, rt, pair, slot, pg):
        row0 = pl.multiple_of(pair * LANES, LANES)
        return pltpu.make_async_copy(pool_ref.at[page, rt, pl.ds(row0, LANES), :], raw.at[slot, pg], sem.at[slot])

    def issue(st, slot):
        seq, rt, pair = st // 4, (st // 2) % 2, st % 2

        def go(pg, c):
            page_copy(pt_ref[seq * n_pages + pg], rt, pair, slot, pg).start()
            return c
        lax.fori_loop(0, n_pages, go, 0, unroll=8)

    @pl.when(step == 0)
    def _():
        issue(step, 0)

    @pl.when(step + 1 < n_steps)
    def _():
        issue(step + 1, (step + 1) % 2)

    slot = step % 2

    def wait(pg, c):
        page_copy(0, t, p, slot, pg).wait()
        return c
    lax.fori_loop(0, n_pages, wait, 0, unroll=8)

    def to_token_major(pg, c):
        xbuf[pl.ds(pl.multiple_of(pg * PAGE_SIZE, PAGE_SIZE), PAGE_SIZE), :] = raw[slot, pg].T
        return c
    lax.fori_loop(0, n_pages, to_token_major, 0, unroll=8)

    o_ref[...] = _compress_math(xbuf, w1_ref, pe_ref, w1f_ref, w2_ref, bd_ref, kg_ref, t == 0, n_chunk)


def _compress_sample(pool, page_table, cw):
    N, n_pages = page_table.shape
    n_chunk = n_pages * PAGE_SIZE // CMP_STRIDE
    gs = pltpu.PrefetchScalarGridSpec(
        num_scalar_prefetch=1,
        grid=(N, 2, 2),
        in_specs=[pl.BlockSpec(memory_space=pl.ANY)] + _cw_specs(4),
        out_specs=pl.BlockSpec((None, None, n_chunk, LANES), lambda b, t, p, *_: (b, t, 0, p)),
        scratch_shapes=[pltpu.VMEM((2, n_pages, LANES, PAGE_SIZE), F32),
                        pltpu.VMEM((n_pages * PAGE_SIZE, LANES), F32), pltpu.SemaphoreType.DMA((2,))],
    )
    return pl.pallas_call(
        functools.partial(_compress_sample_kernel, n_pages=n_pages, n_chunk=n_chunk),
        grid_spec=gs,
        out_shape=jax.ShapeDtypeStruct((N, 2, n_chunk, KVW_B), F32),
        compiler_params=_cparams(("arbitrary", "arbitrary", "arbitrary")),
    )(page_table.reshape(-1), pool, *cw)


def _cmp_kernel(q_ref, kc_ref, vc_ref, cov_ref, g_ref, o_ref, sel_ref, idx_ref, *,
                tq, n_chunk, nselp, pos_base, top_n):
    nh, gqa = 2 * GQA_B, GQA_B
    i = pl.program_id(2)
    R = nh * tq
    Q = _prep_q(q_ref[...], nh, gqa, tq)
    kc = kc_ref[...].astype(BF16)
    vc = vc_ref[...].astype(BF16)
    s = _nt_dot(Q, kc)
    pos_r = pos_base + i * tq + (lax.broadcasted_iota(jnp.int32, (R, n_chunk), 0) & (tq - 1))
    cend = lax.broadcasted_iota(jnp.int32, (R, n_chunk), 1) * CMP_STRIDE + (CMP_LEN - 1)
    s = jnp.where(cend <= pos_r, s, -jnp.inf)
    m = jnp.max(s, axis=-1, keepdims=True)
    m = jnp.where(m > -jnp.inf, m, 0.0)
    e = jnp.exp(s - m)
    den = jnp.sum(e, axis=-1, keepdims=True)
    p = e / jnp.maximum(den, TINY)
    o = _dot(p.astype(BF16), vc)
    o = _apply_gates(o, g_ref, nh, tq)
    o_ref[...] = _unprep_o(o, nh, gqa, tq)

    imps = []
    for a in range(2):
        ps = p[(a * gqa) * tq:(a * gqa + 1) * tq]
        for g in range(1, gqa):
            ps = ps + p[(a * gqa + g) * tq:(a * gqa + g + 1) * tq]
        hi = ps.astype(BF16)
        lo = (ps - hi.astype(F32)).astype(BF16)
        imps.append(_nt_dot(cov_ref[...], hi) + _nt_dot(cov_ref[...], lo))
    _select_blocks(imps, sel_ref, idx_ref, pos_base + i * tq, tq, nselp, top_n)


def _select_blocks(imps, sel_ref, idx_ref, pos0, tq, nselp, top_n):
    blk = lax.broadcasted_iota(jnp.int32, (nselp, tq), 0)
    cur = (pos0 + lax.broadcasted_iota(jnp.int32, (1, tq), 1)) // SEL_BLOCK
    valid = blk <= cur
    forced = (blk == 0) | (blk == cur) | (blk == cur - 1)
    n_forced = jnp.sum(forced.astype(jnp.int32), axis=0, keepdims=True)
    zero_row = jnp.zeros((1, tq), jnp.int32)
    for a, imp in enumerate(imps):
        rem = jnp.where(valid & jnp.logical_not(forced), imp, -1.0)
        sel = forced
        idx_ref[a, pl.ds(0, 1), :] = zero_row
        idx_ref[a, pl.ds(1, 1), :] = cur
        idx_ref[a, pl.ds(2, 1), :] = jnp.maximum(cur - 1, 0)
        for it in range(top_n - 1):
            mx = jnp.max(rem, axis=0, keepdims=True)
            first = jnp.min(jnp.where(rem == mx, blk, nselp), axis=0, keepdims=True)
            active = (it < top_n - n_forced) & (mx >= 0.0)
            pick = (blk == first) & active
            sel = sel | pick
            rem = jnp.where(pick, -1.0, rem)
            idx_ref[a, pl.ds(3 + it, 1), :] = jnp.where(active, first, 0)
        for r in range(3 + top_n - 1, IDX_ROWS):
            idx_ref[a, pl.ds(r, 1), :] = zero_row
        sel_ref[a] = sel.astype(F32)


def _cmp_t_kernel(q_ref, kc_ref, vct_ref, cov_ref, g_ref, o_ref, sel_ref, idx_ref, *,
                  tq, n_chunk, nselp, pos_base, top_n):
    nh, gqa = 2 * GQA_B, GQA_B
    i = pl.program_id(2)
    qt = _q_transposed(q_ref[...], nh, gqa, tq)
    s = _dot(kc_ref[...].astype(BF16), qt)
    pos_l = pos_base + i * tq + lax.broadcasted_iota(jnp.int32, (n_chunk, tq), 1)
    cend = lax.broadcasted_iota(jnp.int32, (n_chunk, tq), 0) * CMP_STRIDE + (CMP_LEN - 1)
    ok = cend <= pos_l
    s = jnp.concatenate([jnp.where(ok, s[:, j * tq:(j + 1) * tq], -jnp.inf) for j in range(nh)], axis=1)
    m = jnp.max(s, axis=0, keepdims=True)
    m = jnp.where(m > -jnp.inf, m, 0.0)
    e = jnp.exp(s - m)
    den = jnp.sum(e, axis=0, keepdims=True)
    p = e / jnp.maximum(den, TINY)
    acc = _dot(vct_ref[...].astype(BF16), p.astype(BF16))
    g_t = g_ref[...].T
    parts = []
    for j in range(nh):
        half = (j // gqa) % 2
        parts.append(acc[half * HEAD_DIM:(half + 1) * HEAD_DIM, j * tq:(j + 1) * tq] * g_t[j:j + 1, :])
    o_ref[...] = jnp.concatenate(parts, axis=0).T

    imps = []
    for a in range(2):
        ps = p[:, (a * gqa) * tq:(a * gqa + 1) * tq]
        for g in range(1, gqa):
            ps = ps + p[:, (a * gqa + g) * tq:(a * gqa + g + 1) * tq]
        hi = ps.astype(BF16)
        lo = (ps - hi.astype(F32)).astype(BF16)
        imps.append(_dot(cov_ref[...], hi) + _dot(cov_ref[...], lo))
    _select_blocks(imps, sel_ref, idx_ref, pos_base + i * tq, tq, nselp, top_n)


def _cover_t(n_chunk, n_cmp, n_sel, nselp):
    c = jnp.arange(n_chunk)[None, :]
    j = jnp.arange(nselp)[:, None]
    cov = ((c * CMP_STRIDE <= j * SEL_BLOCK + SEL_BLOCK - 1) & (c * CMP_STRIDE + CMP_LEN - 1 >= j * SEL_BLOCK)
           & (c < n_cmp) & (j < n_sel))
    return cov.astype(BF16)


def _cmp_attention(q, q_off, kvc, gates, g_off, *, tq, pos_base, n_keys, kvc_t=None):
    N, Tq, W = q.shape
    n_chunk = kvc.shape[2]
    n_cmp = (n_keys - CMP_LEN) // CMP_STRIDE + 1
    n_sel = -(-n_keys // SEL_BLOCK)
    nselp = -(-n_sel // 8) * 8
    top_n = min(SEL_TOPN, n_sel)
    qs = 2 * GQA_B * HEAD_DIM
    Wg = gates.shape[-1]
    cov = _cover_t(n_chunk, n_cmp, n_sel, nselp)
    if kvc_t is None:
        body, values = _cmp_kernel, kvc
        v_spec = pl.BlockSpec((None, None, n_chunk, LANES), lambda b, p, i: (b, 1, 0, p))
    else:
        body, values = _cmp_t_kernel, kvc_t
        v_spec = pl.BlockSpec((None, None, LANES, n_chunk), lambda b, p, i: (b, 1, p, 0))
    return pl.pallas_call(
        functools.partial(body, tq=tq, n_chunk=n_chunk, nselp=nselp, pos_base=pos_base, top_n=top_n),
        grid=(N, 2, Tq // tq),
        in_specs=[
            pl.BlockSpec((None, tq, qs), lambda b, p, i: (b, i, q_off // qs + p)),
            pl.BlockSpec((None, None, n_chunk, LANES), lambda b, p, i: (b, 0, 0, p)),
            v_spec,
            pl.BlockSpec((nselp, n_chunk), lambda b, p, i: (0, 0)),
            pl.BlockSpec((None, tq, LANES), lambda b, p, i: (b, i, g_off // LANES + p)),
        ],
        out_specs=[
            pl.BlockSpec((None, tq, qs), lambda b, p, i: (b, i, p)),
            pl.BlockSpec((None, 2, nselp, tq), lambda b, p, i: (b, p, 0, i)),
            pl.BlockSpec((None, 2, IDX_ROWS, tq), lambda b, p, i: (b, p, 0, i)),
        ],
        out_shape=[
            jax.ShapeDtypeStruct((N, Tq, WIDTH_B), F32),
            jax.ShapeDtypeStruct((N, KV_HEADS_B, nselp, Tq), F32),
            jax.ShapeDtypeStruct((N, KV_HEADS_B, IDX_ROWS, Tq), jnp.int32),
        ],
        compiler_params=_cparams(("parallel", "parallel", "parallel")),
    )(q, kvc, values, cov, gates)


def _sel_prompt_kernel(q_ref, k_ref, vt_ref, sel_ref, g_ref, o_ref, m_ref, acc_ref, *, tq, tk):
    nh, gqa = 2 * GQA_B, GQA_B
    i = pl.program_id(2)
    q0 = i * tq
    qt = _q_transposed(q_ref[...], nh, gqa, tq)
    _init_stats(m_ref, acc_ref)
    nblk = tk // SEL_BLOCK
    rel = lax.broadcasted_iota(jnp.int32, (tk, tq), 1) - lax.broadcasted_iota(jnp.int32, (tk, tq), 0)

    def body(jj, carry):
        ks = pl.multiple_of(jj * tk, tk)
        kt = k_ref[pl.ds(ks, tk), :].astype(BF16)
        vt_t = vt_ref[:, pl.ds(ks, tk)].astype(BF16)
        causal = rel + (q0 - ks) >= 0
        masks = []
        for a in range(2):
            rows = sel_ref[a, pl.ds(pl.multiple_of(jj * nblk, nblk), nblk), :]
            chosen = jnp.concatenate(
                [jnp.broadcast_to(rows[c:c + 1], (SEL_BLOCK, tq)) for c in range(nblk)], axis=0)
            masks += [(chosen > 0.5) & causal] * gqa
        _attend_tile_t(kt, vt_t, qt, masks, tq, m_ref, acc_ref)
        return carry

    lax.fori_loop(0, (q0 + tq - 1) // tk + 1, body, 0)
    _finish_t(o_ref, None, g_ref, m_ref, acc_ref, nh, gqa, tq)


def _sel_prompt(u, vt, sel_t, *, q_off, k_off, g_off, tq=128, tk=512):
    B, S, W = u.shape
    qs = 2 * GQA_B * HEAD_DIM
    nselp = sel_t.shape[2]
    tq = min(tq, S)
    tk = min(tk, S)
    assert tk % SEL_BLOCK == 0 and (tk // SEL_BLOCK) % 8 == 0 and S % tk == 0 and nselp * SEL_BLOCK >= S
    R = 2 * GQA_B * tq
    return pl.pallas_call(
        functools.partial(_sel_prompt_kernel, tq=tq, tk=tk),
        grid=(B, 2, S // tq),
        in_specs=[
            pl.BlockSpec((None, tq, qs), lambda b, p, i: (b, i, q_off // qs + p)),
            pl.BlockSpec((None, S, LANES), lambda b, p, i: (b, 0, k_off // LANES + p)),
            pl.BlockSpec((None, None, LANES, S), lambda b, p, i: (b, 0, p, 0)),
            pl.BlockSpec((None, 2, nselp, tq), lambda b, p, i: (b, p, 0, i)),
            pl.BlockSpec((None, tq, LANES), lambda b, p, i: (b, i, g_off // LANES + p)),
        ],
        out_specs=pl.BlockSpec((None, tq, qs), lambda b, p, i: (b, i, p)),
        out_shape=jax.ShapeDtypeStruct((B, S, WIDTH_B), F32),
        scratch_shapes=[pltpu.VMEM((1, R), F32), pltpu.VMEM((ACC_ROWS, R), F32)],
        compiler_params=_cparams(("parallel", "parallel", "arbitrary")),
    )(u, u, vt, sel_t, u)


def _sel_sample_kernel(pt_ref, idx_ref, q_ref, kn_ref, vn_ref, g_ref, pool_ref, o_ref, kvbuf, sem, *,
                       nt, n_pages, n_pick):
    nh, gqa, tq = 2 * GQA_B, GQA_B, SAMPLE_ROWS
    n, p, a = pl.program_id(0), pl.program_id(1), pl.program_id(2)
    step = (n * 2 + p) * 2 + a
    n_steps = pl.num_programs(0) * 4
    blocks_per_page = PAGE_SIZE // SEL_BLOCK
    width = n_pick * PAGE_SIZE

    def picked_block(seq, pair, head, t, r):
        src_row = jnp.where(r == 0, 0, r + 1)
        return idx_ref[((seq * KV_HEADS_B + pair * 2 + head) * IDX_ROWS + src_row) * tq + t]

    def page_copy(st, c, lookup):
        seq, pair, head = st // 4, (st // 2) % 2, st % 2
        t, r = c // n_pick, c % n_pick
        page = pt_ref[seq * n_pages + picked_block(seq, pair, head, t, r) // blocks_per_page] if lookup else 0
        dst0 = pl.multiple_of(r * PAGE_SIZE, PAGE_SIZE)
        half0 = pl.multiple_of(head * HEAD_DIM, HEAD_DIM)
        row0 = pl.multiple_of(pair * LANES + head * HEAD_DIM, HEAD_DIM)
        return pltpu.make_async_copy(
            pool_ref.at[page, pl.ds(2, 2), pl.ds(row0, HEAD_DIM), :],
            kvbuf.at[head, t, :, pl.ds(half0, HEAD_DIM), pl.ds(dst0, PAGE_SIZE)], sem.at[head])

    def issue(st):
        def go(c, carry):
            page_copy(st, c, True).start()
            return carry
        lax.fori_loop(0, nt * n_pick, go, 0, unroll=4)

    @pl.when(step == 0)
    def _():
        kvbuf[0, :, :, HEAD_DIM:, :] = jnp.zeros((nt, 2, HEAD_DIM, width), F32)
        kvbuf[1, :, :, :HEAD_DIM, :] = jnp.zeros((nt, 2, HEAD_DIM, width), F32)
        issue(step)

    @pl.when(step + 1 < n_steps)
    def _():
        issue(step + 1)

    slot = a

    def wait(c, carry):
        page_copy(step, c, False).wait()
        return carry
    lax.fori_loop(0, nt * n_pick, wait, 0, unroll=4)

    R = nh * tq
    Q = _prep_q(q_ref[...], nh, gqa, tq)
    pad = jnp.zeros((LANES - tq, LANES), F32)
    kn = jnp.concatenate([kn_ref[...], pad], axis=0).astype(BF16)
    vn = jnp.concatenate([vn_ref[...], pad], axis=0).astype(BF16)
    row = lax.broadcasted_iota(jnp.int32, (R, LANES), 0)
    colk = lax.broadcasted_iota(jnp.int32, (R, LANES), 1)
    trow = row & (tq - 1)
    s_new = jnp.where((colk <= trow) & (colk < nt), _nt_dot(Q, kn), NEG)
    lane = lax.broadcasted_iota(jnp.int32, (1, width), 1)
    lane_pick = lane // PAGE_SIZE
    lane_half = (lane // SEL_BLOCK) % blocks_per_page
    o = jnp.zeros((R, LANES), F32)
    for t in range(nt):
        chosen = lane < 0
        for r in range(n_pick):
            half = picked_block(n, p, a, t, r) % blocks_per_page
            chosen = chosen | ((lane_pick == r) & (lane_half == half))
        s = jnp.where(chosen, _dot(Q, kvbuf[slot, t, 0].astype(BF16)), NEG)
        m = jnp.maximum(jnp.max(s, axis=-1, keepdims=True), jnp.max(s_new, axis=-1, keepdims=True))
        p1 = jnp.exp(s - m)
        p2 = jnp.exp(s_new - m)
        l = jnp.sum(p1, axis=-1, keepdims=True) + jnp.sum(p2, axis=-1, keepdims=True)
        o_t = (_nt_dot(p1.astype(BF16), kvbuf[slot, t, 1].astype(BF16)) + _dot(p2.astype(BF16), vn)) / l
        o = jnp.where(trow == t, o_t, o)

    g = g_ref[...]
    low = lax.broadcasted_iota(jnp.int32, (tq, LANES), 1) < HEAD_DIM
    first = a == 0
    placed = []
    for jj in range(gqa):
        oj = jnp.where(first, o[jj * tq:(jj + 1) * tq], o[(gqa + jj) * tq:(gqa + jj + 1) * tq])
        gj = jnp.where(first, g[:, jj:jj + 1], g[:, gqa + jj:gqa + jj + 1])
        oj = oj * gj
        placed.append(jnp.where(a == jj % 2, oj, pltpu.roll(oj, HEAD_DIM, axis=1)))
    o_ref[...] = jnp.concatenate([jnp.where(low, placed[0], placed[1]),
                                  jnp.where(low, placed[2], placed[3])], axis=1)


def _sel_sample(us, idx, pool, page_table, *, nt, q_off, k_off, v_off, g_off):
    N, tq, W = us.shape
    n_pages = page_table.shape[1]
    n_pick = SEL_TOPN - 1
    qs = 2 * GQA_B * HEAD_DIM
    hw = GQA_B * HEAD_DIM
    gs = pltpu.PrefetchScalarGridSpec(
        num_scalar_prefetch=2,
        grid=(N, 2, 2),
        in_specs=[
            pl.BlockSpec((None, tq, qs), lambda b, p, a, *_: (b, 0, q_off // qs + p)),
            pl.BlockSpec((None, tq, LANES), lambda b, p, a, *_: (b, 0, k_off // LANES + p)),
            pl.BlockSpec((None, tq, LANES), lambda b, p, a, *_: (b, 0, v_off // LANES + p)),
            pl.BlockSpec((None, tq, LANES), lambda b, p, a, *_: (b, 0, g_off // LANES + p)),
            pl.BlockSpec(memory_space=pl.ANY),
        ],
        out_specs=pl.BlockSpec((None, tq, hw), lambda b, p, a, *_: (b, 0, 2 * p + a)),
        scratch_shapes=[
            pltpu.VMEM((2, nt, 2, LANES, n_pick * PAGE_SIZE), F32),
            pltpu.SemaphoreType.DMA((2,)),
        ],
    )
    return pl.pallas_call(
        functools.partial(_sel_sample_kernel, nt=nt, n_pages=n_pages, n_pick=n_pick),
        grid_spec=gs,
        out_shape=jax.ShapeDtypeStruct((N, tq, WIDTH_B), F32),
        compiler_params=_cparams(("arbitrary", "arbitrary", "arbitrary")),
    )(page_table.reshape(-1), idx.reshape(-1), us, us, us, us, pool)


def _cache_attn_kernel(*refs, nh, gqa, L, dil, win, nt, want_lse, gated):
    q_ref, kc_ref, vc_ref, kn_ref, vn_ref = refs[:5]
    pos = 5
    g_ref = None
    if gated:
        g_ref = refs[pos]
        pos += 1
    o_ref = refs[pos]
    lse_ref = refs[pos + 1] if want_lse else None
    tq = SAMPLE_ROWS
    R = nh * tq
    Q = _prep_q(q_ref[...], nh, gqa, tq)
    kc_t = kc_ref[...].astype(BF16)
    vc_t = vc_ref[...].astype(BF16)
    pad = jnp.zeros((LANES - tq, LANES), F32)
    kn = jnp.concatenate([kn_ref[...], pad], axis=0).astype(BF16)
    vn = jnp.concatenate([vn_ref[...], pad], axis=0).astype(BF16)
    t1 = lax.broadcasted_iota(jnp.int32, (R, L), 0) & (tq - 1)
    d1 = L + t1 - lax.broadcasted_iota(jnp.int32, (R, L), 1)
    s1 = jnp.where(((d1 & (dil - 1)) == 0) & (d1 <= win), _dot(Q, kc_t), NEG)
    t2 = lax.broadcasted_iota(jnp.int32, (R, LANES), 0) & (tq - 1)
    c2 = lax.broadcasted_iota(jnp.int32, (R, LANES), 1)
    d2 = t2 - c2
    s2 = jnp.where((d2 >= 0) & ((d2 & (dil - 1)) == 0) & (d2 <= win) & (c2 < nt), _nt_dot(Q, kn), NEG)
    m = jnp.maximum(jnp.max(s1, axis=-1, keepdims=True), jnp.max(s2, axis=-1, keepdims=True))
    p1 = jnp.exp(s1 - m)
    p2 = jnp.exp(s2 - m)
    l = jnp.sum(p1, axis=-1, keepdims=True) + jnp.sum(p2, axis=-1, keepdims=True)
    o = (_nt_dot(p1.astype(BF16), vc_t) + _dot(p2.astype(BF16), vn)) / l
    o = _apply_gates(o, g_ref, nh, tq)
    o_ref[...] = _unprep_o(o, nh, gqa, tq)
    if want_lse:
        lse_ref[...] = _unprep_o(jnp.broadcast_to(m + jnp.log(l), (R, LANES)), nh, gqa, tq)


def _feature_major(x):
    nd = x.ndim
    xt = jnp.transpose(x, (0, 1) + tuple(range(3, nd)) + (2,))
    return xt.reshape(x.shape[0], x.shape[1], -1, x.shape[2])


def _cache_attention(us, cache_t, layer, *, q_off, k_off, v_off, nh, gqa, n_slabs, dil, win, nt, want_lse,
                     g_off=None):
    N, tq, W = us.shape
    L = cache_t.shape[3]
    cache = cache_t
    qs = nh * HEAD_DIM
    wo = n_slabs * qs
    in_specs = [
        pl.BlockSpec((None, tq, qs), lambda b, p: (b, 0, q_off // qs + p)),
        pl.BlockSpec((None, None, LANES, L), lambda b, p: (layer, b, p, 0)),
        pl.BlockSpec((None, None, LANES, L), lambda b, p: (layer, b, n_slabs + p, 0)),
        pl.BlockSpec((None, tq, LANES), lambda b, p: (b, 0, k_off // LANES + p)),
        pl.BlockSpec((None, tq, LANES), lambda b, p: (b, 0, v_off // LANES + p)),
    ]
    args = [us, cache, cache, us, us]
    if g_off is not None:
        in_specs.append(pl.BlockSpec((None, tq, LANES), lambda b, p: (b, 0, g_off // LANES + p)))
        args.append(us)
    o_spec = pl.BlockSpec((None, tq, qs), lambda b, p: (b, 0, p))
    o_shape = jax.ShapeDtypeStruct((N, tq, wo), F32)
    return pl.pallas_call(
        functools.partial(_cache_attn_kernel, nh=nh, gqa=gqa, L=L, dil=dil, win=win, nt=nt,
                          want_lse=want_lse, gated=g_off is not None),
        grid=(N, n_slabs),
        in_specs=in_specs,
        out_specs=[o_spec, o_spec] if want_lse else o_spec,
        out_shape=[o_shape, o_shape] if want_lse else o_shape,
        compiler_params=_cparams(("parallel", "parallel")),
    )(*args)


def _merge_a_kernel(o0, o1, o2, l0, l1, l2, z_ref, x_ref, w_ref, y_ref):
    a0, a1, a2 = l0[...], l1[...], l2[...]
    mx = jnp.maximum(jnp.maximum(a0, a1), a2)
    e0, e1, e2 = jnp.exp(a0 - mx), jnp.exp(a1 - mx), jnp.exp(a2 - mx)
    den = e0 + e1 + e2
    o = (e0 / den) * o0[...] + (e1 / den) * o1[...] + (e2 / den) * o2[...]
    y_ref[...] = x_ref[...] + _dot((o * jax.nn.silu(z_ref[...])).astype(BF16), w_ref[...])


def _merge_a(outs, lses, u, z_off, x, w_out_bf16):
    T, D = x.shape
    tm = min(T, 512)
    wa = WIDTH_A
    row = pl.BlockSpec((tm, wa), lambda i: (i, 0))
    return pl.pallas_call(
        _merge_a_kernel,
        grid=(T // tm,),
        in_specs=[row] * 6 + [
            pl.BlockSpec((tm, wa), lambda i: (i, z_off // wa)),
            pl.BlockSpec((tm, D), lambda i: (i, 0)),
            pl.BlockSpec((wa, D), lambda i: (0, 0)),
        ],
        out_specs=pl.BlockSpec((tm, D), lambda i: (i, 0)),
        out_shape=jax.ShapeDtypeStruct((T, D), F32),
        compiler_params=_cparams(("parallel",)),
    )(*outs, *lses, u, x, w_out_bf16)


def _merge_b_kernel(oc, os_, ow, z_ref, x_ref, w_ref, y_ref):
    o = oc[...] + os_[...] + ow[...]
    y_ref[...] = x_ref[...] + _dot((o * jax.nn.silu(z_ref[...])).astype(BF16), w_ref[...])


def _merge_b(o_c, o_s, o_w, u, z_off, x, w_out_bf16):
    T, D = x.shape
    tm = min(T, 512)
    wb = WIDTH_B
    row = pl.BlockSpec((tm, wb), lambda i: (i, 0))
    return pl.pallas_call(
        _merge_b_kernel,
        grid=(T // tm,),
        in_specs=[row] * 3 + [
            pl.BlockSpec((tm, wb), lambda i: (i, z_off // wb)),
            pl.BlockSpec((tm, D), lambda i: (i, 0)),
            pl.BlockSpec((wb, D), lambda i: (0, 0)),
        ],
        out_specs=pl.BlockSpec((tm, D), lambda i: (i, 0)),
        out_shape=jax.ShapeDtypeStruct((T, D), F32),
        compiler_params=_cparams(("parallel",)),
    )(o_c, o_s, o_w, u, x, w_out_bf16)


A_KINDS = ([EP_ROPE_Q] * (N_GROUPS_A * WIDTH_A // PROJ_TN) + [EP_ROPE_K] * (N_GROUPS_A * WIDTH_A // PROJ_TN)
           + [EP_NONE] * (N_GROUPS_A * WIDTH_A // PROJ_TN) + [EP_NONE] * (WIDTH_A // PROJ_TN))
A_K, A_V, A_Z = N_GROUPS_A * WIDTH_A, 2 * N_GROUPS_A * WIDTH_A, QKV_A

B_KINDS = ([EP_NORM_Q] * 4 + [EP_ROPE_Q] * 4 + [EP_NONE] * 4
           + [EP_NORM_K, EP_NONE, EP_ROPE_K, EP_NONE, EP_ROPE_K, EP_NONE] + [EP_SIG] * 3)


def _b_weight(w_in):
    wq = w_in[:, :WIDTH_B]
    o1 = WIDTH_B + 6 * KVW_B
    wkv = w_in[:, WIDTH_B:o1]
    wg = w_in[:, o1:o1 + 3 * HEADS_B]
    wz = w_in[:, o1 + 3 * HEADS_B:]
    nh = 2 * GQA_B
    tiles = []
    for b in range(3):
        for p in range(2):
            g = wg[:, b * HEADS_B + p * nh: b * HEADS_B + (p + 1) * nh]
            tiles.append(jnp.pad(g, ((0, 0), (0, LANES - nh))))
    return jnp.concatenate([wq, wq, wz, wkv] + tiles, axis=1).astype(BF16)


def _pad_rows(u, N, T):
    return jnp.pad(u.reshape(N, T, -1), ((0, 0), (0, SAMPLE_ROWS - T), (0, 0)))


def _layer_a_prompt(x, norm_g, w_bf16, q_gain, k_gain, w_out_bf16, tabs):
    B, S, D = x.shape
    u = _project(x.reshape(B * S, D), norm_g, w_bf16, A_KINDS, tabs[0], tabs[1], q_gain, k_gain)
    u3 = u.reshape(B, S, -1)
    outs, lses, states = [], [], []
    for g, (win, dil) in enumerate(DIL_PATTERNS):
        vt = _values_transposed(u3, A_V + g * WIDTH_A, WIDTH_A, dil)
        if dil == 1:
            qk, q_off, k_off = u3, g * WIDTH_A, A_K + g * WIDTH_A
        else:
            qk = jnp.concatenate([u3[:, :, g * WIDTH_A:(g + 1) * WIDTH_A],
                                  u3[:, :, A_K + g * WIDTH_A:A_K + (g + 1) * WIDTH_A]], axis=2)
            q_off, k_off = 0, WIDTH_A
        o, lse = _band_attention(qk, vt, q_off=q_off, k_off=k_off,
                                 nh=2, gqa=1, n_slabs=HEADS_A // 2, kw=win // dil, dil=dil, want_lse=True)
        outs.append(o.reshape(B * S, WIDTH_A))
        lses.append(lse.reshape(B * S, WIDTH_A))
        w = min(win, S)
        kg = u3[:, S - w:, A_K + g * WIDTH_A: A_K + (g + 1) * WIDTH_A]
        vg = u3[:, S - w:, A_V + g * WIDTH_A: A_V + (g + 1) * WIDTH_A]
        states.append(jnp.stack([kg, vg], axis=2).reshape(B, w, 2, HEADS_A, HEAD_DIM))
    y = _merge_a(outs, lses, u, A_Z, x.reshape(B * S, D), w_out_bf16)
    return y.reshape(B, S, D), states


def _layer_a_sample(x, caches_t, layer, norm_g, w_bf16, q_gain, k_gain, w_out_bf16, tabs):
    N, T, D = x.shape
    u = _project(x.reshape(N * T, D), norm_g, w_bf16, A_KINDS, tabs[0], tabs[1], q_gain, k_gain)
    us = _pad_rows(u, N, T)
    outs, lses, new_rows = [], [], []
    for g, (win, dil) in enumerate(DIL_PATTERNS):
        o, lse = _cache_attention(us, caches_t[g], layer, q_off=g * WIDTH_A, k_off=A_K + g * WIDTH_A,
                                  v_off=A_V + g * WIDTH_A, nh=2, gqa=1, n_slabs=HEADS_A // 2,
                                  dil=dil, win=win, nt=T, want_lse=True)
        outs.append(o[:, :T].reshape(N * T, WIDTH_A))
        lses.append(lse[:, :T].reshape(N * T, WIDTH_A))
        new_rows.append(jnp.concatenate([us[:, :, A_K + g * WIDTH_A: A_K + (g + 1) * WIDTH_A],
                                         us[:, :, A_V + g * WIDTH_A: A_V + (g + 1) * WIDTH_A]], axis=2))
    y = _merge_a(outs, lses, u, A_Z, x.reshape(N * T, D), w_out_bf16)
    return y.reshape(N, T, D), new_rows


def _layer_b_prompt(x, norm_g, w_bf16, q_gain, k_gain, cw, w_out_bf16, tabs):
    B, S, D = x.shape
    u = _project(x.reshape(B * S, D), norm_g, w_bf16, B_KINDS, tabs[0], tabs[1], q_gain, k_gain)
    u3 = u.reshape(B, S, -1)
    kvc, kvc_t = _compress_prompt(u3, B_KV, cw)
    o_c, sel_t, _ = _cmp_attention(u3, B_QC, kvc, u3, B_GATE, tq=min(256, S), pos_base=0, n_keys=S,
                                   kvc_t=kvc_t)
    o_s = _sel_prompt(u3, _values_transposed(u3, B_KV + 3 * KVW_B, KVW_B, 1), sel_t, q_off=B_QR,
                      k_off=B_KV + 2 * KVW_B, g_off=B_GATE + 2 * LANES)
    o_w = _band_attention(u3, _values_transposed(u3, B_WIN + KVW_B, KVW_B, 1), q_off=B_QR, k_off=B_WIN,
                          nh=2 * GQA_B, gqa=GQA_B, n_slabs=2, kw=WIN_B, dil=1, want_lse=False, gates=u3,
                          g_off=B_GATE + 4 * LANES, tq=128, tk=256)
    y = _merge_b(o_c.reshape(B * S, -1), o_s.reshape(B * S, -1), o_w.reshape(B * S, -1), u, B_Z,
                 x.reshape(B * S, D), w_out_bf16)
    rows = u3[:, :, B_KV:B_KV + 4 * KVW_B].reshape(B, S, 4, KV_HEADS_B, HEAD_DIM)
    w = min(WIN_B, S)
    wrows = u3[:, S - w:, B_WIN:B_WIN + 2 * KVW_B].reshape(B, w, 2, KV_HEADS_B, HEAD_DIM)
    return y.reshape(B, S, D), rows, wrows


def _layer_b_sample(x, pool_t, page_table, win_t, layer, norm_g, w_bf16, q_gain, k_gain, cw, w_out_bf16, tabs):
    N, T, D = x.shape
    past_len = page_table.shape[1] * PAGE_SIZE
    assert past_len % SEL_BLOCK == 0 and T <= SEL_BLOCK and T <= SAMPLE_ROWS
    u = _project(x.reshape(N * T, D), norm_g, w_bf16, B_KINDS, tabs[0], tabs[1], q_gain, k_gain)
    us = _pad_rows(u, N, T)
    u3 = u.reshape(N, T, -1)
    kvc = _compress_sample(pool_t, page_table, cw)
    o_c, _, idx = _cmp_attention(us, B_QC, kvc, us, B_GATE, tq=SAMPLE_ROWS, pos_base=past_len,
                                 n_keys=past_len + T)
    o_s = _sel_sample(us, idx, pool_t, page_table, nt=T, q_off=B_QR, k_off=B_KV + 2 * KVW_B,
                      v_off=B_KV + 3 * KVW_B, g_off=B_GATE + 2 * LANES)
    o_w = _cache_attention(us, win_t, layer, q_off=B_QR, k_off=B_WIN, v_off=B_WIN + KVW_B,
                           nh=2 * GQA_B, gqa=GQA_B, n_slabs=2, dil=1, win=WIN_B, nt=T, want_lse=False,
                           g_off=B_GATE + 4 * LANES)
    y = _merge_b(o_c[:, :T].reshape(N * T, -1), o_s[:, :T].reshape(N * T, -1), o_w[:, :T].reshape(N * T, -1),
                 u, B_Z, x.reshape(N * T, D), w_out_bf16)
    rows = u3[:, :, B_KV:B_KV + 4 * KVW_B].reshape(N, T, 4, KV_HEADS_B, HEAD_DIM)
    return y.reshape(N, T, D), rows, us[:, :, B_WIN:B_WIN + 2 * KVW_B]


def _append_kernel(c_ref, n_ref, o_ref, *, L, nt, fb):
    rolled = pltpu.roll(c_ref[...], L - nt, axis=1)
    pad = jnp.zeros((LANES - SAMPLE_ROWS, LANES), F32)
    new = n_ref[...]
    new_t = jnp.concatenate(
        [jnp.concatenate([new[:, c * LANES:(c + 1) * LANES], pad], axis=0).T for c in range(fb // LANES)], axis=0)
    new_t = pltpu.roll(new_t, LANES - nt, axis=1)
    lane = lax.broadcasted_iota(jnp.int32, (fb, LANES), 1)
    if L > LANES:
        o_ref[:, :L - LANES] = rolled[:, :L - LANES]
    o_ref[:, L - LANES:] = jnp.where(lane >= LANES - nt, new_t, rolled[:, L - LANES:])


def _append_cache(cache, cache_t, new_rows, nt, win):
    J, N, F, L = cache_t.shape
    tail = cache.shape[3:]
    if L + nt <= win or L % LANES:
        new = new_rows[:, :, :nt].reshape((J, N, nt) + tail)
        return jnp.concatenate([cache, new], axis=2)[:, :, -min(win, L + nt):]
    assert L == win
    fb = min(F, LANES * max(1, 2048 // L))
    assert F % fb == 0
    out_t = pl.pallas_call(
        functools.partial(_append_kernel, L=L, nt=nt, fb=fb),
        grid=(J, N, F // fb),
        in_specs=[pl.BlockSpec((None, None, fb, L), lambda j, n, f: (j, n, f, 0)),
                  pl.BlockSpec((None, None, SAMPLE_ROWS, fb), lambda j, n, f: (j, n, 0, f))],
        out_specs=pl.BlockSpec((None, None, fb, L), lambda j, n, f: (j, n, f, 0)),
        out_shape=jax.ShapeDtypeStruct((J, N, F, L), F32),
        compiler_params=_cparams(("parallel", "parallel", "parallel")),
    )(cache_t, new_rows)
    nd = len(tail)
    out = out_t.reshape((J, N) + tail + (L,))
    return jnp.transpose(out, (0, 1, nd + 2) + tuple(range(2, nd + 2)))


def kernel(x_prompt, x_sample, cache_dil_0, cache_dil_1, cache_dil_2, cache_nsa_paged, cache_nsa_win,
           page_table, a_norm, a_w_in, a_q_norm, a_k_norm, a_w_out, b_norm, b_w_in, b_q_norm, b_k_norm,
           b_cmp_pe, b_cmp_w1, b_cmp_w2, b_w_out):
    dil_caches = (cache_dil_0, cache_dil_1, cache_dil_2)
    B, S, _ = x_prompt.shape
    N, T, _ = x_sample.shape
    past_len = page_table.shape[1] * PAGE_SIZE
    depth = a_norm.shape[0] + b_norm.shape[0]
    n_pool = cache_nsa_paged.shape[1]
    tabs_p = _rope_tables(jnp.tile(jnp.arange(S), B))
    tabs_s = _rope_tables(jnp.tile(past_len + jnp.arange(T), N))
    dil_t = [_feature_major(c) for c in dil_caches]
    win_t = _feature_major(cache_nsa_win)
    pool_t = _feature_major(cache_nsa_paged)
    pool_t = pool_t.reshape(-1, 4, KVW_B, PAGE_SIZE)
    xp, xs = x_prompt, x_sample
    dil_p = [[] for _ in DIL_PATTERNS]
    dil_new = [[] for _ in DIL_PATTERNS]
    rows_p, rows_s, win_p, win_new = [], [], [], []
    for layer in range(depth):
        j = layer // 2
        if layer % 2 == 0:
            w = a_w_in[j].astype(BF16)
            wo = a_w_out[j].astype(BF16)
            xp, st_p = _layer_a_prompt(xp, a_norm[j], w, a_q_norm[j], a_k_norm[j], wo, tabs_p)
            xs, new_s = _layer_a_sample(xs, dil_t, j, a_norm[j], w, a_q_norm[j], a_k_norm[j], wo, tabs_s)
            for g in range(N_GROUPS_A):
                dil_p[g].append(st_p[g])
                dil_new[g].append(new_s[g])
        else:
            w = _b_weight(b_w_in[j])
            wo = b_w_out[j].astype(BF16)
            cw = _compress_weights(b_cmp_pe[j], b_cmp_w1[j], b_cmp_w2[j], b_k_norm[j])
            xp, rp, wp = _layer_b_prompt(xp, b_norm[j], w, b_q_norm[j], b_k_norm[j], cw, wo, tabs_p)
            xs, rs, wn = _layer_b_sample(xs, pool_t, page_table + j * n_pool, win_t, j, b_norm[j], w,
                                         b_q_norm[j], b_k_norm[j], cw, wo, tabs_s)
            rows_p.append(rp)
            win_p.append(wp)
            rows_s.append(rs)
            win_new.append(wn)
    dil_s = [_append_cache(dil_caches[g], dil_t[g], jnp.stack(dil_new[g]), T, DIL_PATTERNS[g][0])
             for g in range(N_GROUPS_A)]
    win_s = _append_cache(cache_nsa_win, win_t, jnp.stack(win_new), T, WIN_B)
    return (xp, xs,
            jnp.stack(dil_p[0]), jnp.stack(dil_p[1]), jnp.stack(dil_p[2]),
            jnp.stack(rows_p), jnp.stack(win_p),
            dil_s[0], dil_s[1], dil_s[2],
            jnp.stack(rows_s), win_s)
```

```python
import functools

import jax
import jax.numpy as jnp
from jax import lax
from jax.experimental import pallas as pl
from jax.experimental.pallas import tpu as pltpu

F32 = jnp.float32
BF16 = jnp.bfloat16

HEAD_DIM = 64
SCALE = HEAD_DIM ** -0.5
ROPE_THETA = 10000.0
EPS = 1e-6
TINY = 1e-30
NEG = -1e30
DIL_PATTERNS = ((128, 1), (512, 4), (2048, 16))
N_GROUPS_A = len(DIL_PATTERNS)
HEADS_A = 8
WIDTH_A = HEADS_A * HEAD_DIM
QKV_A = 3 * N_GROUPS_A * WIDTH_A
KV_HEADS_B = 4
GQA_B = 4
HEADS_B = KV_HEADS_B * GQA_B
WIDTH_B = HEADS_B * HEAD_DIM
KVW_B = KV_HEADS_B * HEAD_DIM
CMP_LEN = 32
CMP_STRIDE = 16
CMP_RATIO = CMP_LEN // CMP_STRIDE
CMP_HIDDEN = 128
SEL_BLOCK = 64
SEL_TOPN = 16
WIN_B = 512
PAGE_SIZE = 128

LANES = 128
PROJ_TN = 256
VMEM_LIMIT = 56 * 1024 * 1024
IDX_ROWS = 24
SAMPLE_ROWS = 8

B_QC, B_QR, B_Z, B_KV, B_WIN, B_GATE = 0, 1024, 2048, 3072, 4096, 4608
B_COLS = B_GATE + 6 * LANES

EP_NONE, EP_NORM_Q, EP_ROPE_Q, EP_NORM_K, EP_ROPE_K, EP_SIG = range(6)


def _cparams(sem):
    return pltpu.CompilerParams(dimension_semantics=sem, vmem_limit_bytes=VMEM_LIMIT)


def _nt_dot(a, b):
    return lax.dot_general(a, b, (((1,), (1,)), ((), ())), preferred_element_type=F32)


def _dot(a, b):
    return jnp.dot(a, b, preferred_element_type=F32)


def _head_norm(acc, bd_ref, gain):
    msq = _dot((acc * acc).astype(BF16), bd_ref[...])
    return acc * lax.rsqrt(msq + EPS) * gain


def _rope_tile(y, cos, sin_signed):
    lane = lax.broadcasted_iota(jnp.int32, (y.shape[0], LANES), 1)
    first_half = (lane & (HEAD_DIM - 1)) < (HEAD_DIM // 2)
    outs = []
    for c in range(y.shape[1] // LANES):
        yc = y[:, c * LANES:(c + 1) * LANES]
        partner = jnp.where(first_half, pltpu.roll(yc, LANES - HEAD_DIM // 2, axis=1),
                            pltpu.roll(yc, HEAD_DIM // 2, axis=1))
        outs.append(yc * cos + partner * sin_signed)
    return jnp.concatenate(outs, axis=1)


def _proj_kernel(x_ref, g_ref, w_ref, cos_ref, sin_ref, qg_ref, kg_ref, bd_ref, o_ref, h_ref, *, kinds):
    j = pl.program_id(1)

    @pl.when(j == 0)
    def _():
        x = x_ref[...]
        r = lax.rsqrt(jnp.mean(x * x, axis=-1, keepdims=True) + EPS)
        h_ref[...] = (x * r * g_ref[...]).astype(BF16)

    acc = _dot(h_ref[...], w_ref[...])

    def ranges(kind):
        out, start = [], None
        for t, k in enumerate(list(kinds) + [None]):
            if k == kind and start is None:
                start = t
            if k != kind and start is not None:
                out.append((start, t))
                start = None
        return out

    def emit(kind, fn):
        rs = ranges(kind)
        if not rs:
            return
        cond = None
        for a, b in rs:
            c = (j >= a) & (j < b)
            cond = c if cond is None else (cond | c)

        @pl.when(cond)
        def _():
            o_ref[...] = fn()

    emit(EP_NONE, lambda: acc)
    emit(EP_SIG, lambda: jax.nn.sigmoid(acc))
    emit(EP_NORM_Q, lambda: _head_norm(acc, bd_ref, qg_ref[...]))
    emit(EP_NORM_K, lambda: _head_norm(acc, bd_ref, kg_ref[...]))
    emit(EP_ROPE_Q, lambda: _rope_tile(_head_norm(acc, bd_ref, qg_ref[...]), cos_ref[...], sin_ref[...]))
    emit(EP_ROPE_K, lambda: _rope_tile(_head_norm(acc, bd_ref, kg_ref[...]), cos_ref[...], sin_ref[...]))


def _project(x, norm_g, w_bf16, kinds, cos_t, sin_t, q_gain, k_gain):
    T, D = x.shape
    E = w_bf16.shape[1]
    tn = PROJ_TN
    assert E == len(kinds) * tn
    tm = min(T, 1024)
    assert T % tm == 0
    rep = tn // HEAD_DIM
    qg = jnp.tile(q_gain.astype(F32), rep)[None]
    kg = jnp.tile(k_gain.astype(F32), rep)[None]
    hid = jnp.arange(tn) // HEAD_DIM
    bd = ((hid[:, None] == hid[None, :]).astype(F32) / HEAD_DIM).astype(BF16)
    return pl.pallas_call(
        functools.partial(_proj_kernel, kinds=tuple(kinds)),
        grid=(T // tm, E // tn),
        in_specs=[
            pl.BlockSpec((tm, D), lambda i, j: (i, 0)),
            pl.BlockSpec((1, D), lambda i, j: (0, 0)),
            pl.BlockSpec((D, tn), lambda i, j: (0, j)),
            pl.BlockSpec((tm, LANES), lambda i, j: (i, 0)),
            pl.BlockSpec((tm, LANES), lambda i, j: (i, 0)),
            pl.BlockSpec((1, tn), lambda i, j: (0, 0)),
            pl.BlockSpec((1, tn), lambda i, j: (0, 0)),
            pl.BlockSpec((tn, tn), lambda i, j: (0, 0)),
        ],
        out_specs=pl.BlockSpec((tm, tn), lambda i, j: (i, j)),
        out_shape=jax.ShapeDtypeStruct((T, E), F32),
        scratch_shapes=[pltpu.VMEM((tm, D), BF16)],
        compiler_params=_cparams(("parallel", "arbitrary")),
    )(x, norm_g.astype(F32)[None], w_bf16, cos_t, sin_t, qg, kg, bd)


def _rope_tables(pos):
    half = HEAD_DIM // 2
    inv_freq = ROPE_THETA ** (-jnp.arange(half, dtype=F32) / half)
    ang = pos.astype(F32)[:, None] * inv_freq[None, :]
    cos, sin = jnp.cos(ang), jnp.sin(ang)
    cos_t = jnp.concatenate([cos, cos, cos, cos], axis=1)
    sin_t = jnp.concatenate([-sin, sin, -sin, sin], axis=1)
    return cos_t, sin_t


def _prep_q(q, nh, gqa, tq):
    lane = lax.broadcasted_iota(jnp.int32, (tq, LANES), 1)
    low = lane < HEAD_DIM
    parts = []
    for j in range(nh):
        col = q[:, (j // 2) * LANES:(j // 2 + 1) * LANES]
        nat, tgt = j % 2, (j // gqa) % 2
        if nat != tgt:
            col = pltpu.roll(col, HEAD_DIM, axis=1)
        keep = low if tgt == 0 else jnp.logical_not(low)
        parts.append(jnp.where(keep, col * SCALE, 0.0))
    return jnp.concatenate(parts, axis=0).astype(BF16)


def _unprep_o(o, nh, gqa, tq):
    lane = lax.broadcasted_iota(jnp.int32, (tq, LANES), 1)
    low = lane < HEAD_DIM
    cols = []
    for c in range(nh // 2):
        halves = []
        for nat in (0, 1):
            j = 2 * c + nat
            tgt = (j // gqa) % 2
            oj = o[j * tq:(j + 1) * tq]
            if tgt != nat:
                oj = pltpu.roll(oj, HEAD_DIM, axis=1)
            halves.append(oj)
        cols.append(jnp.where(low, halves[0], halves[1]))
    return jnp.concatenate(cols, axis=1)


def _apply_gates(o, g_ref, nh, tq):
    if g_ref is None:
        return o
    g = g_ref[...]
    return jnp.concatenate([o[j * tq:(j + 1) * tq] * g[:, j:j + 1] for j in range(nh)], axis=0)


def _q_transposed(q, nh, gqa, tq):
    qt = (q * SCALE).T
    zero = jnp.zeros((HEAD_DIM, tq), F32)
    cols = []
    for j in range(nh):
        h = qt[j * HEAD_DIM:(j + 1) * HEAD_DIM]
        cols.append(jnp.concatenate([h, zero] if (j // gqa) % 2 == 0 else [zero, h], axis=0))
    return jnp.concatenate(cols, axis=1).astype(BF16)


def _masked_heads(s, masks, tq):
    return jnp.concatenate([jnp.where(mk, s[:, j * tq:(j + 1) * tq], NEG) for j, mk in enumerate(masks)], axis=1)


ACC_ROWS = LANES + 16


def _attend_tile_t(kt, vt_t, qts, masks, tq, m_ref, acc_ref):
    n_sl = len(qts)
    rs = qts[0].shape[1]
    s = jnp.concatenate([_dot(kt[:, sl * LANES:(sl + 1) * LANES], qts[sl]) for sl in range(n_sl)], axis=1)
    s = _masked_heads(s, masks, tq)
    m_old = m_ref[...]
    m_new = jnp.maximum(m_old, jnp.max(s, axis=0, keepdims=True))
    alpha = jnp.exp(m_old - m_new)
    p = jnp.exp((s - m_new).astype(BF16))
    ones = jnp.ones((ACC_ROWS - LANES, kt.shape[0]), BF16)
    for sl in range(n_sl):
        cols = slice(sl * rs, (sl + 1) * rs)
        vt1 = jnp.concatenate([vt_t[sl * LANES:(sl + 1) * LANES], ones], axis=0)
        acc_ref[sl] = alpha[:, cols] * acc_ref[sl] + _dot(vt1, p[:, cols])
    m_ref[...] = m_new


def _init_stats(m_ref, acc_ref):
    m_ref[...] = jnp.full(m_ref.shape, NEG, F32)
    acc_ref[...] = jnp.zeros(acc_ref.shape, F32)


def _finish_t(o_ref, lse_ref, g_ref, m_ref, acc_ref, nh, gqa, tq):
    n_sl = acc_ref.shape[0]
    g_t = g_ref[...].T if g_ref is not None else None
    parts, lse_rows = [], []
    for sl in range(n_sl):
        l = acc_ref[sl, LANES:LANES + 1, :]
        inv = 1.0 / l
        for j in range(nh):
            half = (j // gqa) % 2
            sc = inv[:, j * tq:(j + 1) * tq]
            if g_t is not None:
                sc = sc * g_t[sl * LANES + j:sl * LANES + j + 1, :]
            parts.append(acc_ref[sl, half * HEAD_DIM:(half + 1) * HEAD_DIM, j * tq:(j + 1) * tq] * sc)
        if lse_ref is not None:
            lse = m_ref[:, sl * nh * tq:(sl + 1) * nh * tq] + jnp.log(l)
            lse_rows += [jnp.broadcast_to(lse[:, j * tq:(j + 1) * tq], (HEAD_DIM, tq)) for j in range(nh)]
    o_ref[...] = jnp.concatenate(parts, axis=0).T
    if lse_ref is not None:
        lse_ref[...] = jnp.concatenate(lse_rows, axis=0).T


def _slab_queries(q_ref, n_sl, nh, gqa, tq):
    qs = nh * HEAD_DIM
    return [_q_transposed(q_ref[:, sl * qs:(sl + 1) * qs], nh, gqa, tq) for sl in range(n_sl)]


def _band_kernel(*refs, n_sl, nh, gqa, tq, tk, kw, want_lse, gated):
    q_ref, k_ref, vt_ref = refs[:3]
    pos = 3
    g_ref = None
    if gated:
        g_ref = refs[pos]
        pos += 1
    o_ref = refs[pos]
    pos += 1
    lse_ref = None
    if want_lse:
        lse_ref = refs[pos]
        pos += 1
    m_ref, acc_ref = refs[pos:pos + 2]

    i = pl.program_id(2)
    q0 = i * tq
    qts = _slab_queries(q_ref, n_sl, nh, gqa, tq)
    _init_stats(m_ref, acc_ref)
    rel = lax.broadcasted_iota(jnp.int32, (tk, tq), 1) - lax.broadcasted_iota(jnp.int32, (tk, tq), 0)

    def body(jj, carry):
        ks = pl.multiple_of(jj * tk, tk)
        kt = k_ref[pl.ds(ks, tk), :].astype(BF16)
        vt_t = vt_ref[:, pl.ds(ks, tk)].astype(BF16)
        dist = rel + (q0 - ks)
        _attend_tile_t(kt, vt_t, qts, [(dist >= 0) & (dist <= kw)] * (n_sl * nh), tq, m_ref, acc_ref)
        return carry

    lo_t = jnp.maximum(q0 - kw, 0) // tk
    hi_t = (q0 + tq - 1) // tk
    lax.fori_loop(lo_t, hi_t + 1, body, 0)
    _finish_t(o_ref, lse_ref, g_ref, m_ref, acc_ref, nh, gqa, tq)


def _band_attention(u, vt, *, q_off, k_off, nh, gqa, n_slabs, kw, dil, want_lse,
                    gates=None, g_off=0, tq=256, tk=256):
    B, S, W = u.shape
    n = S // dil
    tq = min(tq, n)
    tk = min(tk, n)
    uv = u.reshape(B, n, dil * W)
    wo = n_slabs * nh * HEAD_DIM
    kwid = n_slabs * LANES
    assert q_off % wo == 0 and k_off % kwid == 0 and (dil == 1 or (W % wo == 0 and W % kwid == 0))
    in_specs = [
        pl.BlockSpec((None, tq, wo), lambda b, r, i: (b, i, r * (W // wo) + q_off // wo)),
        pl.BlockSpec((None, n, kwid), lambda b, r, i: (b, 0, r * (W // kwid) + k_off // kwid)),
        pl.BlockSpec((None, None, kwid, n), lambda b, r, i: (b, r, 0, 0)),
    ]
    args = [uv, uv, vt]
    if gates is not None:
        Wg = gates.shape[-1]
        assert g_off % kwid == 0 and (dil == 1 or Wg % kwid == 0)
        in_specs.append(pl.BlockSpec((None, tq, kwid), lambda b, r, i: (b, i, r * (Wg // kwid) + g_off // kwid)))
        args.append(gates.reshape(B, n, dil * Wg))
    o_spec = pl.BlockSpec((None, tq, wo), lambda b, r, i: (b, i, r))
    o_shape = jax.ShapeDtypeStruct((B, n, dil * wo), F32)
    R = nh * tq
    res = pl.pallas_call(
        functools.partial(_band_kernel, n_sl=n_slabs, nh=nh, gqa=gqa, tq=tq, tk=tk, kw=kw, want_lse=want_lse,
                          gated=gates is not None),
        grid=(B, dil, n // tq),
        in_specs=in_specs,
        out_specs=[o_spec, o_spec] if want_lse else o_spec,
        out_shape=[o_shape, o_shape] if want_lse else o_shape,
        scratch_shapes=[pltpu.VMEM((1, n_slabs * R), F32), pltpu.VMEM((n_slabs, ACC_ROWS, R), F32)],
        compiler_params=_cparams(("parallel", "parallel", "arbitrary")),
    )(*args)
    if want_lse:
        return res[0].reshape(B, S, wo), res[1].reshape(B, S, wo)
    return res.reshape(B, S, wo)


def _values_transposed(u, v_off, width, dil):
    B, S, _ = u.shape
    v = u[:, :, v_off:v_off + width].reshape(B, S // dil, dil, width)
    return jnp.transpose(v, (0, 2, 3, 1))


def _compress_math(x_ref, w1_ref, pe_ref, w1f_ref, w2_ref, bd_ref, kg_ref, is_key, n_chunk):
    hid0 = _dot(pe_ref[...], w1f_ref[...])[0:1]
    acc = jnp.zeros((n_chunk, 4 * CMP_HIDDEN), F32)
    for j in range(CMP_STRIDE):
        xj = x_ref[pl.ds(j, n_chunk, stride=CMP_STRIDE), :].astype(BF16)
        acc = acc + _dot(xj, w1_ref[j])
    out = jnp.zeros((n_chunk, LANES), F32)
    for a in range(2):
        p0 = acc[:, (2 * a) * CMP_HIDDEN:(2 * a + 1) * CMP_HIDDEN]
        p1 = acc[:, (2 * a + 1) * CMP_HIDDEN:(2 * a + 2) * CMP_HIDDEN]
        hid = hid0 + p0 + pltpu.roll(p1, n_chunk - 1, axis=0)
        out = out + _dot(jax.nn.silu(hid).astype(BF16), w2_ref[a])
    normed = _head_norm(out, bd_ref, kg_ref[...])
    return jnp.where(is_key, normed, out)


def _compress_prompt_kernel(x_ref, w1_ref, pe_ref, w1f_ref, w2_ref, bd_ref, kg_ref, o_ref, ot_ref, *, n_chunk):
    is_key = pl.program_id(1) == 0
    out = _compress_math(x_ref, w1_ref, pe_ref, w1f_ref, w2_ref, bd_ref, kg_ref, is_key, n_chunk)
    o_ref[...] = out
    ot_ref[...] = out.T


def _compress_weights(pe, w1, w2, k_gain):
    w1r = w1.reshape(2, CMP_RATIO, CMP_STRIDE, HEAD_DIM, CMP_HIDDEN)
    wj = jnp.concatenate([w1r[:, 0], w1r[:, 1]], axis=-1)
    z = jnp.zeros_like(wj)
    w1bd = jnp.concatenate([jnp.concatenate([wj, z], axis=-1),
                            jnp.concatenate([z, wj], axis=-1)], axis=-2).astype(BF16)
    pe8 = jnp.concatenate([pe.reshape(2, 1, CMP_LEN * HEAD_DIM),
                           jnp.zeros((2, 7, CMP_LEN * HEAD_DIM), F32)], axis=1).astype(BF16)
    w1f = w1.reshape(2, CMP_LEN * HEAD_DIM, CMP_HIDDEN).astype(BF16)
    z2 = jnp.zeros_like(w2)
    w2pad = jnp.stack([jnp.concatenate([w2, z2], axis=-1),
                       jnp.concatenate([z2, w2], axis=-1)], axis=1).astype(BF16)
    hid = jnp.arange(LANES) // HEAD_DIM
    bd = ((hid[:, None] == hid[None, :]).astype(F32) / HEAD_DIM).astype(BF16)
    kg = jnp.tile(k_gain.astype(F32), 2)[None]
    return w1bd, pe8, w1f, w2pad, bd, kg


def _cw_specs(nd):
    def sp(shape, fn):
        return pl.BlockSpec(shape, fn)
    if nd == 3:
        return [
            sp((None, CMP_STRIDE, LANES, 4 * CMP_HIDDEN), lambda b, t, p: (t, 0, 0, 0)),
            sp((None, 8, CMP_LEN * HEAD_DIM), lambda b, t, p: (t, 0, 0)),
            sp((None, CMP_LEN * HEAD_DIM, CMP_HIDDEN), lambda b, t, p: (t, 0, 0)),
            sp((None, 2, CMP_HIDDEN, LANES), lambda b, t, p: (t, 0, 0, 0)),
            sp((LANES, LANES), lambda b, t, p: (0, 0)),
            sp((1, LANES), lambda b, t, p: (0, 0)),
        ]
    return [
        sp((None, CMP_STRIDE, LANES, 4 * CMP_HIDDEN), lambda b, t, p, *_: (t, 0, 0, 0)),
        sp((None, 8, CMP_LEN * HEAD_DIM), lambda b, t, p, *_: (t, 0, 0)),
        sp((None, CMP_LEN * HEAD_DIM, CMP_HIDDEN), lambda b, t, p, *_: (t, 0, 0)),
        sp((None, 2, CMP_HIDDEN, LANES), lambda b, t, p, *_: (t, 0, 0, 0)),
        sp((LANES, LANES), lambda b, t, p, *_: (0, 0)),
        sp((1, LANES), lambda b, t, p, *_: (0, 0)),
    ]


def _compress_prompt(u, col0, cw):
    B, S, W = u.shape
    n_chunk = S // CMP_STRIDE
    base = col0 // LANES
    return pl.pallas_call(
        functools.partial(_compress_prompt_kernel, n_chunk=n_chunk),
        grid=(B, 2, 2),
        in_specs=[pl.BlockSpec((None, S, LANES), lambda b, t, p: (b, 0, base + 2 * t + p))] + _cw_specs(3),
        out_specs=[pl.BlockSpec((None, None, n_chunk, LANES), lambda b, t, p: (b, t, 0, p)),
                   pl.BlockSpec((None, None, LANES, n_chunk), lambda b, t, p: (b, t, p, 0))],
        out_shape=[jax.ShapeDtypeStruct((B, 2, n_chunk, KVW_B), F32),
                   jax.ShapeDtypeStruct((B, 2, KVW_B, n_chunk), F32)],
        compiler_params=_cparams(("parallel", "parallel", "parallel")),
    )(u, *cw)


def _compress_sample_kernel(pt_ref, pool_ref, w1_ref, pe_ref, w1f_ref, w2_ref, bd_ref, kg_ref, o_ref,
                            raw, xbuf, sem, *, n_pages, n_chunk):
    n, t, p = pl.program_id(0), pl.program_id(1), pl.program_id(2)
    step = (n * 2 + t) * 2 + p
    n_steps = pl.num_programs(0) * 4

    def page_copy(page, rt, pair, slot, pg):
        row0 = pl.multiple_of(pair * LANES, LANES)
        return pltpu.make_async_copy(pool_ref.at[page, rt, pl.ds(row0, LANES), :], raw.at[slot, pg], sem.at[slot])

    def issue(st, slot):
        seq, rt, pair = st // 4, (st // 2) % 2, st % 2

        def go(pg, c):
            page_copy(pt_ref[seq * n_pages + pg], rt, pair, slot, pg).start()
            return c
        lax.fori_loop(0, n_pages, go, 0, unroll=8)

    @pl.when(step == 0)
    def _():
        issue(step, 0)

    @pl.when(step + 1 < n_steps)
    def _():
        issue(step + 1, (step + 1) % 2)

    slot = step % 2

    def wait(pg, c):
        page_copy(0, t, p, slot, pg).wait()
        return c
    lax.fori_loop(0, n_pages, wait, 0, unroll=8)

    def to_token_major(pg, c):
        xbuf[pl.ds(pl.multiple_of(pg * PAGE_SIZE, PAGE_SIZE), PAGE_SIZE), :] = raw[slot, pg].T
        return c
    lax.fori_loop(0, n_pages, to_token_major, 0, unroll=8)

    o_ref[...] = _compress_math(xbuf, w1_ref, pe_ref, w1f_ref, w2_ref, bd_ref, kg_ref, t == 0, n_chunk)


def _compress_sample(pool, page_table, cw):
    N, n_pages = page_table.shape
    n_chunk = n_pages * PAGE_SIZE // CMP_STRIDE
    gs = pltpu.PrefetchScalarGridSpec(
        num_scalar_prefetch=1,
        grid=(N, 2, 2),
        in_specs=[pl.BlockSpec(memory_space=pl.ANY)] + _cw_specs(4),
        out_specs=pl.BlockSpec((None, None, n_chunk, LANES), lambda b, t, p, *_: (b, t, 0, p)),
        scratch_shapes=[pltpu.VMEM((2, n_pages, LANES, PAGE_SIZE), F32),
                        pltpu.VMEM((n_pages * PAGE_SIZE, LANES), F32), pltpu.SemaphoreType.DMA((2,))],
    )
    return pl.pallas_call(
        functools.partial(_compress_sample_kernel, n_pages=n_pages, n_chunk=n_chunk),
        grid_spec=gs,
        out_shape=jax.ShapeDtypeStruct((N, 2, n_chunk, KVW_B), F32),
        compiler_params=_cparams(("arbitrary", "arbitrary", "arbitrary")),
    )(page_table.reshape(-1), pool, *cw)


def _cmp_kernel(q_ref, kc_ref, vc_ref, cov_ref, g_ref, o_ref, sel_ref, idx_ref, *,
                tq, n_chunk, nselp, pos_base, top_n):
    nh, gqa = 2 * GQA_B, GQA_B
    i = pl.program_id(2)
    R = nh * tq
    Q = _prep_q(q_ref[...], nh, gqa, tq)
    kc = kc_ref[...].astype(BF16)
    vc = vc_ref[...].astype(BF16)
    s = _nt_dot(Q, kc)
    pos_r = pos_base + i * tq + (lax.broadcasted_iota(jnp.int32, (R, n_chunk), 0) & (tq - 1))
    cend = lax.broadcasted_iota(jnp.int32, (R, n_chunk), 1) * CMP_STRIDE + (CMP_LEN - 1)
    s = jnp.where(cend <= pos_r, s, -jnp.inf)
    m = jnp.max(s, axis=-1, keepdims=True)
    m = jnp.where(m > -jnp.inf, m, 0.0)
    e = jnp.exp(s - m)
    den = jnp.sum(e, axis=-1, keepdims=True)
    p = e / jnp.maximum(den, TINY)
    o = _dot(p.astype(BF16), vc)
    o = _apply_gates(o, g_ref, nh, tq)
    o_ref[...] = _unprep_o(o, nh, gqa, tq)

    imps = []
    for a in range(2):
        ps = p[(a * gqa) * tq:(a * gqa + 1) * tq]
        for g in range(1, gqa):
            ps = ps + p[(a * gqa + g) * tq:(a * gqa + g + 1) * tq]
        hi = ps.astype(BF16)
        lo = (ps - hi.astype(F32)).astype(BF16)
        imps.append(_nt_dot(cov_ref[...], hi) + _nt_dot(cov_ref[...], lo))
    _select_blocks(imps, sel_ref, idx_ref, pos_base + i * tq, tq, nselp, top_n)


def _select_blocks(imps, sel_ref, idx_ref, pos0, tq, nselp, top_n):
    blk = lax.broadcasted_iota(jnp.int32, (nselp, tq), 0)
    cur = (pos0 + lax.broadcasted_iota(jnp.int32, (1, tq), 1)) // SEL_BLOCK
    valid = blk <= cur
    forced = (blk == 0) | (blk == cur) | (blk == cur - 1)
    n_forced = jnp.sum(forced.astype(jnp.int32), axis=0, keepdims=True)
    zero_row = jnp.zeros((1, tq), jnp.int32)
    for a, imp in enumerate(imps):
        rem = jnp.where(valid & jnp.logical_not(forced), imp, -1.0)
        sel = forced
        idx_ref[a, pl.ds(0, 1), :] = zero_row
        idx_ref[a, pl.ds(1, 1), :] = cur
        idx_ref[a, pl.ds(2, 1), :] = jnp.maximum(cur - 1, 0)
        for it in range(top_n - 1):
            mx = jnp.max(rem, axis=0, keepdims=True)
            first = jnp.min(jnp.where(rem == mx, blk, nselp), axis=0, keepdims=True)
            active = (it < top_n - n_forced) & (mx >= 0.0)
            pick = (blk == first) & active
            sel = sel | pick
            rem = jnp.where(pick, -1.0, rem)
            idx_ref[a, pl.ds(3 + it, 1), :] = jnp.where(active, first, 0)
        for r in range(3 + top_n - 1, IDX_ROWS):
            idx_ref[a, pl.ds(r, 1), :] = zero_row
        sel_ref[a] = sel.astype(F32)


def _cmp_t_kernel(q_ref, kc_ref, vct_ref, cov_ref, g_ref, o_ref, sel_ref, idx_ref, *,
                  tq, n_chunk, nselp, pos_base, top_n):
    nh, gqa = 2 * GQA_B, GQA_B
    i = pl.program_id(2)
    qt = _q_transposed(q_ref[...], nh, gqa, tq)
    s = _dot(kc_ref[...].astype(BF16), qt)
    pos_l = pos_base + i * tq + lax.broadcasted_iota(jnp.int32, (n_chunk, tq), 1)
    cend = lax.broadcasted_iota(jnp.int32, (n_chunk, tq), 0) * CMP_STRIDE + (CMP_LEN - 1)
    ok = cend <= pos_l
    s = jnp.concatenate([jnp.where(ok, s[:, j * tq:(j + 1) * tq], -jnp.inf) for j in range(nh)], axis=1)
    m = jnp.max(s, axis=0, keepdims=True)
    m = jnp.where(m > -jnp.inf, m, 0.0)
    e = jnp.exp(s - m)
    den = jnp.sum(e, axis=0, keepdims=True)
    p = e / jnp.maximum(den, TINY)
    acc = _dot(vct_ref[...].astype(BF16), p.astype(BF16))
    g_t = g_ref[...].T
    parts = []
    for j in range(nh):
        half = (j // gqa) % 2
        parts.append(acc[half * HEAD_DIM:(half + 1) * HEAD_DIM, j * tq:(j + 1) * tq] * g_t[j:j + 1, :])
    o_ref[...] = jnp.concatenate(parts, axis=0).T

    imps = []
    for a in range(2):
        ps = p[:, (a * gqa) * tq:(a * gqa + 1) * tq]
        for g in range(1, gqa):
            ps = ps + p[:, (a * gqa + g) * tq:(a * gqa + g + 1) * tq]
        hi = ps.astype(BF16)
        lo = (ps - hi.astype(F32)).astype(BF16)
        imps.append(_dot(cov_ref[...], hi) + _dot(cov_ref[...], lo))
    _select_blocks(imps, sel_ref, idx_ref, pos_base + i * tq, tq, nselp, top_n)


def _cover_t(n_chunk, n_cmp, n_sel, nselp):
    c = jnp.arange(n_chunk)[None, :]
    j = jnp.arange(nselp)[:, None]
    cov = ((c * CMP_STRIDE <= j * SEL_BLOCK + SEL_BLOCK - 1) & (c * CMP_STRIDE + CMP_LEN - 1 >= j * SEL_BLOCK)
           & (c < n_cmp) & (j < n_sel))
    return cov.astype(BF16)


def _cmp_attention(q, q_off, kvc, gates, g_off, *, tq, pos_base, n_keys, kvc_t=None):
    N, Tq, W = q.shape
    n_chunk = kvc.shape[2]
    n_cmp = (n_keys - CMP_LEN) // CMP_STRIDE + 1
    n_sel = -(-n_keys // SEL_BLOCK)
    nselp = -(-n_sel // 8) * 8
    top_n = min(SEL_TOPN, n_sel)
    qs = 2 * GQA_B * HEAD_DIM
    Wg = gates.shape[-1]
    cov = _cover_t(n_chunk, n_cmp, n_sel, nselp)
    if kvc_t is None:
        body, values = _cmp_kernel, kvc
        v_spec = pl.BlockSpec((None, None, n_chunk, LANES), lambda b, p, i: (b, 1, 0, p))
    else:
        body, values = _cmp_t_kernel, kvc_t
        v_spec = pl.BlockSpec((None, None, LANES, n_chunk), lambda b, p, i: (b, 1, p, 0))
    return pl.pallas_call(
        functools.partial(body, tq=tq, n_chunk=n_chunk, nselp=nselp, pos_base=pos_base, top_n=top_n),
        grid=(N, 2, Tq // tq),
        in_specs=[
            pl.BlockSpec((None, tq, qs), lambda b, p, i: (b, i, q_off // qs + p)),
            pl.BlockSpec((None, None, n_chunk, LANES), lambda b, p, i: (b, 0, 0, p)),
            v_spec,
            pl.BlockSpec((nselp, n_chunk), lambda b, p, i: (0, 0)),
            pl.BlockSpec((None, tq, LANES), lambda b, p, i: (b, i, g_off // LANES + p)),
        ],
        out_specs=[
            pl.BlockSpec((None, tq, qs), lambda b, p, i: (b, i, p)),
            pl.BlockSpec((None, 2, nselp, tq), lambda b, p, i: (b, p, 0, i)),
            pl.BlockSpec((None, 2, IDX_ROWS, tq), lambda b, p, i: (b, p, 0, i)),
        ],
        out_shape=[
            jax.ShapeDtypeStruct((N, Tq, WIDTH_B), F32),
            jax.ShapeDtypeStruct((N, KV_HEADS_B, nselp, Tq), F32),
            jax.ShapeDtypeStruct((N, KV_HEADS_B, IDX_ROWS, Tq), jnp.int32),
        ],
        compiler_params=_cparams(("parallel", "parallel", "parallel")),
    )(q, kvc, values, cov, gates)


def _sel_prompt_kernel(q_ref, k_ref, vt_ref, sel_ref, g_ref, o_ref, m_ref, acc_ref, *, tq, tk):
    nh, gqa, n_sl = 2 * GQA_B, GQA_B, KV_HEADS_B // 2
    i = pl.program_id(1)
    q0 = i * tq
    qts = _slab_queries(q_ref, n_sl, nh, gqa, tq)
    _init_stats(m_ref, acc_ref)
    nblk = tk // SEL_BLOCK
    rel = lax.broadcasted_iota(jnp.int32, (tk, tq), 1) - lax.broadcasted_iota(jnp.int32, (tk, tq), 0)

    def body(jj, carry):
        ks = pl.multiple_of(jj * tk, tk)
        kt = k_ref[pl.ds(ks, tk), :].astype(BF16)
        vt_t = vt_ref[:, pl.ds(ks, tk)].astype(BF16)
        causal = rel + (q0 - ks) >= 0
        masks = []
        for kvh in range(KV_HEADS_B):
            rows = sel_ref[kvh, pl.ds(pl.multiple_of(jj * nblk, nblk), nblk), :]
            chosen = jnp.concatenate(
                [jnp.broadcast_to(rows[c:c + 1], (SEL_BLOCK, tq)) for c in range(nblk)], axis=0)
            masks += [(chosen > 0.5) & causal] * gqa
        _attend_tile_t(kt, vt_t, qts, masks, tq, m_ref, acc_ref)
        return carry

    lax.fori_loop(0, (q0 + tq - 1) // tk + 1, body, 0)
    _finish_t(o_ref, None, g_ref, m_ref, acc_ref, nh, gqa, tq)


def _sel_prompt(u, vt, sel_t, *, q_off, k_off, g_off, tq=128, tk=512):
    B, S, W = u.shape
    n_sl = KV_HEADS_B // 2
    nselp = sel_t.shape[2]
    tq = min(tq, S)
    tk = min(tk, S)
    assert tk % SEL_BLOCK == 0 and (tk // SEL_BLOCK) % 8 == 0 and S % tk == 0 and nselp * SEL_BLOCK >= S
    assert q_off % WIDTH_B == 0 and k_off % KVW_B == 0 and g_off % KVW_B == 0
    R = 2 * GQA_B * tq
    return pl.pallas_call(
        functools.partial(_sel_prompt_kernel, tq=tq, tk=tk),
        grid=(B, S // tq),
        in_specs=[
            pl.BlockSpec((None, tq, WIDTH_B), lambda b, i: (b, i, q_off // WIDTH_B)),
            pl.BlockSpec((None, S, KVW_B), lambda b, i: (b, 0, k_off // KVW_B)),
            pl.BlockSpec((None, None, KVW_B, S), lambda b, i: (b, 0, 0, 0)),
            pl.BlockSpec((None, KV_HEADS_B, nselp, tq), lambda b, i: (b, 0, 0, i)),
            pl.BlockSpec((None, tq, KVW_B), lambda b, i: (b, i, g_off // KVW_B)),
        ],
        out_specs=pl.BlockSpec((None, tq, WIDTH_B), lambda b, i: (b, i, 0)),
        out_shape=jax.ShapeDtypeStruct((B, S, WIDTH_B), F32),
        scratch_shapes=[pltpu.VMEM((1, n_sl * R), F32), pltpu.VMEM((n_sl, ACC_ROWS, R), F32)],
        compiler_params=_cparams(("parallel", "arbitrary")),
    )(u, u, vt, sel_t, u)


def _sel_sample_kernel(pt_ref, idx_ref, q_ref, kn_ref, vn_ref, g_ref, pool_ref, o_ref, kvbuf, sem, *,
                       nt, n_pages, n_pick):
    nh, gqa, tq = 2 * GQA_B, GQA_B, SAMPLE_ROWS
    n, p, a = pl.program_id(0), pl.program_id(1), pl.program_id(2)
    step = (n * 2 + p) * 2 + a
    n_steps = pl.num_programs(0) * 4
    blocks_per_page = PAGE_SIZE // SEL_BLOCK
    width = n_pick * PAGE_SIZE

    def picked_block(seq, pair, head, t, r):
        src_row = jnp.where(r == 0, 0, r + 1)
        return idx_ref[((seq * KV_HEADS_B + pair * 2 + head) * IDX_ROWS + src_row) * tq + t]

    def page_copy(st, c, lookup):
        seq, pair, head = st // 4, (st // 2) % 2, st % 2
        t, r = c // n_pick, c % n_pick
        page = pt_ref[seq * n_pages + picked_block(seq, pair, head, t, r) // blocks_per_page] if lookup else 0
        dst0 = pl.multiple_of(r * PAGE_SIZE, PAGE_SIZE)
        half0 = pl.multiple_of(head * HEAD_DIM, HEAD_DIM)
        row0 = pl.multiple_of(pair * LANES + head * HEAD_DIM, HEAD_DIM)
        return pltpu.make_async_copy(
            pool_ref.at[page, pl.ds(2, 2), pl.ds(row0, HEAD_DIM), :],
            kvbuf.at[head, t, :, pl.ds(half0, HEAD_DIM), pl.ds(dst0, PAGE_SIZE)], sem.at[head])

    def issue(st):
        def go(c, carry):
            page_copy(st, c, True).start()
            return carry
        lax.fori_loop(0, nt * n_pick, go, 0, unroll=4)

    @pl.when(step == 0)
    def _():
        kvbuf[0, :, :, HEAD_DIM:, :] = jnp.zeros((nt, 2, HEAD_DIM, width), F32)
        kvbuf[1, :, :, :HEAD_DIM, :] = jnp.zeros((nt, 2, HEAD_DIM, width), F32)
        issue(step)

    @pl.when(step + 1 < n_steps)
    def _():
        issue(step + 1)

    slot = a

    def wait(c, carry):
        page_copy(step, c, False).wait()
        return carry
    lax.fori_loop(0, nt * n_pick, wait, 0, unroll=4)

    R = nh * tq
    Q = _prep_q(q_ref[...], nh, gqa, tq)
    pad = jnp.zeros((LANES - tq, LANES), F32)
    kn = jnp.concatenate([kn_ref[...], pad], axis=0).astype(BF16)
    vn = jnp.concatenate([vn_ref[...], pad], axis=0).astype(BF16)
    row = lax.broadcasted_iota(jnp.int32, (R, LANES), 0)
    colk = lax.broadcasted_iota(jnp.int32, (R, LANES), 1)
    trow = row & (tq - 1)
    s_new = jnp.where((colk <= trow) & (colk < nt), _nt_dot(Q, kn), NEG)
    lane = lax.broadcasted_iota(jnp.int32, (1, width), 1)
    lane_pick = lane // PAGE_SIZE
    lane_half = (lane // SEL_BLOCK) % blocks_per_page
    o = jnp.zeros((R, LANES), F32)
    for t in range(nt):
        chosen = lane < 0
        for r in range(n_pick):
            half = picked_block(n, p, a, t, r) % blocks_per_page
            chosen = chosen | ((lane_pick == r) & (lane_half == half))
        s = jnp.where(chosen, _dot(Q, kvbuf[slot, t, 0].astype(BF16)), NEG)
        m = jnp.maximum(jnp.max(s, axis=-1, keepdims=True), jnp.max(s_new, axis=-1, keepdims=True))
        p1 = jnp.exp(s - m)
        p2 = jnp.exp(s_new - m)
        l = jnp.sum(p1, axis=-1, keepdims=True) + jnp.sum(p2, axis=-1, keepdims=True)
        o_t = (_nt_dot(p1.astype(BF16), kvbuf[slot, t, 1].astype(BF16)) + _dot(p2.astype(BF16), vn)) / l
        o = jnp.where(trow == t, o_t, o)

    g = g_ref[...]
    low = lax.broadcasted_iota(jnp.int32, (tq, LANES), 1) < HEAD_DIM
    first = a == 0
    placed = []
    for jj in range(gqa):
        oj = jnp.where(first, o[jj * tq:(jj + 1) * tq], o[(gqa + jj) * tq:(gqa + jj + 1) * tq])
        gj = jnp.where(first, g[:, jj:jj + 1], g[:, gqa + jj:gqa + jj + 1])
        oj = oj * gj
        placed.append(jnp.where(a == jj % 2, oj, pltpu.roll(oj, HEAD_DIM, axis=1)))
    o_ref[...] = jnp.concatenate([jnp.where(low, placed[0], placed[1]),
                                  jnp.where(low, placed[2], placed[3])], axis=1)


def _sel_sample(us, idx, pool, page_table, *, nt, q_off, k_off, v_off, g_off):
    N, tq, W = us.shape
    n_pages = page_table.shape[1]
    n_pick = SEL_TOPN - 1
    qs = 2 * GQA_B * HEAD_DIM
    hw = GQA_B * HEAD_DIM
    gs = pltpu.PrefetchScalarGridSpec(
        num_scalar_prefetch=2,
        grid=(N, 2, 2),
        in_specs=[
            pl.BlockSpec((None, tq, qs), lambda b, p, a, *_: (b, 0, q_off // qs + p)),
            pl.BlockSpec((None, tq, LANES), lambda b, p, a, *_: (b, 0, k_off // LANES + p)),
            pl.BlockSpec((None, tq, LANES), lambda b, p, a, *_: (b, 0, v_off // LANES + p)),
            pl.BlockSpec((None, tq, LANES), lambda b, p, a, *_: (b, 0, g_off // LANES + p)),
            pl.BlockSpec(memory_space=pl.ANY),
        ],
        out_specs=pl.BlockSpec((None, tq, hw), lambda b, p, a, *_: (b, 0, 2 * p + a)),
        scratch_shapes=[
            pltpu.VMEM((2, nt, 2, LANES, n_pick * PAGE_SIZE), F32),
            pltpu.SemaphoreType.DMA((2,)),
        ],
    )
    return pl.pallas_call(
        functools.partial(_sel_sample_kernel, nt=nt, n_pages=n_pages, n_pick=n_pick),
        grid_spec=gs,
        out_shape=jax.ShapeDtypeStruct((N, tq, WIDTH_B), F32),
        compiler_params=_cparams(("arbitrary", "arbitrary", "arbitrary")),
    )(page_table.reshape(-1), idx.reshape(-1), us, us, us, us, pool)


def _cache_attn_kernel(*refs, n_sl, nh, gqa, L, dil, win, nt, want_lse, gated):
    q_ref, kc_ref, vc_ref, kn_ref, vn_ref = refs[:5]
    pos = 5
    g_ref = None
    if gated:
        g_ref = refs[pos]
        pos += 1
    o_ref = refs[pos]
    lse_ref = refs[pos + 1] if want_lse else None
    tq = SAMPLE_ROWS
    R = nh * tq
    qs = nh * HEAD_DIM
    pad = jnp.zeros((LANES - tq, LANES), F32)
    t1 = lax.broadcasted_iota(jnp.int32, (R, L), 0) & (tq - 1)
    d1 = L + t1 - lax.broadcasted_iota(jnp.int32, (R, L), 1)
    ok1 = ((d1 & (dil - 1)) == 0) & (d1 <= win)
    t2 = lax.broadcasted_iota(jnp.int32, (R, LANES), 0) & (tq - 1)
    c2 = lax.broadcasted_iota(jnp.int32, (R, LANES), 1)
    d2 = t2 - c2
    ok2 = (d2 >= 0) & ((d2 & (dil - 1)) == 0) & (d2 <= win) & (c2 < nt)
    for sl in range(n_sl):
        rows = slice(sl * LANES, (sl + 1) * LANES)
        Q = _prep_q(q_ref[:, sl * qs:(sl + 1) * qs], nh, gqa, tq)
        kc_t = kc_ref[rows, :].astype(BF16)
        vc_t = vc_ref[rows, :].astype(BF16)
        kn = jnp.concatenate([kn_ref[:, rows], pad], axis=0).astype(BF16)
        vn = jnp.concatenate([vn_ref[:, rows], pad], axis=0).astype(BF16)
        s1 = jnp.where(ok1, _dot(Q, kc_t), NEG)
        s2 = jnp.where(ok2, _nt_dot(Q, kn), NEG)
        m = jnp.maximum(jnp.max(s1, axis=-1, keepdims=True), jnp.max(s2, axis=-1, keepdims=True))
        p1 = jnp.exp(s1 - m)
        p2 = jnp.exp(s2 - m)
        l = jnp.sum(p1, axis=-1, keepdims=True) + jnp.sum(p2, axis=-1, keepdims=True)
        o = (_nt_dot(p1.astype(BF16), vc_t) + _dot(p2.astype(BF16), vn)) / l
        if g_ref is not None:
            g = g_ref[:, rows]
            o = jnp.concatenate([o[j * tq:(j + 1) * tq] * g[:, j:j + 1] for j in range(nh)], axis=0)
        o_ref[:, sl * qs:(sl + 1) * qs] = _unprep_o(o, nh, gqa, tq)
        if want_lse:
            lse_ref[:, sl * qs:(sl + 1) * qs] = _unprep_o(jnp.broadcast_to(m + jnp.log(l), (R, LANES)),
                                                          nh, gqa, tq)


def _feature_major(x):
    nd = x.ndim
    xt = jnp.transpose(x, (0, 1) + tuple(range(3, nd)) + (2,))
    return xt.reshape(x.shape[0], x.shape[1], -1, x.shape[2])


def _cache_attention(us, cache_t, layer, *, q_off, k_off, v_off, nh, gqa, n_slabs, dil, win, nt, want_lse,
                     g_off=None):
    N, tq, W = us.shape
    L = cache_t.shape[3]
    cache = cache_t
    wo = n_slabs * nh * HEAD_DIM
    kwid = n_slabs * LANES
    assert q_off % wo == 0 and k_off % kwid == 0 and v_off % kwid == 0
    in_specs = [
        pl.BlockSpec((None, tq, wo), lambda b: (b, 0, q_off // wo)),
        pl.BlockSpec((None, None, kwid, L), lambda b: (layer, b, 0, 0)),
        pl.BlockSpec((None, None, kwid, L), lambda b: (layer, b, 1, 0)),
        pl.BlockSpec((None, tq, kwid), lambda b: (b, 0, k_off // kwid)),
        pl.BlockSpec((None, tq, kwid), lambda b: (b, 0, v_off // kwid)),
    ]
    args = [us, cache, cache, us, us]
    if g_off is not None:
        assert g_off % kwid == 0
        in_specs.append(pl.BlockSpec((None, tq, kwid), lambda b: (b, 0, g_off // kwid)))
        args.append(us)
    o_spec = pl.BlockSpec((None, tq, wo), lambda b: (b, 0, 0))
    o_shape = jax.ShapeDtypeStruct((N, tq, wo), F32)
    return pl.pallas_call(
        functools.partial(_cache_attn_kernel, n_sl=n_slabs, nh=nh, gqa=gqa, L=L, dil=dil, win=win, nt=nt,
                          want_lse=want_lse, gated=g_off is not None),
        grid=(N,),
        in_specs=in_specs,
        out_specs=[o_spec, o_spec] if want_lse else o_spec,
        out_shape=[o_shape, o_shape] if want_lse else o_shape,
        compiler_params=_cparams(("parallel",)),
    )(*args)


def _merge_a_kernel(o0, o1, o2, l0, l1, l2, z_ref, x_ref, w_ref, y_ref):
    a0, a1, a2 = l0[...], l1[...], l2[...]
    mx = jnp.maximum(jnp.maximum(a0, a1), a2)
    e0, e1, e2 = jnp.exp(a0 - mx), jnp.exp(a1 - mx), jnp.exp(a2 - mx)
    den = e0 + e1 + e2
    o = (e0 / den) * o0[...] + (e1 / den) * o1[...] + (e2 / den) * o2[...]
    y_ref[...] = x_ref[...] + _dot((o * jax.nn.silu(z_ref[...])).astype(BF16), w_ref[...])


def _merge_a(outs, lses, u, z_off, x, w_out_bf16):
    T, D = x.shape
    tm = min(T, 512)
    wa = WIDTH_A
    row = pl.BlockSpec((tm, wa), lambda i: (i, 0))
    return pl.pallas_call(
        _merge_a_kernel,
        grid=(T // tm,),
        in_specs=[row] * 6 + [
            pl.BlockSpec((tm, wa), lambda i: (i, z_off // wa)),
            pl.BlockSpec((tm, D), lambda i: (i, 0)),
            pl.BlockSpec((wa, D), lambda i: (0, 0)),
        ],
        out_specs=pl.BlockSpec((tm, D), lambda i: (i, 0)),
        out_shape=jax.ShapeDtypeStruct((T, D), F32),
        compiler_params=_cparams(("parallel",)),
    )(*outs, *lses, u, x, w_out_bf16)


def _merge_b_kernel(oc, os_, ow, z_ref, x_ref, w_ref, y_ref):
    o = oc[...] + os_[...] + ow[...]
    y_ref[...] = x_ref[...] + _dot((o * jax.nn.silu(z_ref[...])).astype(BF16), w_ref[...])


def _merge_b(o_c, o_s, o_w, u, z_off, x, w_out_bf16):
    T, D = x.shape
    tm = min(T, 512)
    wb = WIDTH_B
    row = pl.BlockSpec((tm, wb), lambda i: (i, 0))
    return pl.pallas_call(
        _merge_b_kernel,
        grid=(T // tm,),
        in_specs=[row] * 3 + [
            pl.BlockSpec((tm, wb), lambda i: (i, z_off // wb)),
            pl.BlockSpec((tm, D), lambda i: (i, 0)),
            pl.BlockSpec((wb, D), lambda i: (0, 0)),
        ],
        out_specs=pl.BlockSpec((tm, D), lambda i: (i, 0)),
        out_shape=jax.ShapeDtypeStruct((T, D), F32),
        compiler_params=_cparams(("parallel",)),
    )(o_c, o_s, o_w, u, x, w_out_bf16)


A_KINDS = ([EP_ROPE_Q] * (N_GROUPS_A * WIDTH_A // PROJ_TN) + [EP_ROPE_K] * (N_GROUPS_A * WIDTH_A // PROJ_TN)
           + [EP_NONE] * (N_GROUPS_A * WIDTH_A // PROJ_TN) + [EP_NONE] * (WIDTH_A // PROJ_TN))
A_K, A_V, A_Z = N_GROUPS_A * WIDTH_A, 2 * N_GROUPS_A * WIDTH_A, QKV_A

B_KINDS = ([EP_NORM_Q] * 4 + [EP_ROPE_Q] * 4 + [EP_NONE] * 4
           + [EP_NORM_K, EP_NONE, EP_ROPE_K, EP_NONE, EP_ROPE_K, EP_NONE] + [EP_SIG] * 3)


def _b_weight(w_in):
    wq = w_in[:, :WIDTH_B]
    o1 = WIDTH_B + 6 * KVW_B
    wkv = w_in[:, WIDTH_B:o1]
    wg = w_in[:, o1:o1 + 3 * HEADS_B]
    wz = w_in[:, o1 + 3 * HEADS_B:]
    nh = 2 * GQA_B
    tiles = []
    for b in range(3):
        for p in range(2):
            g = wg[:, b * HEADS_B + p * nh: b * HEADS_B + (p + 1) * nh]
            tiles.append(jnp.pad(g, ((0, 0), (0, LANES - nh))))
    return jnp.concatenate([wq, wq, wz, wkv] + tiles, axis=1).astype(BF16)


def _pad_rows(u, N, T):
    return jnp.pad(u.reshape(N, T, -1), ((0, 0), (0, SAMPLE_ROWS - T), (0, 0)))


def _layer_a_prompt(x, norm_g, w_bf16, q_gain, k_gain, w_out_bf16, tabs):
    B, S, D = x.shape
    u = _project(x.reshape(B * S, D), norm_g, w_bf16, A_KINDS, tabs[0], tabs[1], q_gain, k_gain)
    u3 = u.reshape(B, S, -1)
    outs, lses, states = [], [], []
    for g, (win, dil) in enumerate(DIL_PATTERNS):
        vt = _values_transposed(u3, A_V + g * WIDTH_A, WIDTH_A, dil)
        if dil == 1:
            qk, q_off, k_off = u3, g * WIDTH_A, A_K + g * WIDTH_A
        else:
            qk = jnp.concatenate([u3[:, :, g * WIDTH_A:(g + 1) * WIDTH_A],
                                  u3[:, :, A_K + g * WIDTH_A:A_K + (g + 1) * WIDTH_A]], axis=2)
            q_off, k_off = 0, WIDTH_A
        o, lse = _band_attention(qk, vt, q_off=q_off, k_off=k_off,
                                 nh=2, gqa=1, n_slabs=HEADS_A // 2, kw=win // dil, dil=dil, want_lse=True)
        outs.append(o.reshape(B * S, WIDTH_A))
        lses.append(lse.reshape(B * S, WIDTH_A))
        w = min(win, S)
        kg = u3[:, S - w:, A_K + g * WIDTH_A: A_K + (g + 1) * WIDTH_A]
        vg = u3[:, S - w:, A_V + g * WIDTH_A: A_V + (g + 1) * WIDTH_A]
        states.append(jnp.stack([kg, vg], axis=2).reshape(B, w, 2, HEADS_A, HEAD_DIM))
    y = _merge_a(outs, lses, u, A_Z, x.reshape(B * S, D), w_out_bf16)
    return y.reshape(B, S, D), states


def _layer_a_sample(x, caches_t, layer, norm_g, w_bf16, q_gain, k_gain, w_out_bf16, tabs):
    N, T, D = x.shape
    u = _project(x.reshape(N * T, D), norm_g, w_bf16, A_KINDS, tabs[0], tabs[1], q_gain, k_gain)
    us = _pad_rows(u, N, T)
    outs, lses, new_rows = [], [], []
    for g, (win, dil) in enumerate(DIL_PATTERNS):
        o, lse = _cache_attention(us, caches_t[g], layer, q_off=g * WIDTH_A, k_off=A_K + g * WIDTH_A,
                                  v_off=A_V + g * WIDTH_A, nh=2, gqa=1, n_slabs=HEADS_A // 2,
                                  dil=dil, win=win, nt=T, want_lse=True)
        outs.append(o[:, :T].reshape(N * T, WIDTH_A))
        lses.append(lse[:, :T].reshape(N * T, WIDTH_A))
        new_rows.append(jnp.concatenate([us[:, :, A_K + g * WIDTH_A: A_K + (g + 1) * WIDTH_A],
                                         us[:, :, A_V + g * WIDTH_A: A_V + (g + 1) * WIDTH_A]], axis=2))
    y = _merge_a(outs, lses, u, A_Z, x.reshape(N * T, D), w_out_bf16)
    return y.reshape(N, T, D), new_rows


def _layer_b_prompt(x, norm_g, w_bf16, q_gain, k_gain, cw, w_out_bf16, tabs):
    B, S, D = x.shape
    u = _project(x.reshape(B * S, D), norm_g, w_bf16, B_KINDS, tabs[0], tabs[1], q_gain, k_gain)
    u3 = u.reshape(B, S, -1)
    kvc, kvc_t = _compress_prompt(u3, B_KV, cw)
    o_c, sel_t, _ = _cmp_attention(u3, B_QC, kvc, u3, B_GATE, tq=min(256, S), pos_base=0, n_keys=S,
                                   kvc_t=kvc_t)
    o_s = _sel_prompt(u3, _values_transposed(u3, B_KV + 3 * KVW_B, KVW_B, 1), sel_t, q_off=B_QR,
                      k_off=B_KV + 2 * KVW_B, g_off=B_GATE + 2 * LANES)
    o_w = _band_attention(u3, _values_transposed(u3, B_WIN + KVW_B, KVW_B, 1), q_off=B_QR, k_off=B_WIN,
                          nh=2 * GQA_B, gqa=GQA_B, n_slabs=2, kw=WIN_B, dil=1, want_lse=False, gates=u3,
                          g_off=B_GATE + 4 * LANES, tq=128, tk=256)
    y = _merge_b(o_c.reshape(B * S, -1), o_s.reshape(B * S, -1), o_w.reshape(B * S, -1), u, B_Z,
                 x.reshape(B * S, D), w_out_bf16)
    rows = u3[:, :, B_KV:B_KV + 4 * KVW_B].reshape(B, S, 4, KV_HEADS_B, HEAD_DIM)
    w = min(WIN_B, S)
    wrows = u3[:, S - w:, B_WIN:B_WIN + 2 * KVW_B].reshape(B, w, 2, KV_HEADS_B, HEAD_DIM)
    return y.reshape(B, S, D), rows, wrows


def _layer_b_sample(x, pool_t, page_table, win_t, layer, norm_g, w_bf16, q_gain, k_gain, cw, w_out_bf16, tabs):
    N, T, D = x.shape
    past_len = page_table.shape[1] * PAGE_SIZE
    assert past_len % SEL_BLOCK == 0 and T <= SEL_BLOCK and T <= SAMPLE_ROWS
    u = _project(x.reshape(N * T, D), norm_g, w_bf16, B_KINDS, tabs[0], tabs[1], q_gain, k_gain)
    us = _pad_rows(u, N, T)
    u3 = u.reshape(N, T, -1)
    kvc = _compress_sample(pool_t, page_table, cw)
    o_c, _, idx = _cmp_attention(us, B_QC, kvc, us, B_GATE, tq=SAMPLE_ROWS, pos_base=past_len,
                                 n_keys=past_len + T)
    o_s = _sel_sample(us, idx, pool_t, page_table, nt=T, q_off=B_QR, k_off=B_KV + 2 * KVW_B,
                      v_off=B_KV + 3 * KVW_B, g_off=B_GATE + 2 * LANES)
    o_w = _cache_attention(us, win_t, layer, q_off=B_QR, k_off=B_WIN, v_off=B_WIN + KVW_B,
                           nh=2 * GQA_B, gqa=GQA_B, n_slabs=2, dil=1, win=WIN_B, nt=T, want_lse=False,
                           g_off=B_GATE + 4 * LANES)
    y = _merge_b(o_c[:, :T].reshape(N * T, -1), o_s[:, :T].reshape(N * T, -1), o_w[:, :T].reshape(N * T, -1),
                 u, B_Z, x.reshape(N * T, D), w_out_bf16)
    rows = u3[:, :, B_KV:B_KV + 4 * KVW_B].reshape(N, T, 4, KV_HEADS_B, HEAD_DIM)
    return y.reshape(N, T, D), rows, us[:, :, B_WIN:B_WIN + 2 * KVW_B]


def _append_kernel(c_ref, n_ref, o_ref, *, L, nt, fb):
    rolled = pltpu.roll(c_ref[...], L - nt, axis=1)
    pad = jnp.zeros((LANES - SAMPLE_ROWS, LANES), F32)
    new = n_ref[...]
    new_t = jnp.concatenate(
        [jnp.concatenate([new[:, c * LANES:(c + 1) * LANES], pad], axis=0).T for c in range(fb // LANES)], axis=0)
    new_t = pltpu.roll(new_t, LANES - nt, axis=1)
    lane = lax.broadcasted_iota(jnp.int32, (fb, LANES), 1)
    if L > LANES:
        o_ref[:, :L - LANES] = rolled[:, :L - LANES]
    o_ref[:, L - LANES:] = jnp.where(lane >= LANES - nt, new_t, rolled[:, L - LANES:])


def _append_cache(cache, cache_t, new_rows, nt, win):
    J, N, F, L = cache_t.shape
    tail = cache.shape[3:]
    if L + nt <= win or L % LANES:
        new = new_rows[:, :, :nt].reshape((J, N, nt) + tail)
        return jnp.concatenate([cache, new], axis=2)[:, :, -min(win, L + nt):]
    assert L == win
    fb = min(F, LANES * max(1, 2048 // L))
    assert F % fb == 0
    out_t = pl.pallas_call(
        functools.partial(_append_kernel, L=L, nt=nt, fb=fb),
        grid=(J, N, F // fb),
        in_specs=[pl.BlockSpec((None, None, fb, L), lambda j, n, f: (j, n, f, 0)),
                  pl.BlockSpec((None, None, SAMPLE_ROWS, fb), lambda j, n, f: (j, n, 0, f))],
        out_specs=pl.BlockSpec((None, None, fb, L), lambda j, n, f: (j, n, f, 0)),
        out_shape=jax.ShapeDtypeStruct((J, N, F, L), F32),
        compiler_params=_cparams(("parallel", "parallel", "parallel")),
    )(cache_t, new_rows)
    nd = len(tail)
    out = out_t.reshape((J, N) + tail + (L,))
    return jnp.transpose(out, (0, 1, nd + 2) + tuple(range(2, nd + 2)))


def kernel(x_prompt, x_sample, cache_dil_0, cache_dil_1, cache_dil_2, cache_nsa_paged, cache_nsa_win,
           page_table, a_norm, a_w_in, a_q_norm, a_k_norm, a_w_out, b_norm, b_w_in, b_q_norm, b_k_norm,
           b_cmp_pe, b_cmp_w1, b_cmp_w2, b_w_out):
    dil_caches = (cache_dil_0, cache_dil_1, cache_dil_2)
    B, S, _ = x_prompt.shape
    N, T, _ = x_sample.shape
    past_len = page_table.shape[1] * PAGE_SIZE
    depth = a_norm.shape[0] + b_norm.shape[0]
    n_pool = cache_nsa_paged.shape[1]
    tabs_p = _rope_tables(jnp.tile(jnp.arange(S), B))
    tabs_s = _rope_tables(jnp.tile(past_len + jnp.arange(T), N))
    dil_t = [_feature_major(c) for c in dil_caches]
    win_t = _feature_major(cache_nsa_win)
    pool_t = _feature_major(cache_nsa_paged)
    pool_t = pool_t.reshape(-1, 4, KVW_B, PAGE_SIZE)
    xp, xs = x_prompt, x_sample
    dil_p = [[] for _ in DIL_PATTERNS]
    dil_new = [[] for _ in DIL_PATTERNS]
    rows_p, rows_s, win_p, win_new = [], [], [], []
    for layer in range(depth):
        j = layer // 2
        if layer % 2 == 0:
            w = a_w_in[j].astype(BF16)
            wo = a_w_out[j].astype(BF16)
            xp, st_p = _layer_a_prompt(xp, a_norm[j], w, a_q_norm[j], a_k_norm[j], wo, tabs_p)
            xs, new_s = _layer_a_sample(xs, dil_t, j, a_norm[j], w, a_q_norm[j], a_k_norm[j], wo, tabs_s)
            for g in range(N_GROUPS_A):
                dil_p[g].append(st_p[g])
                dil_new[g].append(new_s[g])
        else:
            w = _b_weight(b_w_in[j])
            wo = b_w_out[j].astype(BF16)
            cw = _compress_weights(b_cmp_pe[j], b_cmp_w1[j], b_cmp_w2[j], b_k_norm[j])
            xp, rp, wp = _layer_b_prompt(xp, b_norm[j], w, b_q_norm[j], b_k_norm[j], cw, wo, tabs_p)
            xs, rs, wn = _layer_b_sample(xs, pool_t, page_table + j * n_pool, win_t, j, b_norm[j], w,
                                         b_q_norm[j], b_k_norm[j], cw, wo, tabs_s)
            rows_p.append(rp)
            win_p.append(wp)
            rows_s.append(rs)
            win_new.append(wn)
    dil_s = [_append_cache(dil_caches[g], dil_t[g], jnp.stack(dil_new[g]), T, DIL_PATTERNS[g][0])
             for g in range(N_GROUPS_A)]
    win_s = _append_cache(cache_nsa_win, win_t, jnp.stack(win_new), T, WIN_B)
    return (xp, xs,
            jnp.stack(dil_p[0]), jnp.stack(dil_p[1]), jnp.stack(dil_p[2]),
            jnp.stack(rows_p), jnp.stack(win_p),
            dil_s[0], dil_s[1], dil_s[2],
            jnp.stack(rows_s), win_s)
```

```python
import functools

import jax
import jax.numpy as jnp
from jax import lax
from jax.experimental import pallas as pl
from jax.experimental.pallas import tpu as pltpu

F32 = jnp.float32
BF16 = jnp.bfloat16

HEAD_DIM = 64
SCALE = HEAD_DIM ** -0.5
ROPE_THETA = 10000.0
EPS = 1e-6
TINY = 1e-30
NEG = -1e30
DIL_PATTERNS = ((128, 1), (512, 4), (2048, 16))
N_GROUPS_A = len(DIL_PATTERNS)
HEADS_A = 8
WIDTH_A = HEADS_A * HEAD_DIM
QKV_A = 3 * N_GROUPS_A * WIDTH_A
KV_HEADS_B = 4
GQA_B = 4
HEADS_B = KV_HEADS_B * GQA_B
WIDTH_B = HEADS_B * HEAD_DIM
KVW_B = KV_HEADS_B * HEAD_DIM
CMP_LEN = 32
CMP_STRIDE = 16
CMP_RATIO = CMP_LEN // CMP_STRIDE
CMP_HIDDEN = 128
SEL_BLOCK = 64
SEL_TOPN = 16
WIN_B = 512
PAGE_SIZE = 128

LANES = 128
PROJ_TN = 256
PROJ_SUB = 2
VMEM_LIMIT = 56 * 1024 * 1024
IDX_ROWS = 24
SAMPLE_ROWS = 8

B_QC, B_QR, B_Z, B_KV, B_WIN, B_GATE = 0, 1024, 2048, 3072, 4096, 4608
B_COLS = B_GATE + 6 * LANES

EP_NONE, EP_NORM_Q, EP_ROPE_Q, EP_NORM_K, EP_ROPE_K, EP_SIG = range(6)


def _cparams(sem):
    return pltpu.CompilerParams(dimension_semantics=sem, vmem_limit_bytes=VMEM_LIMIT)


def _nt_dot(a, b):
    return lax.dot_general(a, b, (((1,), (1,)), ((), ())), preferred_element_type=F32)


def _dot(a, b):
    return jnp.dot(a, b, preferred_element_type=F32)


def _head_norm(acc, bd_ref, gain):
    msq = _dot((acc * acc).astype(BF16), bd_ref[...])
    return acc * lax.rsqrt(msq + EPS) * gain


def _rope_tile(y, cos, sin_signed):
    lane = lax.broadcasted_iota(jnp.int32, (y.shape[0], LANES), 1)
    first_half = (lane & (HEAD_DIM - 1)) < (HEAD_DIM // 2)
    outs = []
    for c in range(y.shape[1] // LANES):
        yc = y[:, c * LANES:(c + 1) * LANES]
        partner = jnp.where(first_half, pltpu.roll(yc, LANES - HEAD_DIM // 2, axis=1),
                            pltpu.roll(yc, HEAD_DIM // 2, axis=1))
        outs.append(yc * cos + partner * sin_signed)
    return jnp.concatenate(outs, axis=1)


def _proj_kernel(x_ref, g_ref, w_ref, cos_ref, sin_ref, qg_ref, kg_ref, bd_ref, o_ref, h_ref, *, kinds):
    j = pl.program_id(1)

    @pl.when(j == 0)
    def _():
        x = x_ref[...]
        r = lax.rsqrt(jnp.mean(x * x, axis=-1, keepdims=True) + EPS)
        h_ref[...] = (x * r * g_ref[...]).astype(BF16)

    acc_all = _dot(h_ref[...], w_ref[...])

    def ranges(kind, half):
        out, start = [], None
        for t, k in enumerate(list(kinds[half::PROJ_SUB]) + [None]):
            if k == kind and start is None:
                start = t
            if k != kind and start is not None:
                out.append((start, t))
                start = None
        return out

    for half in range(PROJ_SUB):
        acc = acc_all[:, half * PROJ_TN:(half + 1) * PROJ_TN]
        cols = slice(half * PROJ_TN, (half + 1) * PROJ_TN)

        def emit(kind, fn, half=half, cols=cols):
            rs = ranges(kind, half)
            if not rs:
                return
            cond = None
            for a, b in rs:
                c = (j >= a) & (j < b)
                cond = c if cond is None else (cond | c)

            @pl.when(cond)
            def _():
                o_ref[:, cols] = fn()

        emit(EP_NONE, lambda acc=acc: acc)
        emit(EP_SIG, lambda acc=acc: jax.nn.sigmoid(acc))
        emit(EP_NORM_Q, lambda acc=acc: _head_norm(acc, bd_ref, qg_ref[...]))
        emit(EP_NORM_K, lambda acc=acc: _head_norm(acc, bd_ref, kg_ref[...]))
        emit(EP_ROPE_Q, lambda acc=acc: _rope_tile(_head_norm(acc, bd_ref, qg_ref[...]), cos_ref[...], sin_ref[...]))
        emit(EP_ROPE_K, lambda acc=acc: _rope_tile(_head_norm(acc, bd_ref, kg_ref[...]), cos_ref[...], sin_ref[...]))


def _project(x, norm_g, w_bf16, kinds, cos_t, sin_t, q_gain, k_gain):
    T, D = x.shape
    E = w_bf16.shape[1]
    tn = PROJ_TN
    assert E == len(kinds) * tn and len(kinds) % PROJ_SUB == 0
    tm = min(T, 2048)
    assert T % tm == 0
    rep = tn // HEAD_DIM
    qg = jnp.tile(q_gain.astype(F32), rep)[None]
    kg = jnp.tile(k_gain.astype(F32), rep)[None]
    hid = jnp.arange(tn) // HEAD_DIM
    bd = ((hid[:, None] == hid[None, :]).astype(F32) / HEAD_DIM).astype(BF16)
    return pl.pallas_call(
        functools.partial(_proj_kernel, kinds=tuple(kinds)),
        grid=(T // tm, E // (PROJ_SUB * tn)),
        in_specs=[
            pl.BlockSpec((tm, D), lambda i, j: (i, 0)),
            pl.BlockSpec((1, D), lambda i, j: (0, 0)),
            pl.BlockSpec((D, PROJ_SUB * tn), lambda i, j: (0, j)),
            pl.BlockSpec((tm, LANES), lambda i, j: (i, 0)),
            pl.BlockSpec((tm, LANES), lambda i, j: (i, 0)),
            pl.BlockSpec((1, tn), lambda i, j: (0, 0)),
            pl.BlockSpec((1, tn), lambda i, j: (0, 0)),
            pl.BlockSpec((tn, tn), lambda i, j: (0, 0)),
        ],
        out_specs=pl.BlockSpec((tm, PROJ_SUB * tn), lambda i, j: (i, j)),
        out_shape=jax.ShapeDtypeStruct((T, E), F32),
        scratch_shapes=[pltpu.VMEM((tm, D), BF16)],
        compiler_params=_cparams(("parallel", "arbitrary")),
    )(x, norm_g.astype(F32)[None], w_bf16, cos_t, sin_t, qg, kg, bd)


def _rope_tables(pos):
    half = HEAD_DIM // 2
    inv_freq = ROPE_THETA ** (-jnp.arange(half, dtype=F32) / half)
    ang = pos.astype(F32)[:, None] * inv_freq[None, :]
    cos, sin = jnp.cos(ang), jnp.sin(ang)
    cos_t = jnp.concatenate([cos, cos, cos, cos], axis=1)
    sin_t = jnp.concatenate([-sin, sin, -sin, sin], axis=1)
    return cos_t, sin_t


def _prep_q(q, nh, gqa, tq):
    lane = lax.broadcasted_iota(jnp.int32, (tq, LANES), 1)
    low = lane < HEAD_DIM
    parts = []
    for j in range(nh):
        col = q[:, (j // 2) * LANES:(j // 2 + 1) * LANES]
        nat, tgt = j % 2, (j // gqa) % 2
        if nat != tgt:
            col = pltpu.roll(col, HEAD_DIM, axis=1)
        keep = low if tgt == 0 else jnp.logical_not(low)
        parts.append(jnp.where(keep, col * SCALE, 0.0))
    return jnp.concatenate(parts, axis=0).astype(BF16)


def _unprep_o(o, nh, gqa, tq):
    lane = lax.broadcasted_iota(jnp.int32, (tq, LANES), 1)
    low = lane < HEAD_DIM
    cols = []
    for c in range(nh // 2):
        halves = []
        for nat in (0, 1):
            j = 2 * c + nat
            tgt = (j // gqa) % 2
            oj = o[j * tq:(j + 1) * tq]
            if tgt != nat:
                oj = pltpu.roll(oj, HEAD_DIM, axis=1)
            halves.append(oj)
        cols.append(jnp.where(low, halves[0], halves[1]))
    return jnp.concatenate(cols, axis=1)


def _apply_gates(o, g_ref, nh, tq):
    if g_ref is None:
        return o
    g = g_ref[...]
    return jnp.concatenate([o[j * tq:(j + 1) * tq] * g[:, j:j + 1] for j in range(nh)], axis=0)


def _q_transposed(q, nh, gqa, tq):
    qt = (q * SCALE).T
    zero = jnp.zeros((HEAD_DIM, tq), F32)
    cols = []
    for j in range(nh):
        h = qt[j * HEAD_DIM:(j + 1) * HEAD_DIM]
        cols.append(jnp.concatenate([h, zero] if (j // gqa) % 2 == 0 else [zero, h], axis=0))
    return jnp.concatenate(cols, axis=1).astype(BF16)


def _masked_heads(s, masks, tq):
    return jnp.concatenate([jnp.where(mk, s[:, j * tq:(j + 1) * tq], NEG) for j, mk in enumerate(masks)], axis=1)


ACC_ROWS = LANES + 16


def _attend_tile_t(kt, vt_t, qts, masks, tq, m_ref, acc_ref):
    n_sl = len(qts)
    rs = qts[0].shape[1]
    s = jnp.concatenate([_dot(kt[:, sl * LANES:(sl + 1) * LANES], qts[sl]) for sl in range(n_sl)], axis=1)
    s = _masked_heads(s, masks, tq)
    m_old = m_ref[...]
    m_new = jnp.maximum(m_old, jnp.max(s, axis=0, keepdims=True))
    alpha = jnp.exp(m_old - m_new)
    p = jnp.exp((s - m_new).astype(BF16))
    ones = jnp.ones((ACC_ROWS - LANES, kt.shape[0]), BF16)
    for sl in range(n_sl):
        cols = slice(sl * rs, (sl + 1) * rs)
        vt1 = jnp.concatenate([vt_t[sl * LANES:(sl + 1) * LANES], ones], axis=0)
        acc_ref[sl] = alpha[:, cols] * acc_ref[sl] + _dot(vt1, p[:, cols])
    m_ref[...] = m_new


def _init_stats(m_ref, acc_ref):
    m_ref[...] = jnp.full(m_ref.shape, NEG, F32)
    acc_ref[...] = jnp.zeros(acc_ref.shape, F32)


def _finish_t(o_ref, lse_ref, g_ref, m_ref, acc_ref, nh, gqa, tq):
    n_sl = acc_ref.shape[0]
    g_t = g_ref[...].T if g_ref is not None else None
    parts, lse_rows = [], []
    for sl in range(n_sl):
        l = acc_ref[sl, LANES:LANES + 1, :]
        inv = 1.0 / l
        for j in range(nh):
            half = (j // gqa) % 2
            sc = inv[:, j * tq:(j + 1) * tq]
            if g_t is not None:
                sc = sc * g_t[sl * LANES + j:sl * LANES + j + 1, :]
            parts.append(acc_ref[sl, half * HEAD_DIM:(half + 1) * HEAD_DIM, j * tq:(j + 1) * tq] * sc)
        if lse_ref is not None:
            lse = m_ref[:, sl * nh * tq:(sl + 1) * nh * tq] + jnp.log(l)
            lse_rows += [jnp.broadcast_to(lse[:, j * tq:(j + 1) * tq], (HEAD_DIM, tq)) for j in range(nh)]
    o_ref[...] = jnp.concatenate(parts, axis=0).T
    if lse_ref is not None:
        lse_ref[...] = jnp.concatenate(lse_rows, axis=0).T


def _slab_queries(q_ref, n_sl, nh, gqa, tq):
    qs = nh * HEAD_DIM
    return [_q_transposed(q_ref[:, sl * qs:(sl + 1) * qs], nh, gqa, tq) for sl in range(n_sl)]


def _band_kernel(*refs, n_sl, nh, gqa, tq, tk, kw, want_lse, gated):
    q_ref, k_ref, vt_ref = refs[:3]
    pos = 3
    g_ref = None
    if gated:
        g_ref = refs[pos]
        pos += 1
    o_ref = refs[pos]
    pos += 1
    lse_ref = None
    if want_lse:
        lse_ref = refs[pos]
        pos += 1
    m_ref, acc_ref = refs[pos:pos + 2]

    i = pl.program_id(2)
    q0 = i * tq
    qts = _slab_queries(q_ref, n_sl, nh, gqa, tq)
    _init_stats(m_ref, acc_ref)
    rel = lax.broadcasted_iota(jnp.int32, (tk, tq), 1) - lax.broadcasted_iota(jnp.int32, (tk, tq), 0)

    def body(jj, carry):
        ks = pl.multiple_of(jj * tk, tk)
        kt = k_ref[pl.ds(ks, tk), :].astype(BF16)
        vt_t = vt_ref[:, pl.ds(ks, tk)].astype(BF16)
        dist = rel + (q0 - ks)
        _attend_tile_t(kt, vt_t, qts, [(dist >= 0) & (dist <= kw)] * (n_sl * nh), tq, m_ref, acc_ref)
        return carry

    lo_t = jnp.maximum(q0 - kw, 0) // tk
    hi_t = (q0 + tq - 1) // tk
    lax.fori_loop(lo_t, hi_t + 1, body, 0)
    _finish_t(o_ref, lse_ref, g_ref, m_ref, acc_ref, nh, gqa, tq)


def _band_attention(u, vt, *, q_off, k_off, nh, gqa, n_slabs, kw, dil, want_lse,
                    gates=None, g_off=0, tq=256, tk=256):
    B, S, W = u.shape
    n = S // dil
    tq = min(tq, n)
    tk = min(tk, n)
    uv = u.reshape(B, n, dil * W)
    wo = n_slabs * nh * HEAD_DIM
    kwid = n_slabs * LANES
    assert q_off % wo == 0 and k_off % kwid == 0 and (dil == 1 or (W % wo == 0 and W % kwid == 0))
    in_specs = [
        pl.BlockSpec((None, tq, wo), lambda b, r, i: (b, i, r * (W // wo) + q_off // wo)),
        pl.BlockSpec((None, n, kwid), lambda b, r, i: (b, 0, r * (W // kwid) + k_off // kwid)),
        pl.BlockSpec((None, None, kwid, n), lambda b, r, i: (b, r, 0, 0)),
    ]
    args = [uv, uv, vt]
    if gates is not None:
        Wg = gates.shape[-1]
        assert g_off % kwid == 0 and (dil == 1 or Wg % kwid == 0)
        in_specs.append(pl.BlockSpec((None, tq, kwid), lambda b, r, i: (b, i, r * (Wg // kwid) + g_off // kwid)))
        args.append(gates.reshape(B, n, dil * Wg))
    o_spec = pl.BlockSpec((None, tq, wo), lambda b, r, i: (b, i, r))
    o_shape = jax.ShapeDtypeStruct((B, n, dil * wo), F32)
    R = nh * tq
    res = pl.pallas_call(
        functools.partial(_band_kernel, n_sl=n_slabs, nh=nh, gqa=gqa, tq=tq, tk=tk, kw=kw, want_lse=want_lse,
                          gated=gates is not None),
        grid=(B, dil, n // tq),
        in_specs=in_specs,
        out_specs=[o_spec, o_spec] if want_lse else o_spec,
        out_shape=[o_shape, o_shape] if want_lse else o_shape,
        scratch_shapes=[pltpu.VMEM((1, n_slabs * R), F32), pltpu.VMEM((n_slabs, ACC_ROWS, R), F32)],
        compiler_params=_cparams(("parallel", "parallel", "arbitrary")),
    )(*args)
    if want_lse:
        return res[0].reshape(B, S, wo), res[1].reshape(B, S, wo)
    return res.reshape(B, S, wo)


def _values_transposed(u, v_off, width, dil):
    B, S, _ = u.shape
    v = u[:, :, v_off:v_off + width].reshape(B, S // dil, dil, width)
    return jnp.transpose(v, (0, 2, 3, 1))


def _compress_math(x_ref, w1_ref, pe_ref, w1f_ref, w2_ref, bd_ref, kg_ref, is_key, n_chunk):
    hid0 = _dot(pe_ref[...], w1f_ref[...])[0:1]
    acc = jnp.zeros((n_chunk, 4 * CMP_HIDDEN), F32)
    for j in range(CMP_STRIDE):
        xj = x_ref[pl.ds(j, n_chunk, stride=CMP_STRIDE), :].astype(BF16)
        acc = acc + _dot(xj, w1_ref[j])
    out = jnp.zeros((n_chunk, LANES), F32)
    for a in range(2):
        p0 = acc[:, (2 * a) * CMP_HIDDEN:(2 * a + 1) * CMP_HIDDEN]
        p1 = acc[:, (2 * a + 1) * CMP_HIDDEN:(2 * a + 2) * CMP_HIDDEN]
        hid = hid0 + p0 + pltpu.roll(p1, n_chunk - 1, axis=0)
        out = out + _dot(jax.nn.silu(hid).astype(BF16), w2_ref[a])
    normed = _head_norm(out, bd_ref, kg_ref[...])
    return jnp.where(is_key, normed, out)


def _compress_prompt_kernel(x_ref, w1_ref, pe_ref, w1f_ref, w2_ref, bd_ref, kg_ref, o_ref, ot_ref, *, n_chunk):
    is_key = pl.program_id(1) == 0
    out = _compress_math(x_ref, w1_ref, pe_ref, w1f_ref, w2_ref, bd_ref, kg_ref, is_key, n_chunk)
    o_ref[...] = out
    ot_ref[...] = out.T


def _compress_weights(pe, w1, w2, k_gain):
    w1r = w1.reshape(2, CMP_RATIO, CMP_STRIDE, HEAD_DIM, CMP_HIDDEN)
    wj = jnp.concatenate([w1r[:, 0], w1r[:, 1]], axis=-1)
    z = jnp.zeros_like(wj)
    w1bd = jnp.concatenate([jnp.concatenate([wj, z], axis=-1),
                            jnp.concatenate([z, wj], axis=-1)], axis=-2).astype(BF16)
    pe8 = jnp.concatenate([pe.reshape(2, 1, CMP_LEN * HEAD_DIM),
                           jnp.zeros((2, 7, CMP_LEN * HEAD_DIM), F32)], axis=1).astype(BF16)
    w1f = w1.reshape(2, CMP_LEN * HEAD_DIM, CMP_HIDDEN).astype(BF16)
    z2 = jnp.zeros_like(w2)
    w2pad = jnp.stack([jnp.concatenate([w2, z2], axis=-1),
                       jnp.concatenate([z2, w2], axis=-1)], axis=1).astype(BF16)
    hid = jnp.arange(LANES) // HEAD_DIM
    bd = ((hid[:, None] == hid[None, :]).astype(F32) / HEAD_DIM).astype(BF16)
    kg = jnp.tile(k_gain.astype(F32), 2)[None]
    return w1bd, pe8, w1f, w2pad, bd, kg


def _cw_specs(nd):
    def sp(shape, fn):
        return pl.BlockSpec(shape, fn)
    if nd == 3:
        return [
            sp((None, CMP_STRIDE, LANES, 4 * CMP_HIDDEN), lambda b, t, p: (t, 0, 0, 0)),
            sp((None, 8, CMP_LEN * HEAD_DIM), lambda b, t, p: (t, 0, 0)),
            sp((None, CMP_LEN * HEAD_DIM, CMP_HIDDEN), lambda b, t, p: (t, 0, 0)),
            sp((None, 2, CMP_HIDDEN, LANES), lambda b, t, p: (t, 0, 0, 0)),
            sp((LANES, LANES), lambda b, t, p: (0, 0)),
            sp((1, LANES), lambda b, t, p: (0, 0)),
        ]
    return [
        sp((None, CMP_STRIDE, LANES, 4 * CMP_HIDDEN), lambda b, t, p, *_: (t, 0, 0, 0)),
        sp((None, 8, CMP_LEN * HEAD_DIM), lambda b, t, p, *_: (t, 0, 0)),
        sp((None, CMP_LEN * HEAD_DIM, CMP_HIDDEN), lambda b, t, p, *_: (t, 0, 0)),
        sp((None, 2, CMP_HIDDEN, LANES), lambda b, t, p, *_: (t, 0, 0, 0)),
        sp((LANES, LANES), lambda b, t, p, *_: (0, 0)),
        sp((1, LANES), lambda b, t, p, *_: (0, 0)),
    ]


def _compress_prompt(u, col0, cw):
    B, S, W = u.shape
    n_chunk = S // CMP_STRIDE
    base = col0 // LANES
    return pl.pallas_call(
        functools.partial(_compress_prompt_kernel, n_chunk=n_chunk),
        grid=(B, 2, 2),
        in_specs=[pl.BlockSpec((None, S, LANES), lambda b, t, p: (b, 0, base + 2 * t + p))] + _cw_specs(3),
        out_specs=[pl.BlockSpec((None, None, n_chunk, LANES), lambda b, t, p: (b, t, 0, p)),
                   pl.BlockSpec((None, None, LANES, n_chunk), lambda b, t, p: (b, t, p, 0))],
        out_shape=[jax.ShapeDtypeStruct((B, 2, n_chunk, KVW_B), F32),
                   jax.ShapeDtypeStruct((B, 2, KVW_B, n_chunk), F32)],
        compiler_params=_cparams(("parallel", "parallel", "parallel")),
    )(u, *cw)


def _compress_sample_kernel(pt_ref, pool_ref, w1_ref, pe_ref, w1f_ref, w2_ref, bd_ref, kg_ref, o_ref,
                            raw, xbuf, sem, *, n_pages, n_chunk):
    n, t, p = pl.program_id(0), pl.program_id(1), pl.program_id(2)
    step = (n * 2 + t) * 2 + p
    n_steps = pl.num_programs(0) * 4

    def page_copy(page, rt, pair, slot, pg):
        row0 = pl.multiple_of(pair * LANES, LANES)
        return pltpu.make_async_copy(pool_ref.at[page, rt, pl.ds(row0, LANES), :], raw.at[slot, pg], sem.at[slot])

    def issue(st, slot):
        seq, rt, pair = st // 4, (st // 2) % 2, st % 2

        def go(pg, c):
            page_copy(pt_ref[seq * n_pages + pg], rt, pair, slot, pg).start()
            return c
        lax.fori_loop(0, n_pages, go, 0, unroll=8)

    @pl.when(step == 0)
    def _():
        issue(step, 0)

    @pl.when(step + 1 < n_steps)
    def _():
        issue(step + 1, (step + 1) % 2)

    slot = step % 2

    def wait(pg, c):
        page_copy(0, t, p, slot, pg).wait()
        return c
    lax.fori_loop(0, n_pages, wait, 0, unroll=8)

    def to_token_major(pg, c):
        xbuf[pl.ds(pl.multiple_of(pg * PAGE_SIZE, PAGE_SIZE), PAGE_SIZE), :] = raw[slot, pg].T
        return c
    lax.fori_loop(0, n_pages, to_token_major, 0, unroll=8)

    o_ref[...] = _compress_math(xbuf, w1_ref, pe_ref, w1f_ref, w2_ref, bd_ref, kg_ref, t == 0, n_chunk)


def _compress_sample(pool, page_table, cw):
    N, n_pages = page_table.shape
    n_chunk = n_pages * PAGE_SIZE // CMP_STRIDE
    gs = pltpu.PrefetchScalarGridSpec(
        num_scalar_prefetch=1,
        grid=(N, 2, 2),
        in_specs=[pl.BlockSpec(memory_space=pl.ANY)] + _cw_specs(4),
        out_specs=pl.BlockSpec((None, None, n_chunk, LANES), lambda b, t, p, *_: (b, t, 0, p)),
        scratch_shapes=[pltpu.VMEM((2, n_pages, LANES, PAGE_SIZE), F32),
                        pltpu.VMEM((n_pages * PAGE_SIZE, LANES), F32), pltpu.SemaphoreType.DMA((2,))],
    )
    return pl.pallas_call(
        functools.partial(_compress_sample_kernel, n_pages=n_pages, n_chunk=n_chunk),
        grid_spec=gs,
        out_shape=jax.ShapeDtypeStruct((N, 2, n_chunk, KVW_B), F32),
        compiler_params=_cparams(("arbitrary", "arbitrary", "arbitrary")),
    )(page_table.reshape(-1), pool, *cw)


def _cmp_kernel(q_ref, kc_ref, vc_ref, cov_ref, g_ref, o_ref, sel_ref, idx_ref, *,
                tq, n_chunk, nselp, pos_base, top_n):
    nh, gqa = 2 * GQA_B, GQA_B
    i = pl.program_id(2)
    R = nh * tq
    Q = _prep_q(q_ref[...], nh, gqa, tq)
    kc = kc_ref[...].astype(BF16)
    vc = vc_ref[...].astype(BF16)
    s = _nt_dot(Q, kc)
    pos_r = pos_base + i * tq + (lax.broadcasted_iota(jnp.int32, (R, n_chunk), 0) & (tq - 1))
    cend = lax.broadcasted_iota(jnp.int32, (R, n_chunk), 1) * CMP_STRIDE + (CMP_LEN - 1)
    s = jnp.where(cend <= pos_r, s, -jnp.inf)
    m = jnp.max(s, axis=-1, keepdims=True)
    m = jnp.where(m > -jnp.inf, m, 0.0)
    e = jnp.exp(s - m)
    den = jnp.sum(e, axis=-1, keepdims=True)
    p = e / jnp.maximum(den, TINY)
    o = _dot(p.astype(BF16), vc)
    o = _apply_gates(o, g_ref, nh, tq)
    o_ref[...] = _unprep_o(o, nh, gqa, tq)

    imps = []
    for a in range(2):
        ps = p[(a * gqa) * tq:(a * gqa + 1) * tq]
        for g in range(1, gqa):
            ps = ps + p[(a * gqa + g) * tq:(a * gqa + g + 1) * tq]
        hi = ps.astype(BF16)
        lo = (ps - hi.astype(F32)).astype(BF16)
        imps.append(_nt_dot(cov_ref[...], hi) + _nt_dot(cov_ref[...], lo))
    _select_blocks(imps, sel_ref, idx_ref, pos_base + i * tq, tq, nselp, top_n)


def _select_blocks(imps, sel_ref, idx_ref, pos0, tq, nselp, top_n):
    blk = lax.broadcasted_iota(jnp.int32, (nselp, tq), 0)
    cur = (pos0 + lax.broadcasted_iota(jnp.int32, (1, tq), 1)) // SEL_BLOCK
    valid = blk <= cur
    forced = (blk == 0) | (blk == cur) | (blk == cur - 1)
    n_forced = jnp.sum(forced.astype(jnp.int32), axis=0, keepdims=True)
    zero_row = jnp.zeros((1, tq), jnp.int32)
    for a, imp in enumerate(imps):
        rem = jnp.where(valid & jnp.logical_not(forced), imp, -1.0)
        sel = forced
        idx_ref[a, pl.ds(0, 1), :] = zero_row
        idx_ref[a, pl.ds(1, 1), :] = cur
        idx_ref[a, pl.ds(2, 1), :] = jnp.maximum(cur - 1, 0)
        for it in range(top_n - 1):
            mx = jnp.max(rem, axis=0, keepdims=True)
            first = jnp.min(jnp.where(rem == mx, blk, nselp), axis=0, keepdims=True)
            active = (it < top_n - n_forced) & (mx >= 0.0)
            pick = (blk == first) & active
            sel = sel | pick
            rem = jnp.where(pick, -1.0, rem)
            idx_ref[a, pl.ds(3 + it, 1), :] = jnp.where(active, first, 0)
        for r in range(3 + top_n - 1, IDX_ROWS):
            idx_ref[a, pl.ds(r, 1), :] = zero_row
        sel_ref[a] = sel.astype(F32)


def _cmp_t_kernel(q_ref, kc_ref, vct_ref, cov_ref, g_ref, o_ref, sel_ref, idx_ref, *,
                  tq, n_chunk, nselp, pos_base, top_n):
    nh, gqa = 2 * GQA_B, GQA_B
    i = pl.program_id(2)
    qt = _q_transposed(q_ref[...], nh, gqa, tq)
    s = _dot(kc_ref[...].astype(BF16), qt)
    pos_l = pos_base + i * tq + lax.broadcasted_iota(jnp.int32, (n_chunk, tq), 1)
    cend = lax.broadcasted_iota(jnp.int32, (n_chunk, tq), 0) * CMP_STRIDE + (CMP_LEN - 1)
    ok = cend <= pos_l
    s = jnp.concatenate([jnp.where(ok, s[:, j * tq:(j + 1) * tq], -jnp.inf) for j in range(nh)], axis=1)
    m = jnp.max(s, axis=0, keepdims=True)
    m = jnp.where(m > -jnp.inf, m, 0.0)
    e = jnp.exp(s - m)
    den = jnp.sum(e, axis=0, keepdims=True)
    p = e / jnp.maximum(den, TINY)
    acc = _dot(vct_ref[...].astype(BF16), p.astype(BF16))
    g_t = g_ref[...].T
    parts = []
    for j in range(nh):
        half = (j // gqa) % 2
        parts.append(acc[half * HEAD_DIM:(half + 1) * HEAD_DIM, j * tq:(j + 1) * tq] * g_t[j:j + 1, :])
    o_ref[...] = jnp.concatenate(parts, axis=0).T

    imps = []
    for a in range(2):
        ps = p[:, (a * gqa) * tq:(a * gqa + 1) * tq]
        for g in range(1, gqa):
            ps = ps + p[:, (a * gqa + g) * tq:(a * gqa + g + 1) * tq]
        hi = ps.astype(BF16)
        lo = (ps - hi.astype(F32)).astype(BF16)
        imps.append(_dot(cov_ref[...], hi) + _dot(cov_ref[...], lo))
    _select_blocks(imps, sel_ref, idx_ref, pos_base + i * tq, tq, nselp, top_n)


def _cover_t(n_chunk, n_cmp, n_sel, nselp):
    c = jnp.arange(n_chunk)[None, :]
    j = jnp.arange(nselp)[:, None]
    cov = ((c * CMP_STRIDE <= j * SEL_BLOCK + SEL_BLOCK - 1) & (c * CMP_STRIDE + CMP_LEN - 1 >= j * SEL_BLOCK)
           & (c < n_cmp) & (j < n_sel))
    return cov.astype(BF16)


def _cmp_attention(q, q_off, kvc, gates, g_off, *, tq, pos_base, n_keys, kvc_t=None):
    N, Tq, W = q.shape
    n_chunk = kvc.shape[2]
    n_cmp = (n_keys - CMP_LEN) // CMP_STRIDE + 1
    n_sel = -(-n_keys // SEL_BLOCK)
    nselp = -(-n_sel // 8) * 8
    top_n = min(SEL_TOPN, n_sel)
    qs = 2 * GQA_B * HEAD_DIM
    Wg = gates.shape[-1]
    cov = _cover_t(n_chunk, n_cmp, n_sel, nselp)
    if kvc_t is None:
        body, values = _cmp_kernel, kvc
        v_spec = pl.BlockSpec((None, None, n_chunk, LANES), lambda b, p, i: (b, 1, 0, p))
    else:
        body, values = _cmp_t_kernel, kvc_t
        v_spec = pl.BlockSpec((None, None, LANES, n_chunk), lambda b, p, i: (b, 1, p, 0))
    return pl.pallas_call(
        functools.partial(body, tq=tq, n_chunk=n_chunk, nselp=nselp, pos_base=pos_base, top_n=top_n),
        grid=(N, 2, Tq // tq),
        in_specs=[
            pl.BlockSpec((None, tq, qs), lambda b, p, i: (b, i, q_off // qs + p)),
            pl.BlockSpec((None, None, n_chunk, LANES), lambda b, p, i: (b, 0, 0, p)),
            v_spec,
            pl.BlockSpec((nselp, n_chunk), lambda b, p, i: (0, 0)),
            pl.BlockSpec((None, tq, LANES), lambda b, p, i: (b, i, g_off // LANES + p)),
        ],
        out_specs=[
            pl.BlockSpec((None, tq, qs), lambda b, p, i: (b, i, p)),
            pl.BlockSpec((None, 2, nselp, tq), lambda b, p, i: (b, p, 0, i)),
            pl.BlockSpec((None, 2, IDX_ROWS, tq), lambda b, p, i: (b, p, 0, i)),
        ],
        out_shape=[
            jax.ShapeDtypeStruct((N, Tq, WIDTH_B), F32),
            jax.ShapeDtypeStruct((N, KV_HEADS_B, nselp, Tq), F32),
            jax.ShapeDtypeStruct((N, KV_HEADS_B, IDX_ROWS, Tq), jnp.int32),
        ],
        compiler_params=_cparams(("parallel", "parallel", "parallel")),
    )(q, kvc, values, cov, gates)


def _sel_prompt_kernel(q_ref, k_ref, vt_ref, sel_ref, g_ref, o_ref, m_ref, acc_ref, *, tq, tk):
    nh, gqa, n_sl = 2 * GQA_B, GQA_B, KV_HEADS_B // 2
    i = pl.program_id(1)
    q0 = i * tq
    qts = _slab_queries(q_ref, n_sl, nh, gqa, tq)
    _init_stats(m_ref, acc_ref)
    nblk = tk // SEL_BLOCK
    rel = lax.broadcasted_iota(jnp.int32, (tk, tq), 1) - lax.broadcasted_iota(jnp.int32, (tk, tq), 0)

    def body(jj, carry):
        ks = pl.multiple_of(jj * tk, tk)
        kt = k_ref[pl.ds(ks, tk), :].astype(BF16)
        vt_t = vt_ref[:, pl.ds(ks, tk)].astype(BF16)
        causal = rel + (q0 - ks) >= 0
        masks = []
        for kvh in range(KV_HEADS_B):
            rows = sel_ref[kvh, pl.ds(pl.multiple_of(jj * nblk, nblk), nblk), :]
            chosen = jnp.concatenate(
                [jnp.broadcast_to(rows[c:c + 1], (SEL_BLOCK, tq)) for c in range(nblk)], axis=0)
            masks += [(chosen > 0.5) & causal] * gqa
        _attend_tile_t(kt, vt_t, qts, masks, tq, m_ref, acc_ref)
        return carry

    lax.fori_loop(0, (q0 + tq - 1) // tk + 1, body, 0)
    _finish_t(o_ref, None, g_ref, m_ref, acc_ref, nh, gqa, tq)


def _sel_prompt(u, vt, sel_t, *, q_off, k_off, g_off, tq=128, tk=512):
    B, S, W = u.shape
    n_sl = KV_HEADS_B // 2
    nselp = sel_t.shape[2]
    tq = min(tq, S)
    tk = min(tk, S)
    assert tk % SEL_BLOCK == 0 and (tk // SEL_BLOCK) % 8 == 0 and S % tk == 0 and nselp * SEL_BLOCK >= S
    assert q_off % WIDTH_B == 0 and k_off % KVW_B == 0 and g_off % KVW_B == 0
    R = 2 * GQA_B * tq
    return pl.pallas_call(
        functools.partial(_sel_prompt_kernel, tq=tq, tk=tk),
        grid=(B, S // tq),
        in_specs=[
            pl.BlockSpec((None, tq, WIDTH_B), lambda b, i: (b, i, q_off // WIDTH_B)),
            pl.BlockSpec((None, S, KVW_B), lambda b, i: (b, 0, k_off // KVW_B)),
            pl.BlockSpec((None, None, KVW_B, S), lambda b, i: (b, 0, 0, 0)),
            pl.BlockSpec((None, KV_HEADS_B, nselp, tq), lambda b, i: (b, 0, 0, i)),
            pl.BlockSpec((None, tq, KVW_B), lambda b, i: (b, i, g_off // KVW_B)),
        ],
        out_specs=pl.BlockSpec((None, tq, WIDTH_B), lambda b, i: (b, i, 0)),
        out_shape=jax.ShapeDtypeStruct((B, S, WIDTH_B), F32),
        scratch_shapes=[pltpu.VMEM((1, n_sl * R), F32), pltpu.VMEM((n_sl, ACC_ROWS, R), F32)],
        compiler_params=_cparams(("parallel", "arbitrary")),
    )(u, u, vt, sel_t, u)


def _sel_sample_kernel(pt_ref, idx_ref, q_ref, kn_ref, vn_ref, g_ref, pool_ref, o_ref, kvbuf, sem, *,
                       nt, n_pages, n_pick):
    nh, gqa, tq = 2 * GQA_B, GQA_B, SAMPLE_ROWS
    n, p, a = pl.program_id(0), pl.program_id(1), pl.program_id(2)
    step = (n * 2 + p) * 2 + a
    n_steps = pl.num_programs(0) * 4
    blocks_per_page = PAGE_SIZE // SEL_BLOCK
    width = n_pick * PAGE_SIZE

    def picked_block(seq, pair, head, t, r):
        src_row = jnp.where(r == 0, 0, r + 1)
        return idx_ref[((seq * KV_HEADS_B + pair * 2 + head) * IDX_ROWS + src_row) * tq + t]

    def page_copy(st, c, lookup):
        seq, pair, head = st // 4, (st // 2) % 2, st % 2
        t, r = c // n_pick, c % n_pick
        page = pt_ref[seq * n_pages + picked_block(seq, pair, head, t, r) // blocks_per_page] if lookup else 0
        dst0 = pl.multiple_of(r * PAGE_SIZE, PAGE_SIZE)
        half0 = pl.multiple_of(head * HEAD_DIM, HEAD_DIM)
        row0 = pl.multiple_of(pair * LANES + head * HEAD_DIM, HEAD_DIM)
        return pltpu.make_async_copy(
            pool_ref.at[page, pl.ds(2, 2), pl.ds(row0, HEAD_DIM), :],
            kvbuf.at[head, t, :, pl.ds(half0, HEAD_DIM), pl.ds(dst0, PAGE_SIZE)], sem.at[head])

    def issue(st):
        def go(c, carry):
            page_copy(st, c, True).start()
            return carry
        lax.fori_loop(0, nt * n_pick, go, 0, unroll=4)

    @pl.when(step == 0)
    def _():
        kvbuf[0, :, :, HEAD_DIM:, :] = jnp.zeros((nt, 2, HEAD_DIM, width), F32)
        kvbuf[1, :, :, :HEAD_DIM, :] = jnp.zeros((nt, 2, HEAD_DIM, width), F32)
        issue(step)

    @pl.when(step + 1 < n_steps)
    def _():
        issue(step + 1)

    slot = a

    def wait(c, carry):
        page_copy(step, c, False).wait()
        return carry
    lax.fori_loop(0, nt * n_pick, wait, 0, unroll=4)

    R = nh * tq
    Q = _prep_q(q_ref[...], nh, gqa, tq)
    pad = jnp.zeros((LANES - tq, LANES), F32)
    kn = jnp.concatenate([kn_ref[...], pad], axis=0).astype(BF16)
    vn = jnp.concatenate([vn_ref[...], pad], axis=0).astype(BF16)
    row = lax.broadcasted_iota(jnp.int32, (R, LANES), 0)
    colk = lax.broadcasted_iota(jnp.int32, (R, LANES), 1)
    trow = row & (tq - 1)
    s_new = jnp.where((colk <= trow) & (colk < nt), _nt_dot(Q, kn), NEG)
    lane = lax.broadcasted_iota(jnp.int32, (1, width), 1)
    lane_pick = lane // PAGE_SIZE
    lane_half = (lane // SEL_BLOCK) % blocks_per_page
    o = jnp.zeros((R, LANES), F32)
    for t in range(nt):
        chosen = lane < 0
        for r in range(n_pick):
            half = picked_block(n, p, a, t, r) % blocks_per_page
            chosen = chosen | ((lane_pick == r) & (lane_half == half))
        s = jnp.where(chosen, _dot(Q, kvbuf[slot, t, 0].astype(BF16)), NEG)
        m = jnp.maximum(jnp.max(s, axis=-1, keepdims=True), jnp.max(s_new, axis=-1, keepdims=True))
        p1 = jnp.exp(s - m)
        p2 = jnp.exp(s_new - m)
        l = jnp.sum(p1, axis=-1, keepdims=True) + jnp.sum(p2, axis=-1, keepdims=True)
        o_t = (_nt_dot(p1.astype(BF16), kvbuf[slot, t, 1].astype(BF16)) + _dot(p2.astype(BF16), vn)) / l
        o = jnp.where(trow == t, o_t, o)

    g = g_ref[...]
    low = lax.broadcasted_iota(jnp.int32, (tq, LANES), 1) < HEAD_DIM
    first = a == 0
    placed = []
    for jj in range(gqa):
        oj = jnp.where(first, o[jj * tq:(jj + 1) * tq], o[(gqa + jj) * tq:(gqa + jj + 1) * tq])
        gj = jnp.where(first, g[:, jj:jj + 1], g[:, gqa + jj:gqa + jj + 1])
        oj = oj * gj
        placed.append(jnp.where(a == jj % 2, oj, pltpu.roll(oj, HEAD_DIM, axis=1)))
    o_ref[...] = jnp.concatenate([jnp.where(low, placed[0], placed[1]),
                                  jnp.where(low, placed[2], placed[3])], axis=1)


def _sel_sample(us, idx, pool, page_table, *, nt, q_off, k_off, v_off, g_off):
    N, tq, W = us.shape
    n_pages = page_table.shape[1]
    n_pick = SEL_TOPN - 1
    qs = 2 * GQA_B * HEAD_DIM
    hw = GQA_B * HEAD_DIM
    gs = pltpu.PrefetchScalarGridSpec(
        num_scalar_prefetch=2,
        grid=(N, 2, 2),
        in_specs=[
            pl.BlockSpec((None, tq, qs), lambda b, p, a, *_: (b, 0, q_off // qs + p)),
            pl.BlockSpec((None, tq, LANES), lambda b, p, a, *_: (b, 0, k_off // LANES + p)),
            pl.BlockSpec((None, tq, LANES), lambda b, p, a, *_: (b, 0, v_off // LANES + p)),
            pl.BlockSpec((None, tq, LANES), lambda b, p, a, *_: (b, 0, g_off // LANES + p)),
            pl.BlockSpec(memory_space=pl.ANY),
        ],
        out_specs=pl.BlockSpec((None, tq, hw), lambda b, p, a, *_: (b, 0, 2 * p + a)),
        scratch_shapes=[
            pltpu.VMEM((2, nt, 2, LANES, n_pick * PAGE_SIZE), F32),
            pltpu.SemaphoreType.DMA((2,)),
        ],
    )
    return pl.pallas_call(
        functools.partial(_sel_sample_kernel, nt=nt, n_pages=n_pages, n_pick=n_pick),
        grid_spec=gs,
        out_shape=jax.ShapeDtypeStruct((N, tq, WIDTH_B), F32),
        compiler_params=_cparams(("arbitrary", "arbitrary", "arbitrary")),
    )(page_table.reshape(-1), idx.reshape(-1), us, us, us, us, pool)


def _cache_attn_kernel(*refs, n_sl, nh, gqa, L, dil, win, nt, want_lse, gated):
    q_ref, kc_ref, vc_ref, kn_ref, vn_ref = refs[:5]
    pos = 5
    g_ref = None
    if gated:
        g_ref = refs[pos]
        pos += 1
    o_ref = refs[pos]
    lse_ref = refs[pos + 1] if want_lse else None
    tq = SAMPLE_ROWS
    R = nh * tq
    qs = nh * HEAD_DIM
    pad = jnp.zeros((LANES - tq, LANES), F32)
    t1 = lax.broadcasted_iota(jnp.int32, (R, L), 0) & (tq - 1)
    d1 = L + t1 - lax.broadcasted_iota(jnp.int32, (R, L), 1)
    ok1 = ((d1 & (dil - 1)) == 0) & (d1 <= win)
    t2 = lax.broadcasted_iota(jnp.int32, (R, LANES), 0) & (tq - 1)
    c2 = lax.broadcasted_iota(jnp.int32, (R, LANES), 1)
    d2 = t2 - c2
    ok2 = (d2 >= 0) & ((d2 & (dil - 1)) == 0) & (d2 <= win) & (c2 < nt)
    for sl in range(n_sl):
        rows = slice(sl * LANES, (sl + 1) * LANES)
        Q = _prep_q(q_ref[:, sl * qs:(sl + 1) * qs], nh, gqa, tq)
        kc_t = kc_ref[rows, :].astype(BF16)
        vc_t = vc_ref[rows, :].astype(BF16)
        kn = jnp.concatenate([kn_ref[:, rows], pad], axis=0).astype(BF16)
        vn = jnp.concatenate([vn_ref[:, rows], pad], axis=0).astype(BF16)
        s1 = jnp.where(ok1, _dot(Q, kc_t), NEG)
        s2 = jnp.where(ok2, _nt_dot(Q, kn), NEG)
        m = jnp.maximum(jnp.max(s1, axis=-1, keepdims=True), jnp.max(s2, axis=-1, keepdims=True))
        p1 = jnp.exp(s1 - m)
        p2 = jnp.exp(s2 - m)
        l = jnp.sum(p1, axis=-1, keepdims=True) + jnp.sum(p2, axis=-1, keepdims=True)
        o = (_nt_dot(p1.astype(BF16), vc_t) + _dot(p2.astype(BF16), vn)) / l
        if g_ref is not None:
            g = g_ref[:, rows]
            o = jnp.concatenate([o[j * tq:(j + 1) * tq] * g[:, j:j + 1] for j in range(nh)], axis=0)
        o_ref[:, sl * qs:(sl + 1) * qs] = _unprep_o(o, nh, gqa, tq)
        if want_lse:
            lse_ref[:, sl * qs:(sl + 1) * qs] = _unprep_o(jnp.broadcast_to(m + jnp.log(l), (R, LANES)),
                                                          nh, gqa, tq)


def _feature_major(x):
    nd = x.ndim
    xt = jnp.transpose(x, (0, 1) + tuple(range(3, nd)) + (2,))
    return xt.reshape(x.shape[0], x.shape[1], -1, x.shape[2])


def _cache_attention(us, cache_t, layer, *, q_off, k_off, v_off, nh, gqa, n_slabs, dil, win, nt, want_lse,
                     g_off=None):
    N, tq, W = us.shape
    L = cache_t.shape[3]
    cache = cache_t
    wo = n_slabs * nh * HEAD_DIM
    kwid = n_slabs * LANES
    assert q_off % wo == 0 and k_off % kwid == 0 and v_off % kwid == 0
    in_specs = [
        pl.BlockSpec((None, tq, wo), lambda b: (b, 0, q_off // wo)),
        pl.BlockSpec((None, None, kwid, L), lambda b: (layer, b, 0, 0)),
        pl.BlockSpec((None, None, kwid, L), lambda b: (layer, b, 1, 0)),
        pl.BlockSpec((None, tq, kwid), lambda b: (b, 0, k_off // kwid)),
        pl.BlockSpec((None, tq, kwid), lambda b: (b, 0, v_off // kwid)),
    ]
    args = [us, cache, cache, us, us]
    if g_off is not None:
        assert g_off % kwid == 0
        in_specs.append(pl.BlockSpec((None, tq, kwid), lambda b: (b, 0, g_off // kwid)))
        args.append(us)
    o_spec = pl.BlockSpec((None, tq, wo), lambda b: (b, 0, 0))
    o_shape = jax.ShapeDtypeStruct((N, tq, wo), F32)
    return pl.pallas_call(
        functools.partial(_cache_attn_kernel, n_sl=n_slabs, nh=nh, gqa=gqa, L=L, dil=dil, win=win, nt=nt,
                          want_lse=want_lse, gated=g_off is not None),
        grid=(N,),
        in_specs=in_specs,
        out_specs=[o_spec, o_spec] if want_lse else o_spec,
        out_shape=[o_shape, o_shape] if want_lse else o_shape,
        compiler_params=_cparams(("parallel",)),
    )(*args)


def _merge_a_kernel(o0, o1, o2, l0, l1, l2, z_ref, x_ref, w_ref, y_ref):
    a0, a1, a2 = l0[...], l1[...], l2[...]
    mx = jnp.maximum(jnp.maximum(a0, a1), a2)
    e0, e1, e2 = jnp.exp(a0 - mx), jnp.exp(a1 - mx), jnp.exp(a2 - mx)
    den = e0 + e1 + e2
    o = (e0 / den) * o0[...] + (e1 / den) * o1[...] + (e2 / den) * o2[...]
    y_ref[...] = x_ref[...] + _dot((o * jax.nn.silu(z_ref[...])).astype(BF16), w_ref[...])


def _merge_a(outs, lses, u, z_off, x, w_out_bf16):
    T, D = x.shape
    tm = min(T, 512)
    wa = WIDTH_A
    row = pl.BlockSpec((tm, wa), lambda i: (i, 0))
    return pl.pallas_call(
        _merge_a_kernel,
        grid=(T // tm,),
        in_specs=[row] * 6 + [
            pl.BlockSpec((tm, wa), lambda i: (i, z_off // wa)),
            pl.BlockSpec((tm, D), lambda i: (i, 0)),
            pl.BlockSpec((wa, D), lambda i: (0, 0)),
        ],
        out_specs=pl.BlockSpec((tm, D), lambda i: (i, 0)),
        out_shape=jax.ShapeDtypeStruct((T, D), F32),
        compiler_params=_cparams(("parallel",)),
    )(*outs, *lses, u, x, w_out_bf16)


def _merge_b_kernel(oc, os_, ow, z_ref, x_ref, w_ref, y_ref):
    o = oc[...] + os_[...] + ow[...]
    y_ref[...] = x_ref[...] + _dot((o * jax.nn.silu(z_ref[...])).astype(BF16), w_ref[...])


def _merge_b(o_c, o_s, o_w, u, z_off, x, w_out_bf16):
    T, D = x.shape
    tm = min(T, 512)
    wb = WIDTH_B
    row = pl.BlockSpec((tm, wb), lambda i: (i, 0))
    return pl.pallas_call(
        _merge_b_kernel,
        grid=(T // tm,),
        in_specs=[row] * 3 + [
            pl.BlockSpec((tm, wb), lambda i: (i, z_off // wb)),
            pl.BlockSpec((tm, D), lambda i: (i, 0)),
            pl.BlockSpec((wb, D), lambda i: (0, 0)),
        ],
        out_specs=pl.BlockSpec((tm, D), lambda i: (i, 0)),
        out_shape=jax.ShapeDtypeStruct((T, D), F32),
        compiler_params=_cparams(("parallel",)),
    )(o_c, o_s, o_w, u, x, w_out_bf16)


A_KINDS = ([EP_ROPE_Q] * (N_GROUPS_A * WIDTH_A // PROJ_TN) + [EP_ROPE_K] * (N_GROUPS_A * WIDTH_A // PROJ_TN)
           + [EP_NONE] * (N_GROUPS_A * WIDTH_A // PROJ_TN) + [EP_NONE] * (WIDTH_A // PROJ_TN))
A_K, A_V, A_Z = N_GROUPS_A * WIDTH_A, 2 * N_GROUPS_A * WIDTH_A, QKV_A

B_KINDS = ([EP_NORM_Q] * 4 + [EP_ROPE_Q] * 4 + [EP_NONE] * 4
           + [EP_NORM_K, EP_NONE, EP_ROPE_K, EP_NONE, EP_ROPE_K, EP_NONE] + [EP_SIG] * 3 + [EP_NONE])
B_PAD = len(B_KINDS) * PROJ_TN - B_COLS


def _b_weight(w_in):
    wq = w_in[:, :WIDTH_B]
    o1 = WIDTH_B + 6 * KVW_B
    wkv = w_in[:, WIDTH_B:o1]
    wg = w_in[:, o1:o1 + 3 * HEADS_B]
    wz = w_in[:, o1 + 3 * HEADS_B:]
    nh = 2 * GQA_B
    tiles = []
    for b in range(3):
        for p in range(2):
            g = wg[:, b * HEADS_B + p * nh: b * HEADS_B + (p + 1) * nh]
            tiles.append(jnp.pad(g, ((0, 0), (0, LANES - nh))))
    tiles.append(jnp.zeros((w_in.shape[0], B_PAD), w_in.dtype))
    return jnp.concatenate([wq, wq, wz, wkv] + tiles, axis=1).astype(BF16)


def _pad_rows(u, N, T):
    return jnp.pad(u.reshape(N, T, -1), ((0, 0), (0, SAMPLE_ROWS - T), (0, 0)))


def _layer_a_prompt(x, norm_g, w_bf16, q_gain, k_gain, w_out_bf16, tabs):
    B, S, D = x.shape
    u = _project(x.reshape(B * S, D), norm_g, w_bf16, A_KINDS, tabs[0], tabs[1], q_gain, k_gain)
    u3 = u.reshape(B, S, -1)
    outs, lses, states = [], [], []
    for g, (win, dil) in enumerate(DIL_PATTERNS):
        vt = _values_transposed(u3, A_V + g * WIDTH_A, WIDTH_A, dil)
        if dil == 1:
            qk, q_off, k_off = u3, g * WIDTH_A, A_K + g * WIDTH_A
        else:
            qk = jnp.concatenate([u3[:, :, g * WIDTH_A:(g + 1) * WIDTH_A],
                                  u3[:, :, A_K + g * WIDTH_A:A_K + (g + 1) * WIDTH_A]], axis=2)
            q_off, k_off = 0, WIDTH_A
        o, lse = _band_attention(qk, vt, q_off=q_off, k_off=k_off,
                                 nh=2, gqa=1, n_slabs=HEADS_A // 2, kw=win // dil, dil=dil, want_lse=True)
        outs.append(o.reshape(B * S, WIDTH_A))
        lses.append(lse.reshape(B * S, WIDTH_A))
        w = min(win, S)
        kg = u3[:, S - w:, A_K + g * WIDTH_A: A_K + (g + 1) * WIDTH_A]
        vg = u3[:, S - w:, A_V + g * WIDTH_A: A_V + (g + 1) * WIDTH_A]
        states.append(jnp.stack([kg, vg], axis=2).reshape(B, w, 2, HEADS_A, HEAD_DIM))
    y = _merge_a(outs, lses, u, A_Z, x.reshape(B * S, D), w_out_bf16)
    return y.reshape(B, S, D), states


def _layer_a_sample(x, caches_t, layer, norm_g, w_bf16, q_gain, k_gain, w_out_bf16, tabs):
    N, T, D = x.shape
    u = _project(x.reshape(N * T, D), norm_g, w_bf16, A_KINDS, tabs[0], tabs[1], q_gain, k_gain)
    us = _pad_rows(u, N, T)
    outs, lses, new_rows = [], [], []
    for g, (win, dil) in enumerate(DIL_PATTERNS):
        o, lse = _cache_attention(us, caches_t[g], layer, q_off=g * WIDTH_A, k_off=A_K + g * WIDTH_A,
                                  v_off=A_V + g * WIDTH_A, nh=2, gqa=1, n_slabs=HEADS_A // 2,
                                  dil=dil, win=win, nt=T, want_lse=True)
        outs.append(o[:, :T].reshape(N * T, WIDTH_A))
        lses.append(lse[:, :T].reshape(N * T, WIDTH_A))
        new_rows.append(jnp.concatenate([us[:, :, A_K + g * WIDTH_A: A_K + (g + 1) * WIDTH_A],
                                         us[:, :, A_V + g * WIDTH_A: A_V + (g + 1) * WIDTH_A]], axis=2))
    y = _merge_a(outs, lses, u, A_Z, x.reshape(N * T, D), w_out_bf16)
    return y.reshape(N, T, D), new_rows


def _layer_b_prompt(x, norm_g, w_bf16, q_gain, k_gain, cw, w_out_bf16, tabs):
    B, S, D = x.shape
    u = _project(x.reshape(B * S, D), norm_g, w_bf16, B_KINDS, tabs[0], tabs[1], q_gain, k_gain)
    u3 = u.reshape(B, S, -1)
    kvc, kvc_t = _compress_prompt(u3, B_KV, cw)
    o_c, sel_t, _ = _cmp_attention(u3, B_QC, kvc, u3, B_GATE, tq=min(512, S), pos_base=0, n_keys=S,
                                   kvc_t=kvc_t)
    o_s = _sel_prompt(u3, _values_transposed(u3, B_KV + 3 * KVW_B, KVW_B, 1), sel_t, q_off=B_QR,
                      k_off=B_KV + 2 * KVW_B, g_off=B_GATE + 2 * LANES)
    o_w = _band_attention(u3, _values_transposed(u3, B_WIN + KVW_B, KVW_B, 1), q_off=B_QR, k_off=B_WIN,
                          nh=2 * GQA_B, gqa=GQA_B, n_slabs=2, kw=WIN_B, dil=1, want_lse=False, gates=u3,
                          g_off=B_GATE + 4 * LANES, tq=128, tk=256)
    y = _merge_b(o_c.reshape(B * S, -1), o_s.reshape(B * S, -1), o_w.reshape(B * S, -1), u, B_Z,
                 x.reshape(B * S, D), w_out_bf16)
    rows = u3[:, :, B_KV:B_KV + 4 * KVW_B].reshape(B, S, 4, KV_HEADS_B, HEAD_DIM)
    w = min(WIN_B, S)
    wrows = u3[:, S - w:, B_WIN:B_WIN + 2 * KVW_B].reshape(B, w, 2, KV_HEADS_B, HEAD_DIM)
    return y.reshape(B, S, D), rows, wrows


def _layer_b_sample(x, pool_t, page_table, win_t, layer, norm_g, w_bf16, q_gain, k_gain, cw, w_out_bf16, tabs):
    N, T, D = x.shape
    past_len = page_table.shape[1] * PAGE_SIZE
    assert past_len % SEL_BLOCK == 0 and T <= SEL_BLOCK and T <= SAMPLE_ROWS
    u = _project(x.reshape(N * T, D), norm_g, w_bf16, B_KINDS, tabs[0], tabs[1], q_gain, k_gain)
    us = _pad_rows(u, N, T)
    u3 = u.reshape(N, T, -1)
    kvc = _compress_sample(pool_t, page_table, cw)
    o_c, _, idx = _cmp_attention(us, B_QC, kvc, us, B_GATE, tq=SAMPLE_ROWS, pos_base=past_len,
                                 n_keys=past_len + T)
    o_s = _sel_sample(us, idx, pool_t, page_table, nt=T, q_off=B_QR, k_off=B_KV + 2 * KVW_B,
                      v_off=B_KV + 3 * KVW_B, g_off=B_GATE + 2 * LANES)
    o_w = _cache_attention(us, win_t, layer, q_off=B_QR, k_off=B_WIN, v_off=B_WIN + KVW_B,
                           nh=2 * GQA_B, gqa=GQA_B, n_slabs=2, dil=1, win=WIN_B, nt=T, want_lse=False,
                           g_off=B_GATE + 4 * LANES)
    y = _merge_b(o_c[:, :T].reshape(N * T, -1), o_s[:, :T].reshape(N * T, -1), o_w[:, :T].reshape(N * T, -1),
                 u, B_Z, x.reshape(N * T, D), w_out_bf16)
    rows = u3[:, :, B_KV:B_KV + 4 * KVW_B].reshape(N, T, 4, KV_HEADS_B, HEAD_DIM)
    return y.reshape(N, T, D), rows, us[:, :, B_WIN:B_WIN + 2 * KVW_B]


def _append_kernel(c_ref, n_ref, o_ref, *, L, nt, fb):
    rolled = pltpu.roll(c_ref[...], L - nt, axis=1)
    pad = jnp.zeros((LANES - SAMPLE_ROWS, LANES), F32)
    new = n_ref[...]
    new_t = jnp.concatenate(
        [jnp.concatenate([new[:, c * LANES:(c + 1) * LANES], pad], axis=0).T for c in range(fb // LANES)], axis=0)
    new_t = pltpu.roll(new_t, LANES - nt, axis=1)
    lane = lax.broadcasted_iota(jnp.int32, (fb, LANES), 1)
    if L > LANES:
        o_ref[:, :L - LANES] = rolled[:, :L - LANES]
    o_ref[:, L - LANES:] = jnp.where(lane >= LANES - nt, new_t, rolled[:, L - LANES:])


def _append_cache(cache, cache_t, new_rows, nt, win):
    J, N, F, L = cache_t.shape
    tail = cache.shape[3:]
    if L + nt <= win or L % LANES:
        new = new_rows[:, :, :nt].reshape((J, N, nt) + tail)
        return jnp.concatenate([cache, new], axis=2)[:, :, -min(win, L + nt):]
    assert L == win
    fb = min(F, LANES * max(1, 2048 // L))
    assert F % fb == 0
    out_t = pl.pallas_call(
        functools.partial(_append_kernel, L=L, nt=nt, fb=fb),
        grid=(J, N, F // fb),
        in_specs=[pl.BlockSpec((None, None, fb, L), lambda j, n, f: (j, n, f, 0)),
                  pl.BlockSpec((None, None, SAMPLE_ROWS, fb), lambda j, n, f: (j, n, 0, f))],
        out_specs=pl.BlockSpec((None, None, fb, L), lambda j, n, f: (j, n, f, 0)),
        out_shape=jax.ShapeDtypeStruct((J, N, F, L), F32),
        compiler_params=_cparams(("parallel", "parallel", "parallel")),
    )(cache_t, new_rows)
    nd = len(tail)
    out = out_t.reshape((J, N) + tail + (L,))
    return jnp.transpose(out, (0, 1, nd + 2) + tuple(range(2, nd + 2)))


def kernel(x_prompt, x_sample, cache_dil_0, cache_dil_1, cache_dil_2, cache_nsa_paged, cache_nsa_win,
           page_table, a_norm, a_w_in, a_q_norm, a_k_norm, a_w_out, b_norm, b_w_in, b_q_norm, b_k_norm,
           b_cmp_pe, b_cmp_w1, b_cmp_w2, b_w_out):
    dil_caches = (cache_dil_0, cache_dil_1, cache_dil_2)
    B, S, _ = x_prompt.shape
    N, T, _ = x_sample.shape
    past_len = page_table.shape[1] * PAGE_SIZE
    depth = a_norm.shape[0] + b_norm.shape[0]
    n_pool = cache_nsa_paged.shape[1]
    tabs_p = _rope_tables(jnp.tile(jnp.arange(S), B))
    tabs_s = _rope_tables(jnp.tile(past_len + jnp.arange(T), N))
    dil_t = [_feature_major(c) for c in dil_caches]
    win_t = _feature_major(cache_nsa_win)
    pool_t = _feature_major(cache_nsa_paged)
    pool_t = pool_t.reshape(-1, 4, KVW_B, PAGE_SIZE)
    xp, xs = x_prompt, x_sample
    dil_p = [[] for _ in DIL_PATTERNS]
    dil_new = [[] for _ in DIL_PATTERNS]
    rows_p, rows_s, win_p, win_new = [], [], [], []
    for layer in range(depth):
        j = layer // 2
        if layer % 2 == 0:
            w = a_w_in[j].astype(BF16)
            wo = a_w_out[j].astype(BF16)
            xp, st_p = _layer_a_prompt(xp, a_norm[j], w, a_q_norm[j], a_k_norm[j], wo, tabs_p)
            xs, new_s = _layer_a_sample(xs, dil_t, j, a_norm[j], w, a_q_norm[j], a_k_norm[j], wo, tabs_s)
            for g in range(N_GROUPS_A):
                dil_p[g].append(st_p[g])
                dil_new[g].append(new_s[g])
        else:
            w = _b_weight(b_w_in[j])
            wo = b_w_out[j].astype(BF16)
            cw = _compress_weights(b_cmp_pe[j], b_cmp_w1[j], b_cmp_w2[j], b_k_norm[j])
            xp, rp, wp = _layer_b_prompt(xp, b_norm[j], w, b_q_norm[j], b_k_norm[j], cw, wo, tabs_p)
            xs, rs, wn = _layer_b_sample(xs, pool_t, page_table + j * n_pool, win_t, j, b_norm[j], w,
                                         b_q_norm[j], b_k_norm[j], cw, wo, tabs_s)
            rows_p.append(rp)
            win_p.append(wp)
            rows_s.append(rs)
            win_new.append(wn)
    dil_s = [_append_cache(dil_caches[g], dil_t[g], jnp.stack(dil_new[g]), T, DIL_PATTERNS[g][0])
             for g in range(N_GROUPS_A)]
    win_s = _append_cache(cache_nsa_win, win_t, jnp.stack(win_new), T, WIN_B)
    return (xp, xs,
            jnp.stack(dil_p[0]), jnp.stack(dil_p[1]), jnp.stack(dil_p[2]),
            jnp.stack(rows_p), jnp.stack(win_p),
            dil_s[0], dil_s[1], dil_s[2],
            jnp.stack(rows_s), win_s)
```

```python
import functools

import jax
import jax.numpy as jnp
from jax import lax
from jax.experimental import pallas as pl
from jax.experimental.pallas import tpu as pltpu

F32 = jnp.float32
BF16 = jnp.bfloat16

HEAD_DIM = 64
SCALE = HEAD_DIM ** -0.5
ROPE_THETA = 10000.0
EPS = 1e-6
TINY = 1e-30
NEG = -1e30
DIL_PATTERNS = ((128, 1), (512, 4), (2048, 16))
N_GROUPS_A = len(DIL_PATTERNS)
HEADS_A = 8
WIDTH_A = HEADS_A * HEAD_DIM
QKV_A = 3 * N_GROUPS_A * WIDTH_A
KV_HEADS_B = 4
GQA_B = 4
HEADS_B = KV_HEADS_B * GQA_B
WIDTH_B = HEADS_B * HEAD_DIM
KVW_B = KV_HEADS_B * HEAD_DIM
CMP_LEN = 32
CMP_STRIDE = 16
CMP_RATIO = CMP_LEN // CMP_STRIDE
CMP_HIDDEN = 128
SEL_BLOCK = 64
SEL_TOPN = 16
WIN_B = 512
PAGE_SIZE = 128

LANES = 128
PROJ_TN = 256
PROJ_SUB = 2
VMEM_LIMIT = 56 * 1024 * 1024
IDX_ROWS = 24
SAMPLE_ROWS = 8

B_QC, B_QR, B_Z, B_KV, B_WIN, B_GATE = 0, 1024, 2048, 3072, 4096, 4608
B_COLS = B_GATE + 6 * LANES

EP_NONE, EP_NORM_Q, EP_ROPE_Q, EP_NORM_K, EP_ROPE_K, EP_SIG = range(6)


def _cparams(sem):
    return pltpu.CompilerParams(dimension_semantics=sem, vmem_limit_bytes=VMEM_LIMIT)


def _nt_dot(a, b):
    return lax.dot_general(a, b, (((1,), (1,)), ((), ())), preferred_element_type=F32)


def _dot(a, b):
    return jnp.dot(a, b, preferred_element_type=F32)


def _head_norm(acc, bd_ref, gain):
    msq = _dot((acc * acc).astype(BF16), bd_ref[...])
    return acc * lax.rsqrt(msq + EPS) * gain


def _rope_tile(y, cos, sin_signed):
    lane = lax.broadcasted_iota(jnp.int32, (y.shape[0], LANES), 1)
    first_half = (lane & (HEAD_DIM - 1)) < (HEAD_DIM // 2)
    outs = []
    for c in range(y.shape[1] // LANES):
        yc = y[:, c * LANES:(c + 1) * LANES]
        partner = jnp.where(first_half, pltpu.roll(yc, LANES - HEAD_DIM // 2, axis=1),
                            pltpu.roll(yc, HEAD_DIM // 2, axis=1))
        outs.append(yc * cos + partner * sin_signed)
    return jnp.concatenate(outs, axis=1)


def _proj_kernel(x_ref, g_ref, w_ref, cos_ref, sin_ref, qg_ref, kg_ref, bd_ref, o_ref, h_ref, *, kinds):
    j = pl.program_id(1)

    @pl.when(j == 0)
    def _():
        x = x_ref[...]
        r = lax.rsqrt(jnp.mean(x * x, axis=-1, keepdims=True) + EPS)
        h_ref[...] = (x * r * g_ref[...]).astype(BF16)

    acc_all = _dot(h_ref[...], w_ref[...])

    def ranges(kind, half):
        out, start = [], None
        for t, k in enumerate(list(kinds[half::PROJ_SUB]) + [None]):
            if k == kind and start is None:
                start = t
            if k != kind and start is not None:
                out.append((start, t))
                start = None
        return out

    for half in range(PROJ_SUB):
        acc = acc_all[:, half * PROJ_TN:(half + 1) * PROJ_TN]
        cols = slice(half * PROJ_TN, (half + 1) * PROJ_TN)

        def emit(kind, fn, half=half, cols=cols):
            rs = ranges(kind, half)
            if not rs:
                return
            cond = None
            for a, b in rs:
                c = (j >= a) & (j < b)
                cond = c if cond is None else (cond | c)

            @pl.when(cond)
            def _():
                o_ref[:, cols] = fn()

        emit(EP_NONE, lambda acc=acc: acc)
        emit(EP_SIG, lambda acc=acc: jax.nn.sigmoid(acc))
        emit(EP_NORM_Q, lambda acc=acc: _head_norm(acc, bd_ref, qg_ref[...]))
        emit(EP_NORM_K, lambda acc=acc: _head_norm(acc, bd_ref, kg_ref[...]))
        emit(EP_ROPE_Q, lambda acc=acc: _rope_tile(_head_norm(acc, bd_ref, qg_ref[...]), cos_ref[...], sin_ref[...]))
        emit(EP_ROPE_K, lambda acc=acc: _rope_tile(_head_norm(acc, bd_ref, kg_ref[...]), cos_ref[...], sin_ref[...]))


def _project(x, norm_g, w_bf16, kinds, cos_t, sin_t, q_gain, k_gain):
    T, D = x.shape
    E = w_bf16.shape[1]
    tn = PROJ_TN
    assert E == len(kinds) * tn and len(kinds) % PROJ_SUB == 0
    tm = min(T, 2048)
    assert T % tm == 0
    rep = tn // HEAD_DIM
    qg = jnp.tile(q_gain.astype(F32), rep)[None]
    kg = jnp.tile(k_gain.astype(F32), rep)[None]
    hid = jnp.arange(tn) // HEAD_DIM
    bd = ((hid[:, None] == hid[None, :]).astype(F32) / HEAD_DIM).astype(BF16)
    return pl.pallas_call(
        functools.partial(_proj_kernel, kinds=tuple(kinds)),
        grid=(T // tm, E // (PROJ_SUB * tn)),
        in_specs=[
            pl.BlockSpec((tm, D), lambda i, j: (i, 0)),
            pl.BlockSpec((1, D), lambda i, j: (0, 0)),
            pl.BlockSpec((D, PROJ_SUB * tn), lambda i, j: (0, j)),
            pl.BlockSpec((tm, LANES), lambda i, j: (i, 0)),
            pl.BlockSpec((tm, LANES), lambda i, j: (i, 0)),
            pl.BlockSpec((1, tn), lambda i, j: (0, 0)),
            pl.BlockSpec((1, tn), lambda i, j: (0, 0)),
            pl.BlockSpec((tn, tn), lambda i, j: (0, 0)),
        ],
        out_specs=pl.BlockSpec((tm, PROJ_SUB * tn), lambda i, j: (i, j)),
        out_shape=jax.ShapeDtypeStruct((T, E), F32),
        scratch_shapes=[pltpu.VMEM((tm, D), BF16)],
        compiler_params=_cparams(("parallel", "arbitrary")),
    )(x, norm_g.astype(F32)[None], w_bf16, cos_t, sin_t, qg, kg, bd)


def _rope_tables(pos):
    half = HEAD_DIM // 2
    inv_freq = ROPE_THETA ** (-jnp.arange(half, dtype=F32) / half)
    ang = pos.astype(F32)[:, None] * inv_freq[None, :]
    cos, sin = jnp.cos(ang), jnp.sin(ang)
    cos_t = jnp.concatenate([cos, cos, cos, cos], axis=1)
    sin_t = jnp.concatenate([-sin, sin, -sin, sin], axis=1)
    return cos_t, sin_t


def _prep_q(q, nh, gqa, tq):
    lane = lax.broadcasted_iota(jnp.int32, (tq, LANES), 1)
    low = lane < HEAD_DIM
    parts = []
    for j in range(nh):
        col = q[:, (j // 2) * LANES:(j // 2 + 1) * LANES]
        nat, tgt = j % 2, (j // gqa) % 2
        if nat != tgt:
            col = pltpu.roll(col, HEAD_DIM, axis=1)
        keep = low if tgt == 0 else jnp.logical_not(low)
        parts.append(jnp.where(keep, col * SCALE, 0.0))
    return jnp.concatenate(parts, axis=0).astype(BF16)


def _unprep_o(o, nh, gqa, tq):
    lane = lax.broadcasted_iota(jnp.int32, (tq, LANES), 1)
    low = lane < HEAD_DIM
    cols = []
    for c in range(nh // 2):
        halves = []
        for nat in (0, 1):
            j = 2 * c + nat
            tgt = (j // gqa) % 2
            oj = o[j * tq:(j + 1) * tq]
            if tgt != nat:
                oj = pltpu.roll(oj, HEAD_DIM, axis=1)
            halves.append(oj)
        cols.append(jnp.where(low, halves[0], halves[1]))
    return jnp.concatenate(cols, axis=1)


def _apply_gates(o, g_ref, nh, tq):
    if g_ref is None:
        return o
    g = g_ref[...]
    return jnp.concatenate([o[j * tq:(j + 1) * tq] * g[:, j:j + 1] for j in range(nh)], axis=0)


def _q_transposed(q, nh, gqa, tq):
    qt = (q * SCALE).T
    zero = jnp.zeros((HEAD_DIM, tq), F32)
    cols = []
    for j in range(nh):
        h = qt[j * HEAD_DIM:(j + 1) * HEAD_DIM]
        cols.append(jnp.concatenate([h, zero] if (j // gqa) % 2 == 0 else [zero, h], axis=0))
    return jnp.concatenate(cols, axis=1).astype(BF16)


def _masked_heads(s, masks, tq):
    return jnp.concatenate([jnp.where(mk, s[:, j * tq:(j + 1) * tq], NEG) for j, mk in enumerate(masks)], axis=1)


ACC_ROWS = LANES + 16


def _attend_tile_t(kt, vt_t, qts, masks, tq, m_ref, acc_ref):
    n_sl = len(qts)
    rs = qts[0].shape[1]
    s = jnp.concatenate([_dot(kt[:, sl * LANES:(sl + 1) * LANES], qts[sl]) for sl in range(n_sl)], axis=1)
    s = _masked_heads(s, masks, tq)
    m_old = m_ref[...]
    m_new = jnp.maximum(m_old, jnp.max(s, axis=0, keepdims=True))
    alpha = jnp.exp(m_old - m_new)
    p = jnp.exp((s - m_new).astype(BF16))
    ones = jnp.ones((ACC_ROWS - LANES, kt.shape[0]), BF16)
    for sl in range(n_sl):
        cols = slice(sl * rs, (sl + 1) * rs)
        vt1 = jnp.concatenate([vt_t[sl * LANES:(sl + 1) * LANES], ones], axis=0)
        acc_ref[sl] = alpha[:, cols] * acc_ref[sl] + _dot(vt1, p[:, cols])
    m_ref[...] = m_new


def _init_stats(m_ref, acc_ref):
    m_ref[...] = jnp.full(m_ref.shape, NEG, F32)
    acc_ref[...] = jnp.zeros(acc_ref.shape, F32)


def _finish_t(o_ref, lse_ref, g_ref, m_ref, acc_ref, nh, gqa, tq):
    n_sl = acc_ref.shape[0]
    g_t = g_ref[...].T if g_ref is not None else None
    parts, lse_rows = [], []
    for sl in range(n_sl):
        l = acc_ref[sl, LANES:LANES + 1, :]
        inv = 1.0 / l
        for j in range(nh):
            half = (j // gqa) % 2
            sc = inv[:, j * tq:(j + 1) * tq]
            if g_t is not None:
                sc = sc * g_t[sl * LANES + j:sl * LANES + j + 1, :]
            parts.append(acc_ref[sl, half * HEAD_DIM:(half + 1) * HEAD_DIM, j * tq:(j + 1) * tq] * sc)
        if lse_ref is not None:
            lse = m_ref[:, sl * nh * tq:(sl + 1) * nh * tq] + jnp.log(l)
            lse_rows += [jnp.broadcast_to(lse[:, j * tq:(j + 1) * tq], (HEAD_DIM, tq)) for j in range(nh)]
    o_ref[...] = jnp.concatenate(parts, axis=0).T
    if lse_ref is not None:
        lse_ref[...] = jnp.concatenate(lse_rows, axis=0).T


def _slab_queries(q_ref, n_sl, nh, gqa, tq):
    qs = nh * HEAD_DIM
    return [_q_transposed(q_ref[:, sl * qs:(sl + 1) * qs], nh, gqa, tq) for sl in range(n_sl)]


def _band_kernel(*refs, n_sl, nh, gqa, tq, tk, kw, want_lse, gated):
    q_ref, k_ref, vt_ref = refs[:3]
    pos = 3
    g_ref = None
    if gated:
        g_ref = refs[pos]
        pos += 1
    o_ref = refs[pos]
    pos += 1
    lse_ref = None
    if want_lse:
        lse_ref = refs[pos]
        pos += 1
    m_ref, acc_ref = refs[pos:pos + 2]

    i = pl.program_id(2)
    q0 = i * tq
    qts = _slab_queries(q_ref, n_sl, nh, gqa, tq)
    _init_stats(m_ref, acc_ref)
    rel = lax.broadcasted_iota(jnp.int32, (tk, tq), 1) - lax.broadcasted_iota(jnp.int32, (tk, tq), 0)

    def body(jj, carry):
        ks = pl.multiple_of(jj * tk, tk)
        kt = k_ref[pl.ds(ks, tk), :].astype(BF16)
        vt_t = vt_ref[:, pl.ds(ks, tk)].astype(BF16)
        dist = rel + (q0 - ks)
        _attend_tile_t(kt, vt_t, qts, [(dist >= 0) & (dist <= kw)] * (n_sl * nh), tq, m_ref, acc_ref)
        return carry

    lo_t = jnp.maximum(q0 - kw, 0) // tk
    hi_t = (q0 + tq - 1) // tk
    lax.fori_loop(lo_t, hi_t + 1, body, 0)
    _finish_t(o_ref, lse_ref, g_ref, m_ref, acc_ref, nh, gqa, tq)


def _band_attention(u, vt, *, uk=None, q_off, k_off, nh, gqa, n_slabs, kw, dil, want_lse,
                    gates=None, g_off=0, tq=256, tk=256):
    B, S, W = u.shape
    uk = u if uk is None else uk
    Wk = uk.shape[2]
    n = S // dil
    tq = min(tq, n)
    tk = min(tk, n)
    uv = u.reshape(B, n, dil * W)
    ukv = uk.reshape(B, n, dil * Wk)
    wo = n_slabs * nh * HEAD_DIM
    kwid = n_slabs * LANES
    assert q_off % wo == 0 and k_off % kwid == 0 and (dil == 1 or (W % wo == 0 and Wk % kwid == 0))
    in_specs = [
        pl.BlockSpec((None, tq, wo), lambda b, r, i: (b, i, r * (W // wo) + q_off // wo)),
        pl.BlockSpec((None, n, kwid), lambda b, r, i: (b, 0, r * (Wk // kwid) + k_off // kwid)),
        pl.BlockSpec((None, None, kwid, n), lambda b, r, i: (b, r, 0, 0)),
    ]
    args = [uv, ukv, vt]
    if gates is not None:
        Wg = gates.shape[-1]
        assert g_off % kwid == 0 and (dil == 1 or Wg % kwid == 0)
        in_specs.append(pl.BlockSpec((None, tq, kwid), lambda b, r, i: (b, i, r * (Wg // kwid) + g_off // kwid)))
        args.append(gates.reshape(B, n, dil * Wg))
    o_spec = pl.BlockSpec((None, tq, wo), lambda b, r, i: (b, i, r))
    o_shape = jax.ShapeDtypeStruct((B, n, dil * wo), F32)
    R = nh * tq
    res = pl.pallas_call(
        functools.partial(_band_kernel, n_sl=n_slabs, nh=nh, gqa=gqa, tq=tq, tk=tk, kw=kw, want_lse=want_lse,
                          gated=gates is not None),
        grid=(B, dil, n // tq),
        in_specs=in_specs,
        out_specs=[o_spec, o_spec] if want_lse else o_spec,
        out_shape=[o_shape, o_shape] if want_lse else o_shape,
        scratch_shapes=[pltpu.VMEM((1, n_slabs * R), F32), pltpu.VMEM((n_slabs, ACC_ROWS, R), F32)],
        compiler_params=_cparams(("parallel", "parallel", "arbitrary")),
    )(*args)
    if want_lse:
        return res[0].reshape(B, S, wo), res[1].reshape(B, S, wo)
    return res.reshape(B, S, wo)


def _values_transposed(u, v_off, width, dil):
    B, S, _ = u.shape
    v = u[:, :, v_off:v_off + width].reshape(B, S // dil, dil, width)
    return jnp.transpose(v, (0, 2, 3, 1))


def _compress_math(x_ref, w1_ref, pe_ref, w1f_ref, w2_ref, bd_ref, kg_ref, is_key, n_chunk):
    hid0 = _dot(pe_ref[...], w1f_ref[...])[0:1]
    low = lax.broadcasted_iota(jnp.int32, (n_chunk, LANES), 1) < HEAD_DIM
    accs = [jnp.zeros((n_chunk, 2 * CMP_HIDDEN), F32) for _ in range(2)]
    for jp in range(CMP_STRIDE // 2):
        x0 = x_ref[pl.ds(2 * jp, n_chunk, stride=CMP_STRIDE), :]
        x1 = x_ref[pl.ds(2 * jp + 1, n_chunk, stride=CMP_STRIDE), :]
        pair = [jnp.where(low, x0, pltpu.roll(x1, HEAD_DIM, axis=1)),
                jnp.where(low, pltpu.roll(x0, HEAD_DIM, axis=1), x1)]
        for a in range(2):
            accs[a] = accs[a] + _dot(pair[a].astype(BF16), w1_ref[jp])
    out = jnp.zeros((n_chunk, LANES), F32)
    for a in range(2):
        p0 = accs[a][:, :CMP_HIDDEN]
        p1 = accs[a][:, CMP_HIDDEN:]
        hid = hid0 + p0 + pltpu.roll(p1, n_chunk - 1, axis=0)
        out = out + _dot(jax.nn.silu(hid).astype(BF16), w2_ref[a])
    normed = _head_norm(out, bd_ref, kg_ref[...])
    return jnp.where(is_key, normed, out)


def _compress_prompt_kernel(x_ref, w1_ref, pe_ref, w1f_ref, w2_ref, bd_ref, kg_ref, o_ref, ot_ref, *, n_chunk):
    is_key = pl.program_id(1) == 0
    out = _compress_math(x_ref, w1_ref, pe_ref, w1f_ref, w2_ref, bd_ref, kg_ref, is_key, n_chunk)
    o_ref[...] = out
    ot_ref[...] = out.T


def _compress_weights(pe, w1, w2, k_gain):
    w1r = w1.reshape(2, CMP_RATIO, CMP_STRIDE, HEAD_DIM, CMP_HIDDEN)
    wj = jnp.concatenate([w1r[:, 0], w1r[:, 1]], axis=-1)
    w1bd = wj.reshape(2, CMP_STRIDE // 2, 2 * HEAD_DIM, 2 * CMP_HIDDEN).astype(BF16)
    pe8 = jnp.concatenate([pe.reshape(2, 1, CMP_LEN * HEAD_DIM),
                           jnp.zeros((2, 7, CMP_LEN * HEAD_DIM), F32)], axis=1).astype(BF16)
    w1f = w1.reshape(2, CMP_LEN * HEAD_DIM, CMP_HIDDEN).astype(BF16)
    z2 = jnp.zeros_like(w2)
    w2pad = jnp.stack([jnp.concatenate([w2, z2], axis=-1),
                       jnp.concatenate([z2, w2], axis=-1)], axis=1).astype(BF16)
    hid = jnp.arange(LANES) // HEAD_DIM
    bd = ((hid[:, None] == hid[None, :]).astype(F32) / HEAD_DIM).astype(BF16)
    kg = jnp.tile(k_gain.astype(F32), 2)[None]
    return w1bd, pe8, w1f, w2pad, bd, kg


def _cw_specs(nd):
    def sp(shape, fn):
        return pl.BlockSpec(shape, fn)
    if nd == 3:
        return [
            sp((None, CMP_STRIDE // 2, LANES, 2 * CMP_HIDDEN), lambda b, t, p: (t, 0, 0, 0)),
            sp((None, 8, CMP_LEN * HEAD_DIM), lambda b, t, p: (t, 0, 0)),
            sp((None, CMP_LEN * HEAD_DIM, CMP_HIDDEN), lambda b, t, p: (t, 0, 0)),
            sp((None, 2, CMP_HIDDEN, LANES), lambda b, t, p: (t, 0, 0, 0)),
            sp((LANES, LANES), lambda b, t, p: (0, 0)),
            sp((1, LANES), lambda b, t, p: (0, 0)),
        ]
    return [
        sp((None, CMP_STRIDE // 2, LANES, 2 * CMP_HIDDEN), lambda b, t, p, *_: (t, 0, 0, 0)),
        sp((None, 8, CMP_LEN * HEAD_DIM), lambda b, t, p, *_: (t, 0, 0)),
        sp((None, CMP_LEN * HEAD_DIM, CMP_HIDDEN), lambda b, t, p, *_: (t, 0, 0)),
        sp((None, 2, CMP_HIDDEN, LANES), lambda b, t, p, *_: (t, 0, 0, 0)),
        sp((LANES, LANES), lambda b, t, p, *_: (0, 0)),
        sp((1, LANES), lambda b, t, p, *_: (0, 0)),
    ]


def _compress_prompt(u, col0, cw):
    B, S, W = u.shape
    n_chunk = S // CMP_STRIDE
    base = col0 // LANES
    return pl.pallas_call(
        functools.partial(_compress_prompt_kernel, n_chunk=n_chunk),
        grid=(B, 2, 2),
        in_specs=[pl.BlockSpec((None, S, LANES), lambda b, t, p: (b, 0, base + 2 * t + p))] + _cw_specs(3),
        out_specs=[pl.BlockSpec((None, None, n_chunk, LANES), lambda b, t, p: (b, t, 0, p)),
                   pl.BlockSpec((None, None, LANES, n_chunk), lambda b, t, p: (b, t, p, 0))],
        out_shape=[jax.ShapeDtypeStruct((B, 2, n_chunk, KVW_B), F32),
                   jax.ShapeDtypeStruct((B, 2, KVW_B, n_chunk), F32)],
        compiler_params=_cparams(("parallel", "parallel", "parallel")),
    )(u, *cw)


def _compress_sample_kernel(pt_ref, pool_ref, w1_ref, pe_ref, w1f_ref, w2_ref, bd_ref, kg_ref, o_ref,
                            raw, xbuf, sem, *, n_pages, n_chunk):
    n, t, p = pl.program_id(0), pl.program_id(1), pl.program_id(2)
    step = (n * 2 + t) * 2 + p
    n_steps = pl.num_programs(0) * 4

    def page_copy(page, rt, pair, slot, pg):
        row0 = pl.multiple_of(pair * LANES, LANES)
        return pltpu.make_async_copy(pool_ref.at[page, rt, pl.ds(row0, LANES), :], raw.at[slot, pg], sem.at[slot])

    def issue(st, slot):
        seq, rt, pair = st // 4, (st // 2) % 2, st % 2

        def go(pg, c):
            page_copy(pt_ref[seq * n_pages + pg], rt, pair, slot, pg).start()
            return c
        lax.fori_loop(0, n_pages, go, 0, unroll=8)

    @pl.when(step == 0)
    def _():
        issue(step, 0)

    @pl.when(step + 1 < n_steps)
    def _():
        issue(step + 1, (step + 1) % 2)

    slot = step % 2

    def wait(pg, c):
        page_copy(0, t, p, slot, pg).wait()
        return c
    lax.fori_loop(0, n_pages, wait, 0, unroll=8)

    def to_token_major(pg, c):
        xbuf[pl.ds(pl.multiple_of(pg * PAGE_SIZE, PAGE_SIZE), PAGE_SIZE), :] = raw[slot, pg].T
        return c
    lax.fori_loop(0, n_pages, to_token_major, 0, unroll=8)

    o_ref[...] = _compress_math(xbuf, w1_ref, pe_ref, w1f_ref, w2_ref, bd_ref, kg_ref, t == 0, n_chunk)


def _compress_sample(pool, page_table, cw):
    N, n_pages = page_table.shape
    n_chunk = n_pages * PAGE_SIZE // CMP_STRIDE
    gs = pltpu.PrefetchScalarGridSpec(
        num_scalar_prefetch=1,
        grid=(N, 2, 2),
        in_specs=[pl.BlockSpec(memory_space=pl.ANY)] + _cw_specs(4),
        out_specs=pl.BlockSpec((None, None, n_chunk, LANES), lambda b, t, p, *_: (b, t, 0, p)),
        scratch_shapes=[pltpu.VMEM((2, n_pages, LANES, PAGE_SIZE), F32),
                        pltpu.VMEM((n_pages * PAGE_SIZE, LANES), F32), pltpu.SemaphoreType.DMA((2,))],
    )
    return pl.pallas_call(
        functools.partial(_compress_sample_kernel, n_pages=n_pages, n_chunk=n_chunk),
        grid_spec=gs,
        out_shape=jax.ShapeDtypeStruct((N, 2, n_chunk, KVW_B), F32),
        compiler_params=_cparams(("arbitrary", "arbitrary", "arbitrary")),
    )(page_table.reshape(-1), pool, *cw)


def _cmp_kernel(q_ref, kc_ref, vc_ref, cov_ref, g_ref, o_ref, sel_ref, idx_ref, *,
                tq, n_chunk, nselp, pos_base, top_n):
    nh, gqa = 2 * GQA_B, GQA_B
    i = pl.program_id(2)
    R = nh * tq
    Q = _prep_q(q_ref[...], nh, gqa, tq)
    kc = kc_ref[...].astype(BF16)
    vc = vc_ref[...].astype(BF16)
    s = _nt_dot(Q, kc)
    pos_r = pos_base + i * tq + (lax.broadcasted_iota(jnp.int32, (R, n_chunk), 0) & (tq - 1))
    cend = lax.broadcasted_iota(jnp.int32, (R, n_chunk), 1) * CMP_STRIDE + (CMP_LEN - 1)
    s = jnp.where(cend <= pos_r, s, -jnp.inf)
    m = jnp.max(s, axis=-1, keepdims=True)
    m = jnp.where(m > -jnp.inf, m, 0.0)
    e = jnp.exp(s - m)
    den = jnp.sum(e, axis=-1, keepdims=True)
    p = e / jnp.maximum(den, TINY)
    o = _dot(p.astype(BF16), vc)
    o = _apply_gates(o, g_ref, nh, tq)
    o_ref[...] = _unprep_o(o, nh, gqa, tq)

    imps = []
    for a in range(2):
        ps = p[(a * gqa) * tq:(a * gqa + 1) * tq]
        for g in range(1, gqa):
            ps = ps + p[(a * gqa + g) * tq:(a * gqa + g + 1) * tq]
        hi = ps.astype(BF16)
        lo = (ps - hi.astype(F32)).astype(BF16)
        imps.append(_nt_dot(cov_ref[...], hi) + _nt_dot(cov_ref[...], lo))
    _select_blocks(imps, sel_ref, idx_ref, pos_base + i * tq, tq, nselp, top_n)


def _select_blocks(imps, sel_ref, idx_ref, pos0, tq, nselp, top_n):
    blk = lax.broadcasted_iota(jnp.int32, (nselp, tq), 0)
    cur = (pos0 + lax.broadcasted_iota(jnp.int32, (1, tq), 1)) // SEL_BLOCK
    valid = blk <= cur
    forced = (blk == 0) | (blk == cur) | (blk == cur - 1)
    n_forced = jnp.sum(forced.astype(jnp.int32), axis=0, keepdims=True)
    zero_row = jnp.zeros((1, tq), jnp.int32)
    for a, imp in enumerate(imps):
        rem = jnp.where(valid & jnp.logical_not(forced), imp, -1.0)
        sel = forced
        idx_ref[a, pl.ds(0, 1), :] = zero_row
        idx_ref[a, pl.ds(1, 1), :] = cur
        idx_ref[a, pl.ds(2, 1), :] = jnp.maximum(cur - 1, 0)
        for it in range(top_n - 1):
            mx = jnp.max(rem, axis=0, keepdims=True)
            first = jnp.min(jnp.where(rem == mx, blk, nselp), axis=0, keepdims=True)
            active = (it < top_n - n_forced) & (mx >= 0.0)
            pick = (blk == first) & active
            sel = sel | pick
            rem = jnp.where(pick, -1.0, rem)
            idx_ref[a, pl.ds(3 + it, 1), :] = jnp.where(active, first, 0)
        for r in range(3 + top_n - 1, IDX_ROWS):
            idx_ref[a, pl.ds(r, 1), :] = zero_row
        sel_ref[a] = sel.astype(F32)


def _cmp_t_kernel(q_ref, kc_ref, vct_ref, cov_ref, g_ref, o_ref, sel_ref, idx_ref, *,
                  tq, n_chunk, nselp, pos_base, top_n):
    nh, gqa = 2 * GQA_B, GQA_B
    i = pl.program_id(2)
    qt = _q_transposed(q_ref[...], nh, gqa, tq)
    s = _dot(kc_ref[...].astype(BF16), qt)
    pos_l = pos_base + i * tq + lax.broadcasted_iota(jnp.int32, (n_chunk, tq), 1)
    cend = lax.broadcasted_iota(jnp.int32, (n_chunk, tq), 0) * CMP_STRIDE + (CMP_LEN - 1)
    ok = cend <= pos_l
    s = jnp.concatenate([jnp.where(ok, s[:, j * tq:(j + 1) * tq], -jnp.inf) for j in range(nh)], axis=1)
    m = jnp.max(s, axis=0, keepdims=True)
    m = jnp.where(m > -jnp.inf, m, 0.0)
    e = jnp.exp(s - m)
    den = jnp.sum(e, axis=0, keepdims=True)
    p = e / jnp.maximum(den, TINY)
    acc = _dot(vct_ref[...].astype(BF16), p.astype(BF16))
    g_t = g_ref[...].T
    parts = []
    for j in range(nh):
        half = (j // gqa) % 2
        parts.append(acc[half * HEAD_DIM:(half + 1) * HEAD_DIM, j * tq:(j + 1) * tq] * g_t[j:j + 1, :])
    o_ref[...] = jnp.concatenate(parts, axis=0).T

    imps = []
    for a in range(2):
        ps = p[:, (a * gqa) * tq:(a * gqa + 1) * tq]
        for g in range(1, gqa):
            ps = ps + p[:, (a * gqa + g) * tq:(a * gqa + g + 1) * tq]
        hi = ps.astype(BF16)
        lo = (ps - hi.astype(F32)).astype(BF16)
        imps.append(_dot(cov_ref[...], hi) + _dot(cov_ref[...], lo))
    _select_blocks(imps, sel_ref, idx_ref, pos_base + i * tq, tq, nselp, top_n)


def _cover_t(n_chunk, n_cmp, n_sel, nselp):
    c = jnp.arange(n_chunk)[None, :]
    j = jnp.arange(nselp)[:, None]
    cov = ((c * CMP_STRIDE <= j * SEL_BLOCK + SEL_BLOCK - 1) & (c * CMP_STRIDE + CMP_LEN - 1 >= j * SEL_BLOCK)
           & (c < n_cmp) & (j < n_sel))
    return cov.astype(BF16)


def _cmp_attention(q, q_off, kvc, gates, g_off, *, tq, pos_base, n_keys, kvc_t=None):
    N, Tq, W = q.shape
    n_chunk = kvc.shape[2]
    n_cmp = (n_keys - CMP_LEN) // CMP_STRIDE + 1
    n_sel = -(-n_keys // SEL_BLOCK)
    nselp = -(-n_sel // 8) * 8
    top_n = min(SEL_TOPN, n_sel)
    qs = 2 * GQA_B * HEAD_DIM
    Wg = gates.shape[-1]
    cov = _cover_t(n_chunk, n_cmp, n_sel, nselp)
    if kvc_t is None:
        body, values = _cmp_kernel, kvc
        v_spec = pl.BlockSpec((None, None, n_chunk, LANES), lambda b, p, i: (b, 1, 0, p))
    else:
        body, values = _cmp_t_kernel, kvc_t
        v_spec = pl.BlockSpec((None, None, LANES, n_chunk), lambda b, p, i: (b, 1, p, 0))
    return pl.pallas_call(
        functools.partial(body, tq=tq, n_chunk=n_chunk, nselp=nselp, pos_base=pos_base, top_n=top_n),
        grid=(N, 2, Tq // tq),
        in_specs=[
            pl.BlockSpec((None, tq, qs), lambda b, p, i: (b, i, q_off // qs + p)),
            pl.BlockSpec((None, None, n_chunk, LANES), lambda b, p, i: (b, 0, 0, p)),
            v_spec,
            pl.BlockSpec((nselp, n_chunk), lambda b, p, i: (0, 0)),
            pl.BlockSpec((None, tq, LANES), lambda b, p, i: (b, i, g_off // LANES + p)),
        ],
        out_specs=[
            pl.BlockSpec((None, tq, qs), lambda b, p, i: (b, i, p)),
            pl.BlockSpec((None, 2, nselp, tq), lambda b, p, i: (b, p, 0, i)),
            pl.BlockSpec((None, 2, IDX_ROWS, tq), lambda b, p, i: (b, p, 0, i)),
        ],
        out_shape=[
            jax.ShapeDtypeStruct((N, Tq, WIDTH_B), F32),
            jax.ShapeDtypeStruct((N, KV_HEADS_B, nselp, Tq), F32),
            jax.ShapeDtypeStruct((N, KV_HEADS_B, IDX_ROWS, Tq), jnp.int32),
        ],
        compiler_params=_cparams(("parallel", "parallel", "parallel")),
    )(q, kvc, values, cov, gates)


def _sel_prompt_kernel(q_ref, k_ref, vt_ref, sel_ref, g_ref, o_ref, m_ref, acc_ref, *, tq, tk):
    nh, gqa, n_sl = 2 * GQA_B, GQA_B, KV_HEADS_B // 2
    i = pl.program_id(1)
    q0 = i * tq
    qts = _slab_queries(q_ref, n_sl, nh, gqa, tq)
    _init_stats(m_ref, acc_ref)
    nblk = tk // SEL_BLOCK
    rel = lax.broadcasted_iota(jnp.int32, (tk, tq), 1) - lax.broadcasted_iota(jnp.int32, (tk, tq), 0)

    def body(jj, carry):
        ks = pl.multiple_of(jj * tk, tk)
        kt = k_ref[pl.ds(ks, tk), :].astype(BF16)
        vt_t = vt_ref[:, pl.ds(ks, tk)].astype(BF16)
        causal = rel + (q0 - ks) >= 0
        masks = []
        for kvh in range(KV_HEADS_B):
            rows = sel_ref[kvh, pl.ds(pl.multiple_of(jj * nblk, nblk), nblk), :]
            chosen = jnp.concatenate(
                [jnp.broadcast_to(rows[c:c + 1], (SEL_BLOCK, tq)) for c in range(nblk)], axis=0)
            masks += [(chosen > 0.5) & causal] * gqa
        _attend_tile_t(kt, vt_t, qts, masks, tq, m_ref, acc_ref)
        return carry

    lax.fori_loop(0, (q0 + tq - 1) // tk + 1, body, 0)
    _finish_t(o_ref, None, g_ref, m_ref, acc_ref, nh, gqa, tq)


def _sel_prompt(u, vt, sel_t, *, q_off, k_off, g_off, tq=128, tk=512):
    B, S, W = u.shape
    n_sl = KV_HEADS_B // 2
    nselp = sel_t.shape[2]
    tq = min(tq, S)
    tk = min(tk, S)
    assert tk % SEL_BLOCK == 0 and (tk // SEL_BLOCK) % 8 == 0 and S % tk == 0 and nselp * SEL_BLOCK >= S
    assert q_off % WIDTH_B == 0 and k_off % KVW_B == 0 and g_off % KVW_B == 0
    R = 2 * GQA_B * tq
    return pl.pallas_call(
        functools.partial(_sel_prompt_kernel, tq=tq, tk=tk),
        grid=(B, S // tq),
        in_specs=[
            pl.BlockSpec((None, tq, WIDTH_B), lambda b, i: (b, i, q_off // WIDTH_B)),
            pl.BlockSpec((None, S, KVW_B), lambda b, i: (b, 0, k_off // KVW_B)),
            pl.BlockSpec((None, None, KVW_B, S), lambda b, i: (b, 0, 0, 0)),
            pl.BlockSpec((None, KV_HEADS_B, nselp, tq), lambda b, i: (b, 0, 0, i)),
            pl.BlockSpec((None, tq, KVW_B), lambda b, i: (b, i, g_off // KVW_B)),
        ],
        out_specs=pl.BlockSpec((None, tq, WIDTH_B), lambda b, i: (b, i, 0)),
        out_shape=jax.ShapeDtypeStruct((B, S, WIDTH_B), F32),
        scratch_shapes=[pltpu.VMEM((1, n_sl * R), F32), pltpu.VMEM((n_sl, ACC_ROWS, R), F32)],
        compiler_params=_cparams(("parallel", "arbitrary")),
    )(u, u, vt, sel_t, u)


def _sel_sample_kernel(pt_ref, idx_ref, q_ref, kn_ref, vn_ref, g_ref, pool_ref, o_ref, kvbuf, sem, *,
                       nt, n_pages, n_pick):
    nh, gqa, tq = 2 * GQA_B, GQA_B, SAMPLE_ROWS
    n, p, a = pl.program_id(0), pl.program_id(1), pl.program_id(2)
    step = (n * 2 + p) * 2 + a
    n_steps = pl.num_programs(0) * 4
    blocks_per_page = PAGE_SIZE // SEL_BLOCK
    width = n_pick * PAGE_SIZE

    def picked_block(seq, pair, head, t, r):
        src_row = jnp.where(r == 0, 0, r + 1)
        return idx_ref[((seq * KV_HEADS_B + pair * 2 + head) * IDX_ROWS + src_row) * tq + t]

    def page_copy(st, c, lookup):
        seq, pair, head = st // 4, (st // 2) % 2, st % 2
        t, r = c // n_pick, c % n_pick
        page = pt_ref[seq * n_pages + picked_block(seq, pair, head, t, r) // blocks_per_page] if lookup else 0
        dst0 = pl.multiple_of(r * PAGE_SIZE, PAGE_SIZE)
        half0 = pl.multiple_of(head * HEAD_DIM, HEAD_DIM)
        row0 = pl.multiple_of(pair * LANES + head * HEAD_DIM, HEAD_DIM)
        return pltpu.make_async_copy(
            pool_ref.at[page, pl.ds(2, 2), pl.ds(row0, HEAD_DIM), :],
            kvbuf.at[head, t, :, pl.ds(half0, HEAD_DIM), pl.ds(dst0, PAGE_SIZE)], sem.at[head])

    def issue(st):
        def go(c, carry):
            page_copy(st, c, True).start()
            return carry
        lax.fori_loop(0, nt * n_pick, go, 0, unroll=4)

    @pl.when(step == 0)
    def _():
        kvbuf[0, :, :, HEAD_DIM:, :] = jnp.zeros((nt, 2, HEAD_DIM, width), F32)
        kvbuf[1, :, :, :HEAD_DIM, :] = jnp.zeros((nt, 2, HEAD_DIM, width), F32)
        issue(step)

    @pl.when(step + 1 < n_steps)
    def _():
        issue(step + 1)

    slot = a

    def wait(c, carry):
        page_copy(step, c, False).wait()
        return carry
    lax.fori_loop(0, nt * n_pick, wait, 0, unroll=4)

    R = nh * tq
    Q = _prep_q(q_ref[...], nh, gqa, tq)
    pad = jnp.zeros((LANES - tq, LANES), F32)
    kn = jnp.concatenate([kn_ref[...], pad], axis=0).astype(BF16)
    vn = jnp.concatenate([vn_ref[...], pad], axis=0).astype(BF16)
    row = lax.broadcasted_iota(jnp.int32, (R, LANES), 0)
    colk = lax.broadcasted_iota(jnp.int32, (R, LANES), 1)
    trow = row & (tq - 1)
    s_new = jnp.where((colk <= trow) & (colk < nt), _nt_dot(Q, kn), NEG)
    lane = lax.broadcasted_iota(jnp.int32, (1, width), 1)
    lane_pick = lane // PAGE_SIZE
    lane_half = (lane // SEL_BLOCK) % blocks_per_page
    o = jnp.zeros((R, LANES), F32)
    for t in range(nt):
        chosen = lane < 0
        for r in range(n_pick):
            half = picked_block(n, p, a, t, r) % blocks_per_page
            chosen = chosen | ((lane_pick == r) & (lane_half == half))
        s = jnp.where(chosen, _dot(Q, kvbuf[slot, t, 0].astype(BF16)), NEG)
        m = jnp.maximum(jnp.max(s, axis=-1, keepdims=True), jnp.max(s_new, axis=-1, keepdims=True))
        p1 = jnp.exp(s - m)
        p2 = jnp.exp(s_new - m)
        l = jnp.sum(p1, axis=-1, keepdims=True) + jnp.sum(p2, axis=-1, keepdims=True)
        o_t = (_nt_dot(p1.astype(BF16), kvbuf[slot, t, 1].astype(BF16)) + _dot(p2.astype(BF16), vn)) / l
        o = jnp.where(trow == t, o_t, o)

    g = g_ref[...]
    low = lax.broadcasted_iota(jnp.int32, (tq, LANES), 1) < HEAD_DIM
    first = a == 0
    placed = []
    for jj in range(gqa):
        oj = jnp.where(first, o[jj * tq:(jj + 1) * tq], o[(gqa + jj) * tq:(gqa + jj + 1) * tq])
        gj = jnp.where(first, g[:, jj:jj + 1], g[:, gqa + jj:gqa + jj + 1])
        oj = oj * gj
        placed.append(jnp.where(a == jj % 2, oj, pltpu.roll(oj, HEAD_DIM, axis=1)))
    o_ref[...] = jnp.concatenate([jnp.where(low, placed[0], placed[1]),
                                  jnp.where(low, placed[2], placed[3])], axis=1)


def _sel_sample(us, idx, pool, page_table, *, nt, q_off, k_off, v_off, g_off):
    N, tq, W = us.shape
    n_pages = page_table.shape[1]
    n_pick = SEL_TOPN - 1
    qs = 2 * GQA_B * HEAD_DIM
    hw = GQA_B * HEAD_DIM
    gs = pltpu.PrefetchScalarGridSpec(
        num_scalar_prefetch=2,
        grid=(N, 2, 2),
        in_specs=[
            pl.BlockSpec((None, tq, qs), lambda b, p, a, *_: (b, 0, q_off // qs + p)),
            pl.BlockSpec((None, tq, LANES), lambda b, p, a, *_: (b, 0, k_off // LANES + p)),
            pl.BlockSpec((None, tq, LANES), lambda b, p, a, *_: (b, 0, v_off // LANES + p)),
            pl.BlockSpec((None, tq, LANES), lambda b, p, a, *_: (b, 0, g_off // LANES + p)),
            pl.BlockSpec(memory_space=pl.ANY),
        ],
        out_specs=pl.BlockSpec((None, tq, hw), lambda b, p, a, *_: (b, 0, 2 * p + a)),
        scratch_shapes=[
            pltpu.VMEM((2, nt, 2, LANES, n_pick * PAGE_SIZE), F32),
            pltpu.SemaphoreType.DMA((2,)),
        ],
    )
    return pl.pallas_call(
        functools.partial(_sel_sample_kernel, nt=nt, n_pages=n_pages, n_pick=n_pick),
        grid_spec=gs,
        out_shape=jax.ShapeDtypeStruct((N, tq, WIDTH_B), F32),
        compiler_params=_cparams(("arbitrary", "arbitrary", "arbitrary")),
    )(page_table.reshape(-1), idx.reshape(-1), us, us, us, us, pool)


def _cache_attn_kernel(*refs, n_sl, nh, gqa, L, dil, win, nt, want_lse, gated):
    q_ref, kc_ref, vc_ref, kn_ref, vn_ref = refs[:5]
    pos = 5
    g_ref = None
    if gated:
        g_ref = refs[pos]
        pos += 1
    o_ref = refs[pos]
    lse_ref = refs[pos + 1] if want_lse else None
    tq = SAMPLE_ROWS
    R = nh * tq
    qs = nh * HEAD_DIM
    pad = jnp.zeros((LANES - tq, LANES), F32)
    t1 = lax.broadcasted_iota(jnp.int32, (R, L), 0) & (tq - 1)
    d1 = L + t1 - lax.broadcasted_iota(jnp.int32, (R, L), 1)
    ok1 = ((d1 & (dil - 1)) == 0) & (d1 <= win)
    t2 = lax.broadcasted_iota(jnp.int32, (R, LANES), 0) & (tq - 1)
    c2 = lax.broadcasted_iota(jnp.int32, (R, LANES), 1)
    d2 = t2 - c2
    ok2 = (d2 >= 0) & ((d2 & (dil - 1)) == 0) & (d2 <= win) & (c2 < nt)
    for sl in range(n_sl):
        rows = slice(sl * LANES, (sl + 1) * LANES)
        Q = _prep_q(q_ref[:, sl * qs:(sl + 1) * qs], nh, gqa, tq)
        kc_t = kc_ref[rows, :].astype(BF16)
        vc_t = vc_ref[rows, :].astype(BF16)
        kn = jnp.concatenate([kn_ref[:, rows], pad], axis=0).astype(BF16)
        vn = jnp.concatenate([vn_ref[:, rows], pad], axis=0).astype(BF16)
        s1 = jnp.where(ok1, _dot(Q, kc_t), NEG)
        s2 = jnp.where(ok2, _nt_dot(Q, kn), NEG)
        m = jnp.maximum(jnp.max(s1, axis=-1, keepdims=True), jnp.max(s2, axis=-1, keepdims=True))
        p1 = jnp.exp(s1 - m)
        p2 = jnp.exp(s2 - m)
        l = jnp.sum(p1, axis=-1, keepdims=True) + jnp.sum(p2, axis=-1, keepdims=True)
        o = (_nt_dot(p1.astype(BF16), vc_t) + _dot(p2.astype(BF16), vn)) / l
        if g_ref is not None:
            g = g_ref[:, rows]
            o = jnp.concatenate([o[j * tq:(j + 1) * tq] * g[:, j:j + 1] for j in range(nh)], axis=0)
        o_ref[:, sl * qs:(sl + 1) * qs] = _unprep_o(o, nh, gqa, tq)
        if want_lse:
            lse_ref[:, sl * qs:(sl + 1) * qs] = _unprep_o(jnp.broadcast_to(m + jnp.log(l), (R, LANES)),
                                                          nh, gqa, tq)


def _feature_major(x):
    nd = x.ndim
    xt = jnp.transpose(x, (0, 1) + tuple(range(3, nd)) + (2,))
    return xt.reshape(x.shape[0], x.shape[1], -1, x.shape[2])


def _cache_attention(us, cache_t, layer, *, q_off, k_off, v_off, nh, gqa, n_slabs, dil, win, nt, want_lse,
                     g_off=None):
    N, tq, W = us.shape
    L = cache_t.shape[3]
    cache = cache_t
    wo = n_slabs * nh * HEAD_DIM
    kwid = n_slabs * LANES
    assert q_off % wo == 0 and k_off % kwid == 0 and v_off % kwid == 0
    in_specs = [
        pl.BlockSpec((None, tq, wo), lambda b: (b, 0, q_off // wo)),
        pl.BlockSpec((None, None, kwid, L), lambda b: (layer, b, 0, 0)),
        pl.BlockSpec((None, None, kwid, L), lambda b: (layer, b, 1, 0)),
        pl.BlockSpec((None, tq, kwid), lambda b: (b, 0, k_off // kwid)),
        pl.BlockSpec((None, tq, kwid), lambda b: (b, 0, v_off // kwid)),
    ]
    args = [us, cache, cache, us, us]
    if g_off is not None:
        assert g_off % kwid == 0
        in_specs.append(pl.BlockSpec((None, tq, kwid), lambda b: (b, 0, g_off // kwid)))
        args.append(us)
    o_spec = pl.BlockSpec((None, tq, wo), lambda b: (b, 0, 0))
    o_shape = jax.ShapeDtypeStruct((N, tq, wo), F32)
    return pl.pallas_call(
        functools.partial(_cache_attn_kernel, n_sl=n_slabs, nh=nh, gqa=gqa, L=L, dil=dil, win=win, nt=nt,
                          want_lse=want_lse, gated=g_off is not None),
        grid=(N,),
        in_specs=in_specs,
        out_specs=[o_spec, o_spec] if want_lse else o_spec,
        out_shape=[o_shape, o_shape] if want_lse else o_shape,
        compiler_params=_cparams(("parallel",)),
    )(*args)


def _merge_a_kernel(o0, o1, o2, l0, l1, l2, z_ref, x_ref, w_ref, y_ref):
    a0, a1, a2 = l0[...], l1[...], l2[...]
    mx = jnp.maximum(jnp.maximum(a0, a1), a2)
    e0, e1, e2 = jnp.exp(a0 - mx), jnp.exp(a1 - mx), jnp.exp(a2 - mx)
    den = e0 + e1 + e2
    o = (e0 / den) * o0[...] + (e1 / den) * o1[...] + (e2 / den) * o2[...]
    y_ref[...] = x_ref[...] + _dot((o * jax.nn.silu(z_ref[...])).astype(BF16), w_ref[...])


def _merge_a(outs, lses, u, z_off, x, w_out_bf16):
    T, D = x.shape
    tm = min(T, 512)
    wa = WIDTH_A
    row = pl.BlockSpec((tm, wa), lambda i: (i, 0))
    return pl.pallas_call(
        _merge_a_kernel,
        grid=(T // tm,),
        in_specs=[row] * 6 + [
            pl.BlockSpec((tm, wa), lambda i: (i, z_off // wa)),
            pl.BlockSpec((tm, D), lambda i: (i, 0)),
            pl.BlockSpec((wa, D), lambda i: (0, 0)),
        ],
        out_specs=pl.BlockSpec((tm, D), lambda i: (i, 0)),
        out_shape=jax.ShapeDtypeStruct((T, D), F32),
        compiler_params=_cparams(("parallel",)),
    )(*outs, *lses, u, x, w_out_bf16)


def _merge_b_kernel(oc, os_, ow, z_ref, x_ref, w_ref, y_ref):
    o = oc[...] + os_[...] + ow[...]
    y_ref[...] = x_ref[...] + _dot((o * jax.nn.silu(z_ref[...])).astype(BF16), w_ref[...])


def _merge_b(o_c, o_s, o_w, u, z_off, x, w_out_bf16):
    T, D = x.shape
    tm = min(T, 512)
    wb = WIDTH_B
    row = pl.BlockSpec((tm, wb), lambda i: (i, 0))
    return pl.pallas_call(
        _merge_b_kernel,
        grid=(T // tm,),
        in_specs=[row] * 3 + [
            pl.BlockSpec((tm, wb), lambda i: (i, z_off // wb)),
            pl.BlockSpec((tm, D), lambda i: (i, 0)),
            pl.BlockSpec((wb, D), lambda i: (0, 0)),
        ],
        out_specs=pl.BlockSpec((tm, D), lambda i: (i, 0)),
        out_shape=jax.ShapeDtypeStruct((T, D), F32),
        compiler_params=_cparams(("parallel",)),
    )(o_c, o_s, o_w, u, x, w_out_bf16)


A_KINDS = ([EP_ROPE_Q] * (N_GROUPS_A * WIDTH_A // PROJ_TN) + [EP_ROPE_K] * (N_GROUPS_A * WIDTH_A // PROJ_TN)
           + [EP_NONE] * (N_GROUPS_A * WIDTH_A // PROJ_TN) + [EP_NONE] * (WIDTH_A // PROJ_TN))
A_K, A_V, A_Z = N_GROUPS_A * WIDTH_A, 2 * N_GROUPS_A * WIDTH_A, QKV_A

B_KINDS = ([EP_NORM_Q] * 4 + [EP_ROPE_Q] * 4 + [EP_NONE] * 4
           + [EP_NORM_K, EP_NONE, EP_ROPE_K, EP_NONE, EP_ROPE_K, EP_NONE] + [EP_SIG] * 3 + [EP_NONE])
B_PAD = len(B_KINDS) * PROJ_TN - B_COLS


def _b_weight(w_in):
    wq = w_in[:, :WIDTH_B]
    o1 = WIDTH_B + 6 * KVW_B
    wkv = w_in[:, WIDTH_B:o1]
    wg = w_in[:, o1:o1 + 3 * HEADS_B]
    wz = w_in[:, o1 + 3 * HEADS_B:]
    nh = 2 * GQA_B
    tiles = []
    for b in range(3):
        for p in range(2):
            g = wg[:, b * HEADS_B + p * nh: b * HEADS_B + (p + 1) * nh]
            tiles.append(jnp.pad(g, ((0, 0), (0, LANES - nh))))
    tiles.append(jnp.zeros((w_in.shape[0], B_PAD), w_in.dtype))
    return jnp.concatenate([wq, wq, wz, wkv] + tiles, axis=1).astype(BF16)


def _pad_rows(u, N, T):
    return jnp.pad(u.reshape(N, T, -1), ((0, 0), (0, SAMPLE_ROWS - T), (0, 0)))


def _layer_a_prompt(x, norm_g, w_bf16, q_gain, k_gain, w_out_bf16, tabs):
    B, S, D = x.shape
    u = _project(x.reshape(B * S, D), norm_g, w_bf16, A_KINDS, tabs[0], tabs[1], q_gain, k_gain)
    u3 = u.reshape(B, S, -1)
    outs, lses, states = [], [], []
    for g, (win, dil) in enumerate(DIL_PATTERNS):
        vt = _values_transposed(u3, A_V + g * WIDTH_A, WIDTH_A, dil)
        if dil == 1:
            uq, uk, q_off, k_off = u3, u3, g * WIDTH_A, A_K + g * WIDTH_A
        else:
            uq = u3[:, :, g * WIDTH_A:(g + 1) * WIDTH_A]
            uk = u3[:, :, A_K + g * WIDTH_A:A_K + (g + 1) * WIDTH_A]
            q_off, k_off = 0, 0
        o, lse = _band_attention(uq, vt, uk=uk, q_off=q_off, k_off=k_off,
                                 nh=2, gqa=1, n_slabs=HEADS_A // 2, kw=win // dil, dil=dil, want_lse=True)
        outs.append(o.reshape(B * S, WIDTH_A))
        lses.append(lse.reshape(B * S, WIDTH_A))
        w = min(win, S)
        kg = u3[:, S - w:, A_K + g * WIDTH_A: A_K + (g + 1) * WIDTH_A]
        vg = u3[:, S - w:, A_V + g * WIDTH_A: A_V + (g + 1) * WIDTH_A]
        states.append(jnp.stack([kg, vg], axis=2).reshape(B, w, 2, HEADS_A, HEAD_DIM))
    y = _merge_a(outs, lses, u, A_Z, x.reshape(B * S, D), w_out_bf16)
    return y.reshape(B, S, D), states


def _layer_a_sample(x, caches_t, layer, norm_g, w_bf16, q_gain, k_gain, w_out_bf16, tabs):
    N, T, D = x.shape
    u = _project(x.reshape(N * T, D), norm_g, w_bf16, A_KINDS, tabs[0], tabs[1], q_gain, k_gain)
    us = _pad_rows(u, N, T)
    outs, lses, new_rows = [], [], []
    for g, (win, dil) in enumerate(DIL_PATTERNS):
        o, lse = _cache_attention(us, caches_t[g], layer, q_off=g * WIDTH_A, k_off=A_K + g * WIDTH_A,
                                  v_off=A_V + g * WIDTH_A, nh=2, gqa=1, n_slabs=HEADS_A // 2,
                                  dil=dil, win=win, nt=T, want_lse=True)
        outs.append(o[:, :T].reshape(N * T, WIDTH_A))
        lses.append(lse[:, :T].reshape(N * T, WIDTH_A))
        new_rows.append(jnp.concatenate([us[:, :, A_K + g * WIDTH_A: A_K + (g + 1) * WIDTH_A],
                                         us[:, :, A_V + g * WIDTH_A: A_V + (g + 1) * WIDTH_A]], axis=2))
    y = _merge_a(outs, lses, u, A_Z, x.reshape(N * T, D), w_out_bf16)
    return y.reshape(N, T, D), new_rows


def _layer_b_prompt(x, norm_g, w_bf16, q_gain, k_gain, cw, w_out_bf16, tabs):
    B, S, D = x.shape
    u = _project(x.reshape(B * S, D), norm_g, w_bf16, B_KINDS, tabs[0], tabs[1], q_gain, k_gain)
    u3 = u.reshape(B, S, -1)
    kvc, kvc_t = _compress_prompt(u3, B_KV, cw)
    o_c, sel_t, _ = _cmp_attention(u3, B_QC, kvc, u3, B_GATE, tq=min(512, S), pos_base=0, n_keys=S,
                                   kvc_t=kvc_t)
    o_s = _sel_prompt(u3, _values_transposed(u3, B_KV + 3 * KVW_B, KVW_B, 1), sel_t, q_off=B_QR,
                      k_off=B_KV + 2 * KVW_B, g_off=B_GATE + 2 * LANES)
    o_w = _band_attention(u3, _values_transposed(u3, B_WIN + KVW_B, KVW_B, 1), q_off=B_QR, k_off=B_WIN,
                          nh=2 * GQA_B, gqa=GQA_B, n_slabs=2, kw=WIN_B, dil=1, want_lse=False, gates=u3,
                          g_off=B_GATE + 4 * LANES, tq=256, tk=256)
    y = _merge_b(o_c.reshape(B * S, -1), o_s.reshape(B * S, -1), o_w.reshape(B * S, -1), u, B_Z,
                 x.reshape(B * S, D), w_out_bf16)
    rows = u3[:, :, B_KV:B_KV + 4 * KVW_B].reshape(B, S, 4, KV_HEADS_B, HEAD_DIM)
    w = min(WIN_B, S)
    wrows = u3[:, S - w:, B_WIN:B_WIN + 2 * KVW_B].reshape(B, w, 2, KV_HEADS_B, HEAD_DIM)
    return y.reshape(B, S, D), rows, wrows


def _layer_b_sample(x, pool_t, page_table, win_t, layer, norm_g, w_bf16, q_gain, k_gain, cw, w_out_bf16, tabs):
    N, T, D = x.shape
    past_len = page_table.shape[1] * PAGE_SIZE
    assert past_len % SEL_BLOCK == 0 and T <= SEL_BLOCK and T <= SAMPLE_ROWS
    u = _project(x.reshape(N * T, D), norm_g, w_bf16, B_KINDS, tabs[0], tabs[1], q_gain, k_gain)
    us = _pad_rows(u, N, T)
    u3 = u.reshape(N, T, -1)
    kvc = _compress_sample(pool_t, page_table, cw)
    o_c, _, idx = _cmp_attention(us, B_QC, kvc, us, B_GATE, tq=SAMPLE_ROWS, pos_base=past_len,
                                 n_keys=past_len + T)
    o_s = _sel_sample(us, idx, pool_t, page_table, nt=T, q_off=B_QR, k_off=B_KV + 2 * KVW_B,
                      v_off=B_KV + 3 * KVW_B, g_off=B_GATE + 2 * LANES)
    o_w = _cache_attention(us, win_t, layer, q_off=B_QR, k_off=B_WIN, v_off=B_WIN + KVW_B,
                           nh=2 * GQA_B, gqa=GQA_B, n_slabs=2, dil=1, win=WIN_B, nt=T, want_lse=False,
                           g_off=B_GATE + 4 * LANES)
    y = _merge_b(o_c[:, :T].reshape(N * T, -1), o_s[:, :T].reshape(N * T, -1), o_w[:, :T].reshape(N * T, -1),
                 u, B_Z, x.reshape(N * T, D), w_out_bf16)
    rows = u3[:, :, B_KV:B_KV + 4 * KVW_B].reshape(N, T, 4, KV_HEADS_B, HEAD_DIM)
    return y.reshape(N, T, D), rows, us[:, :, B_WIN:B_WIN + 2 * KVW_B]


def _append_kernel(c_ref, n_ref, o_ref, *, L, nt, fb):
    rolled = pltpu.roll(c_ref[...], L - nt, axis=1)
    pad = jnp.zeros((LANES - SAMPLE_ROWS, LANES), F32)
    new = n_ref[...]
    new_t = jnp.concatenate(
        [jnp.concatenate([new[:, c * LANES:(c + 1) * LANES], pad], axis=0).T for c in range(fb // LANES)], axis=0)
    new_t = pltpu.roll(new_t, LANES - nt, axis=1)
    lane = lax.broadcasted_iota(jnp.int32, (fb, LANES), 1)
    if L > LANES:
        o_ref[:, :L - LANES] = rolled[:, :L - LANES]
    o_ref[:, L - LANES:] = jnp.where(lane >= LANES - nt, new_t, rolled[:, L - LANES:])


def _append_cache(cache, cache_t, new_rows, nt, win):
    J, N, F, L = cache_t.shape
    tail = cache.shape[3:]
    if L + nt <= win or L % LANES:
        new = new_rows[:, :, :nt].reshape((J, N, nt) + tail)
        return jnp.concatenate([cache, new], axis=2)[:, :, -min(win, L + nt):]
    assert L == win
    fb = min(F, LANES * max(1, 2048 // L))
    assert F % fb == 0
    out_t = pl.pallas_call(
        functools.partial(_append_kernel, L=L, nt=nt, fb=fb),
        grid=(J, N, F // fb),
        in_specs=[pl.BlockSpec((None, None, fb, L), lambda j, n, f: (j, n, f, 0)),
                  pl.BlockSpec((None, None, SAMPLE_ROWS, fb), lambda j, n, f: (j, n, 0, f))],
        out_specs=pl.BlockSpec((None, None, fb, L), lambda j, n, f: (j, n, f, 0)),
        out_shape=jax.ShapeDtypeStruct((J, N, F, L), F32),
        compiler_params=_cparams(("parallel", "parallel", "parallel")),
    )(cache_t, new_rows)
    nd = len(tail)
    out = out_t.reshape((J, N) + tail + (L,))
    return jnp.transpose(out, (0, 1, nd + 2) + tuple(range(2, nd + 2)))


def kernel(x_prompt, x_sample, cache_dil_0, cache_dil_1, cache_dil_2, cache_nsa_paged, cache_nsa_win,
           page_table, a_norm, a_w_in, a_q_norm, a_k_norm, a_w_out, b_norm, b_w_in, b_q_norm, b_k_norm,
           b_cmp_pe, b_cmp_w1, b_cmp_w2, b_w_out):
    dil_caches = (cache_dil_0, cache_dil_1, cache_dil_2)
    B, S, _ = x_prompt.shape
    N, T, _ = x_sample.shape
    past_len = page_table.shape[1] * PAGE_SIZE
    depth = a_norm.shape[0] + b_norm.shape[0]
    n_pool = cache_nsa_paged.shape[1]
    tabs_p = _rope_tables(jnp.tile(jnp.arange(S), B))
    tabs_s = _rope_tables(jnp.tile(past_len + jnp.arange(T), N))
    dil_t = [_feature_major(c) for c in dil_caches]
    win_t = _feature_major(cache_nsa_win)
    pool_t = _feature_major(cache_nsa_paged)
    pool_t = pool_t.reshape(-1, 4, KVW_B, PAGE_SIZE)
    xp, xs = x_prompt, x_sample
    dil_p = [[] for _ in DIL_PATTERNS]
    dil_new = [[] for _ in DIL_PATTERNS]
    rows_p, rows_s, win_p, win_new = [], [], [], []
    for layer in range(depth):
        j = layer // 2
        if layer % 2 == 0:
            w = a_w_in[j].astype(BF16)
            wo = a_w_out[j].astype(BF16)
            xp, st_p = _layer_a_prompt(xp, a_norm[j], w, a_q_norm[j], a_k_norm[j], wo, tabs_p)
            xs, new_s = _layer_a_sample(xs, dil_t, j, a_norm[j], w, a_q_norm[j], a_k_norm[j], wo, tabs_s)
            for g in range(N_GROUPS_A):
                dil_p[g].append(st_p[g])
                dil_new[g].append(new_s[g])
        else:
            w = _b_weight(b_w_in[j])
            wo = b_w_out[j].astype(BF16)
            cw = _compress_weights(b_cmp_pe[j], b_cmp_w1[j], b_cmp_w2[j], b_k_norm[j])
            xp, rp, wp = _layer_b_prompt(xp, b_norm[j], w, b_q_norm[j], b_k_norm[j], cw, wo, tabs_p)
            xs, rs, wn = _layer_b_sample(xs, pool_t, page_table + j * n_pool, win_t, j, b_norm[j], w,
                                         b_q_norm[j], b_k_norm[j], cw, wo, tabs_s)
            rows_p.append(rp)
            win_p.append(wp)
            rows_s.append(rs)
            win_new.append(wn)
    dil_s = [_append_cache(dil_caches[g], dil_t[g], jnp.stack(dil_new[g]), T, DIL_PATTERNS[g][0])
             for g in range(N_GROUPS_A)]
    win_s = _append_cache(cache_nsa_win, win_t, jnp.stack(win_new), T, WIN_B)
    return (xp, xs,
            jnp.stack(dil_p[0]), jnp.stack(dil_p[1]), jnp.stack(dil_p[2]),
            jnp.stack(rows_p), jnp.stack(win_p),
            dil_s[0], dil_s[1], dil_s[2],
            jnp.stack(rows_s), win_s)
```

```python
import functools

import jax
import jax.numpy as jnp
from jax import lax
from jax.experimental import pallas as pl
from jax.experimental.pallas import tpu as pltpu

F32 = jnp.float32
BF16 = jnp.bfloat16

HEAD_DIM = 64
SCALE = HEAD_DIM ** -0.5
ROPE_THETA = 10000.0
EPS = 1e-6
TINY = 1e-30
NEG = -1e30
DIL_PATTERNS = ((128, 1), (512, 4), (2048, 16))
N_GROUPS_A = len(DIL_PATTERNS)
HEADS_A = 8
WIDTH_A = HEADS_A * HEAD_DIM
QKV_A = 3 * N_GROUPS_A * WIDTH_A
KV_HEADS_B = 4
GQA_B = 4
HEADS_B = KV_HEADS_B * GQA_B
WIDTH_B = HEADS_B * HEAD_DIM
KVW_B = KV_HEADS_B * HEAD_DIM
CMP_LEN = 32
CMP_STRIDE = 16
CMP_RATIO = CMP_LEN // CMP_STRIDE
CMP_HIDDEN = 128
SEL_BLOCK = 64
SEL_TOPN = 16
WIN_B = 512
PAGE_SIZE = 128

LANES = 128
PROJ_TN = 256
PROJ_SUB = 2
VMEM_LIMIT = 56 * 1024 * 1024
IDX_ROWS = 24
SAMPLE_ROWS = 8

B_QC, B_QR, B_Z, B_KV, B_WIN, B_GATE = 0, 1024, 2048, 3072, 4096, 4608
B_COLS = B_GATE + 6 * LANES

EP_NONE, EP_NORM_Q, EP_ROPE_Q, EP_NORM_K, EP_ROPE_K, EP_SIG = range(6)


def _cparams(sem):
    return pltpu.CompilerParams(dimension_semantics=sem, vmem_limit_bytes=VMEM_LIMIT)


def _nt_dot(a, b):
    return lax.dot_general(a, b, (((1,), (1,)), ((), ())), preferred_element_type=F32)


def _dot(a, b):
    return jnp.dot(a, b, preferred_element_type=F32)


def _head_norm(acc, bd_ref, gain):
    msq = _dot((acc * acc).astype(BF16), bd_ref[...])
    return acc * lax.rsqrt(msq + EPS) * gain


def _rope_tile(y, cos, sin_signed):
    lane = lax.broadcasted_iota(jnp.int32, (y.shape[0], LANES), 1)
    first_half = (lane & (HEAD_DIM - 1)) < (HEAD_DIM // 2)
    outs = []
    for c in range(y.shape[1] // LANES):
        yc = y[:, c * LANES:(c + 1) * LANES]
        partner = jnp.where(first_half, pltpu.roll(yc, LANES - HEAD_DIM // 2, axis=1),
                            pltpu.roll(yc, HEAD_DIM // 2, axis=1))
        outs.append(yc * cos + partner * sin_signed)
    return jnp.concatenate(outs, axis=1)


def _proj_kernel(x_ref, g_ref, w_ref, cos_ref, sin_ref, qg_ref, kg_ref, bd_ref, o_ref, h_ref, *, kinds):
    j = pl.program_id(1)

    @pl.when(j == 0)
    def _():
        x = x_ref[...]
        r = lax.rsqrt(jnp.mean(x * x, axis=-1, keepdims=True) + EPS)
        h_ref[...] = (x * r * g_ref[...]).astype(BF16)

    acc_all = _dot(h_ref[...], w_ref[...])

    def ranges(kind, half):
        out, start = [], None
        for t, k in enumerate(list(kinds[half::PROJ_SUB]) + [None]):
            if k == kind and start is None:
                start = t
            if k != kind and start is not None:
                out.append((start, t))
                start = None
        return out

    for half in range(PROJ_SUB):
        acc = acc_all[:, half * PROJ_TN:(half + 1) * PROJ_TN]
        cols = slice(half * PROJ_TN, (half + 1) * PROJ_TN)

        def emit(kind, fn, half=half, cols=cols):
            rs = ranges(kind, half)
            if not rs:
                return
            cond = None
            for a, b in rs:
                c = (j >= a) & (j < b)
                cond = c if cond is None else (cond | c)

            @pl.when(cond)
            def _():
                o_ref[:, cols] = fn()

        emit(EP_NONE, lambda acc=acc: acc)
        emit(EP_SIG, lambda acc=acc: jax.nn.sigmoid(acc))
        emit(EP_NORM_Q, lambda acc=acc: _head_norm(acc, bd_ref, qg_ref[...]))
        emit(EP_NORM_K, lambda acc=acc: _head_norm(acc, bd_ref, kg_ref[...]))
        emit(EP_ROPE_Q, lambda acc=acc: _rope_tile(_head_norm(acc, bd_ref, qg_ref[...]), cos_ref[...], sin_ref[...]))
        emit(EP_ROPE_K, lambda acc=acc: _rope_tile(_head_norm(acc, bd_ref, kg_ref[...]), cos_ref[...], sin_ref[...]))


def _project(x, norm_g, w_bf16, kinds, cos_t, sin_t, q_gain, k_gain):
    T, D = x.shape
    E = w_bf16.shape[1]
    tn = PROJ_TN
    assert E == len(kinds) * tn and len(kinds) % PROJ_SUB == 0
    tm = min(T, 2048)
    assert T % tm == 0
    rep = tn // HEAD_DIM
    qg = jnp.tile(q_gain.astype(F32), rep)[None]
    kg = jnp.tile(k_gain.astype(F32), rep)[None]
    hid = jnp.arange(tn) // HEAD_DIM
    bd = ((hid[:, None] == hid[None, :]).astype(F32) / HEAD_DIM).astype(BF16)
    return pl.pallas_call(
        functools.partial(_proj_kernel, kinds=tuple(kinds)),
        grid=(T // tm, E // (PROJ_SUB * tn)),
        in_specs=[
            pl.BlockSpec((tm, D), lambda i, j: (i, 0)),
            pl.BlockSpec((1, D), lambda i, j: (0, 0)),
            pl.BlockSpec((D, PROJ_SUB * tn), lambda i, j: (0, j)),
            pl.BlockSpec((tm, LANES), lambda i, j: (i, 0)),
            pl.BlockSpec((tm, LANES), lambda i, j: (i, 0)),
            pl.BlockSpec((1, tn), lambda i, j: (0, 0)),
            pl.BlockSpec((1, tn), lambda i, j: (0, 0)),
            pl.BlockSpec((tn, tn), lambda i, j: (0, 0)),
        ],
        out_specs=pl.BlockSpec((tm, PROJ_SUB * tn), lambda i, j: (i, j)),
        out_shape=jax.ShapeDtypeStruct((T, E), F32),
        scratch_shapes=[pltpu.VMEM((tm, D), BF16)],
        compiler_params=_cparams(("parallel", "arbitrary")),
    )(x, norm_g.astype(F32)[None], w_bf16, cos_t, sin_t, qg, kg, bd)


def _rope_tables(pos):
    half = HEAD_DIM // 2
    inv_freq = ROPE_THETA ** (-jnp.arange(half, dtype=F32) / half)
    ang = pos.astype(F32)[:, None] * inv_freq[None, :]
    cos, sin = jnp.cos(ang), jnp.sin(ang)
    cos_t = jnp.concatenate([cos, cos, cos, cos], axis=1)
    sin_t = jnp.concatenate([-sin, sin, -sin, sin], axis=1)
    return cos_t, sin_t


def _prep_q(q, nh, gqa, tq):
    lane = lax.broadcasted_iota(jnp.int32, (tq, LANES), 1)
    low = lane < HEAD_DIM
    parts = []
    for j in range(nh):
        col = q[:, (j // 2) * LANES:(j // 2 + 1) * LANES]
        nat, tgt = j % 2, (j // gqa) % 2
        if nat != tgt:
            col = pltpu.roll(col, HEAD_DIM, axis=1)
        keep = low if tgt == 0 else jnp.logical_not(low)
        parts.append(jnp.where(keep, col * SCALE, 0.0))
    return jnp.concatenate(parts, axis=0).astype(BF16)


def _unprep_o(o, nh, gqa, tq):
    lane = lax.broadcasted_iota(jnp.int32, (tq, LANES), 1)
    low = lane < HEAD_DIM
    cols = []
    for c in range(nh // 2):
        halves = []
        for nat in (0, 1):
            j = 2 * c + nat
            tgt = (j // gqa) % 2
            oj = o[j * tq:(j + 1) * tq]
            if tgt != nat:
                oj = pltpu.roll(oj, HEAD_DIM, axis=1)
            halves.append(oj)
        cols.append(jnp.where(low, halves[0], halves[1]))
    return jnp.concatenate(cols, axis=1)


def _apply_gates(o, g_ref, nh, tq):
    if g_ref is None:
        return o
    g = g_ref[...]
    return jnp.concatenate([o[j * tq:(j + 1) * tq] * g[:, j:j + 1] for j in range(nh)], axis=0)


def _q_transposed(q, nh, gqa, tq):
    qt = (q * SCALE).T
    zero = jnp.zeros((HEAD_DIM, tq), F32)
    cols = []
    for j in range(nh):
        h = qt[j * HEAD_DIM:(j + 1) * HEAD_DIM]
        cols.append(jnp.concatenate([h, zero] if (j // gqa) % 2 == 0 else [zero, h], axis=0))
    return jnp.concatenate(cols, axis=1).astype(BF16)


def _masked_heads(s, masks, tq):
    return jnp.concatenate([jnp.where(mk, s[:, j * tq:(j + 1) * tq], NEG) for j, mk in enumerate(masks)], axis=1)


ACC_ROWS = LANES + 16


def _attend_tile_t(kt, vt_t, qts, masks, tq, m_ref, acc_ref):
    n_sl = len(qts)
    rs = qts[0].shape[1]
    s = jnp.concatenate([_dot(kt[:, sl * LANES:(sl + 1) * LANES], qts[sl]) for sl in range(n_sl)], axis=1)
    s = _masked_heads(s, masks, tq)
    m_old = m_ref[...]
    m_new = jnp.maximum(m_old, jnp.max(s, axis=0, keepdims=True))
    alpha = jnp.exp(m_old - m_new)
    p = jnp.exp((s - m_new).astype(BF16))
    ones = jnp.ones((ACC_ROWS - LANES, kt.shape[0]), BF16)
    for sl in range(n_sl):
        cols = slice(sl * rs, (sl + 1) * rs)
        vt1 = jnp.concatenate([vt_t[sl * LANES:(sl + 1) * LANES], ones], axis=0)
        acc_ref[sl] = alpha[:, cols] * acc_ref[sl] + _dot(vt1, p[:, cols])
    m_ref[...] = m_new


def _init_stats(m_ref, acc_ref):
    m_ref[...] = jnp.full(m_ref.shape, NEG, F32)
    acc_ref[...] = jnp.zeros(acc_ref.shape, F32)


def _finish_t(o_ref, lse_ref, g_ref, m_ref, acc_ref, nh, gqa, tq):
    n_sl = acc_ref.shape[0]
    g_t = g_ref[...].T if g_ref is not None else None
    parts, lse_rows = [], []
    for sl in range(n_sl):
        l = acc_ref[sl, LANES:LANES + 1, :]
        inv = 1.0 / l
        for j in range(nh):
            half = (j // gqa) % 2
            sc = inv[:, j * tq:(j + 1) * tq]
            if g_t is not None:
                sc = sc * g_t[sl * LANES + j:sl * LANES + j + 1, :]
            parts.append(acc_ref[sl, half * HEAD_DIM:(half + 1) * HEAD_DIM, j * tq:(j + 1) * tq] * sc)
        if lse_ref is not None:
            lse = m_ref[:, sl * nh * tq:(sl + 1) * nh * tq] + jnp.log(l)
            lse_rows += [jnp.broadcast_to(lse[:, j * tq:(j + 1) * tq], (HEAD_DIM, tq)) for j in range(nh)]
    o_ref[...] = jnp.concatenate(parts, axis=0).T
    if lse_ref is not None:
        lse_ref[...] = jnp.concatenate(lse_rows, axis=0).T


def _slab_queries(q_ref, n_sl, nh, gqa, tq):
    qs = nh * HEAD_DIM
    return [_q_transposed(q_ref[:, sl * qs:(sl + 1) * qs], nh, gqa, tq) for sl in range(n_sl)]


def _band_kernel(*refs, n_sl, nh, gqa, tq, tk, kw, want_lse, gated):
    q_ref, k_ref, vt_ref = refs[:3]
    pos = 3
    g_ref = None
    if gated:
        g_ref = refs[pos]
        pos += 1
    o_ref = refs[pos]
    pos += 1
    lse_ref = None
    if want_lse:
        lse_ref = refs[pos]
        pos += 1
    m_ref, acc_ref = refs[pos:pos + 2]

    i = pl.program_id(2)
    q0 = i * tq
    qts = _slab_queries(q_ref, n_sl, nh, gqa, tq)
    _init_stats(m_ref, acc_ref)
    rel = lax.broadcasted_iota(jnp.int32, (tk, tq), 1) - lax.broadcasted_iota(jnp.int32, (tk, tq), 0)

    def body(jj, carry):
        ks = pl.multiple_of(jj * tk, tk)
        kt = k_ref[pl.ds(ks, tk), :].astype(BF16)
        vt_t = vt_ref[:, pl.ds(ks, tk)].astype(BF16)
        dist = rel + (q0 - ks)
        _attend_tile_t(kt, vt_t, qts, [(dist >= 0) & (dist <= kw)] * (n_sl * nh), tq, m_ref, acc_ref)
        return carry

    lo_t = jnp.maximum(q0 - kw, 0) // tk
    hi_t = (q0 + tq - 1) // tk
    lax.fori_loop(lo_t, hi_t + 1, body, 0)
    _finish_t(o_ref, lse_ref, g_ref, m_ref, acc_ref, nh, gqa, tq)


def _band_attention(u, vt, *, uk=None, q_off, k_off, nh, gqa, n_slabs, kw, dil, want_lse,
                    gates=None, g_off=0, tq=256, tk=256):
    B, S, W = u.shape
    uk = u if uk is None else uk
    Wk = uk.shape[2]
    n = S // dil
    tq = min(tq, n)
    tk = min(tk, n)
    uv = u.reshape(B, n, dil * W)
    ukv = uk.reshape(B, n, dil * Wk)
    wo = n_slabs * nh * HEAD_DIM
    kwid = n_slabs * LANES
    assert q_off % wo == 0 and k_off % kwid == 0 and (dil == 1 or (W % wo == 0 and Wk % kwid == 0))
    in_specs = [
        pl.BlockSpec((None, tq, wo), lambda b, r, i: (b, i, r * (W // wo) + q_off // wo)),
        pl.BlockSpec((None, n, kwid), lambda b, r, i: (b, 0, r * (Wk // kwid) + k_off // kwid)),
        pl.BlockSpec((None, None, kwid, n), lambda b, r, i: (b, r, 0, 0)),
    ]
    args = [uv, ukv, vt]
    if gates is not None:
        Wg = gates.shape[-1]
        assert g_off % kwid == 0 and (dil == 1 or Wg % kwid == 0)
        in_specs.append(pl.BlockSpec((None, tq, kwid), lambda b, r, i: (b, i, r * (Wg // kwid) + g_off // kwid)))
        args.append(gates.reshape(B, n, dil * Wg))
    o_spec = pl.BlockSpec((None, tq, wo), lambda b, r, i: (b, i, r))
    o_shape = jax.ShapeDtypeStruct((B, n, dil * wo), F32)
    R = nh * tq
    res = pl.pallas_call(
        functools.partial(_band_kernel, n_sl=n_slabs, nh=nh, gqa=gqa, tq=tq, tk=tk, kw=kw, want_lse=want_lse,
                          gated=gates is not None),
        grid=(B, dil, n // tq),
        in_specs=in_specs,
        out_specs=[o_spec, o_spec] if want_lse else o_spec,
        out_shape=[o_shape, o_shape] if want_lse else o_shape,
        scratch_shapes=[pltpu.VMEM((1, n_slabs * R), F32), pltpu.VMEM((n_slabs, ACC_ROWS, R), F32)],
        compiler_params=_cparams(("parallel", "parallel", "arbitrary")),
    )(*args)
    if want_lse:
        return res[0].reshape(B, S, wo), res[1].reshape(B, S, wo)
    return res.reshape(B, S, wo)


def _values_transposed(u, v_off, width, dil):
    B, S, _ = u.shape
    v = u[:, :, v_off:v_off + width].reshape(B, S // dil, dil, width)
    return jnp.transpose(v, (0, 2, 3, 1))


def _compress_math(x_ref, w1_ref, pe_ref, w1f_ref, w2_ref, bd_ref, kg_ref, is_key, n_chunk):
    hid0 = _dot(pe_ref[...], w1f_ref[...])[0:1]
    low = lax.broadcasted_iota(jnp.int32, (n_chunk, LANES), 1) < HEAD_DIM
    accs = [jnp.zeros((n_chunk, 2 * CMP_HIDDEN), F32) for _ in range(2)]
    for jp in range(CMP_STRIDE // 2):
        x0 = x_ref[pl.ds(2 * jp, n_chunk, stride=CMP_STRIDE), :]
        x1 = x_ref[pl.ds(2 * jp + 1, n_chunk, stride=CMP_STRIDE), :]
        pair = [jnp.where(low, x0, pltpu.roll(x1, HEAD_DIM, axis=1)),
                jnp.where(low, pltpu.roll(x0, HEAD_DIM, axis=1), x1)]
        for a in range(2):
            accs[a] = accs[a] + _dot(pair[a].astype(BF16), w1_ref[jp])
    out = jnp.zeros((n_chunk, LANES), F32)
    for a in range(2):
        p0 = accs[a][:, :CMP_HIDDEN]
        p1 = accs[a][:, CMP_HIDDEN:]
        hid = hid0 + p0 + pltpu.roll(p1, n_chunk - 1, axis=0)
        out = out + _dot(jax.nn.silu(hid).astype(BF16), w2_ref[a])
    normed = _head_norm(out, bd_ref, kg_ref[...])
    return jnp.where(is_key, normed, out)


def _compress_prompt_kernel(x_ref, w1_ref, pe_ref, w1f_ref, w2_ref, bd_ref, kg_ref, o_ref, ot_ref, *, n_chunk):
    is_key = pl.program_id(1) == 0
    out = _compress_math(x_ref, w1_ref, pe_ref, w1f_ref, w2_ref, bd_ref, kg_ref, is_key, n_chunk)
    o_ref[...] = out
    ot_ref[...] = out.T


def _compress_weights(pe, w1, w2, k_gain):
    w1r = w1.reshape(2, CMP_RATIO, CMP_STRIDE, HEAD_DIM, CMP_HIDDEN)
    wj = jnp.concatenate([w1r[:, 0], w1r[:, 1]], axis=-1)
    w1bd = wj.reshape(2, CMP_STRIDE // 2, 2 * HEAD_DIM, 2 * CMP_HIDDEN).astype(BF16)
    pe8 = jnp.concatenate([pe.reshape(2, 1, CMP_LEN * HEAD_DIM),
                           jnp.zeros((2, 7, CMP_LEN * HEAD_DIM), F32)], axis=1).astype(BF16)
    w1f = w1.reshape(2, CMP_LEN * HEAD_DIM, CMP_HIDDEN).astype(BF16)
    z2 = jnp.zeros_like(w2)
    w2pad = jnp.stack([jnp.concatenate([w2, z2], axis=-1),
                       jnp.concatenate([z2, w2], axis=-1)], axis=1).astype(BF16)
    hid = jnp.arange(LANES) // HEAD_DIM
    bd = ((hid[:, None] == hid[None, :]).astype(F32) / HEAD_DIM).astype(BF16)
    kg = jnp.tile(k_gain.astype(F32), 2)[None]
    return w1bd, pe8, w1f, w2pad, bd, kg


def _cw_specs(nd):
    def sp(shape, fn):
        return pl.BlockSpec(shape, fn)
    if nd == 3:
        return [
            sp((None, CMP_STRIDE // 2, LANES, 2 * CMP_HIDDEN), lambda b, t, p: (t, 0, 0, 0)),
            sp((None, 8, CMP_LEN * HEAD_DIM), lambda b, t, p: (t, 0, 0)),
            sp((None, CMP_LEN * HEAD_DIM, CMP_HIDDEN), lambda b, t, p: (t, 0, 0)),
            sp((None, 2, CMP_HIDDEN, LANES), lambda b, t, p: (t, 0, 0, 0)),
            sp((LANES, LANES), lambda b, t, p: (0, 0)),
            sp((1, LANES), lambda b, t, p: (0, 0)),
        ]
    return [
        sp((None, CMP_STRIDE // 2, LANES, 2 * CMP_HIDDEN), lambda b, t, p, *_: (t, 0, 0, 0)),
        sp((None, 8, CMP_LEN * HEAD_DIM), lambda b, t, p, *_: (t, 0, 0)),
        sp((None, CMP_LEN * HEAD_DIM, CMP_HIDDEN), lambda b, t, p, *_: (t, 0, 0)),
        sp((None, 2, CMP_HIDDEN, LANES), lambda b, t, p, *_: (t, 0, 0, 0)),
        sp((LANES, LANES), lambda b, t, p, *_: (0, 0)),
        sp((1, LANES), lambda b, t, p, *_: (0, 0)),
    ]


def _compress_prompt(u, col0, cw):
    B, S, W = u.shape
    n_chunk = S // CMP_STRIDE
    base = col0 // LANES
    return pl.pallas_call(
        functools.partial(_compress_prompt_kernel, n_chunk=n_chunk),
        grid=(B, 2, 2),
        in_specs=[pl.BlockSpec((None, S, LANES), lambda b, t, p: (b, 0, base + 2 * t + p))] + _cw_specs(3),
        out_specs=[pl.BlockSpec((None, None, n_chunk, LANES), lambda b, t, p: (b, t, 0, p)),
                   pl.BlockSpec((None, None, LANES, n_chunk), lambda b, t, p: (b, t, p, 0))],
        out_shape=[jax.ShapeDtypeStruct((B, 2, n_chunk, KVW_B), F32),
                   jax.ShapeDtypeStruct((B, 2, KVW_B, n_chunk), F32)],
        compiler_params=_cparams(("parallel", "parallel", "parallel")),
    )(u, *cw)


def _compress_sample_kernel(pt_ref, pool_ref, w1_ref, pe_ref, w1f_ref, w2_ref, bd_ref, kg_ref, o_ref,
                            raw, xbuf, sem, *, n_pages, n_chunk):
    n, t, p = pl.program_id(0), pl.program_id(1), pl.program_id(2)
    step = (n * 2 + t) * 2 + p
    n_steps = pl.num_programs(0) * 4

    def page_copy(page, rt, pair, slot, pg):
        row0 = pl.multiple_of(pair * LANES, LANES)
        return pltpu.make_async_copy(pool_ref.at[page, rt, pl.ds(row0, LANES), :], raw.at[slot, pg], sem.at[slot])

    def issue(st, slot):
        seq, rt, pair = st // 4, (st // 2) % 2, st % 2

        def go(pg, c):
            page_copy(pt_ref[seq * n_pages + pg], rt, pair, slot, pg).start()
            return c
        lax.fori_loop(0, n_pages, go, 0, unroll=8)

    @pl.when(step == 0)
    def _():
        issue(step, 0)

    @pl.when(step + 1 < n_steps)
    def _():
        issue(step + 1, (step + 1) % 2)

    slot = step % 2

    def wait(pg, c):
        page_copy(0, t, p, slot, pg).wait()
        return c
    lax.fori_loop(0, n_pages, wait, 0, unroll=8)

    def to_token_major(pg, c):
        xbuf[pl.ds(pl.multiple_of(pg * PAGE_SIZE, PAGE_SIZE), PAGE_SIZE), :] = raw[slot, pg].T
        return c
    lax.fori_loop(0, n_pages, to_token_major, 0, unroll=8)

    o_ref[...] = _compress_math(xbuf, w1_ref, pe_ref, w1f_ref, w2_ref, bd_ref, kg_ref, t == 0, n_chunk)


def _compress_sample(pool, page_table, cw):
    N, n_pages = page_table.shape
    n_chunk = n_pages * PAGE_SIZE // CMP_STRIDE
    gs = pltpu.PrefetchScalarGridSpec(
        num_scalar_prefetch=1,
        grid=(N, 2, 2),
        in_specs=[pl.BlockSpec(memory_space=pl.ANY)] + _cw_specs(4),
        out_specs=pl.BlockSpec((None, None, n_chunk, LANES), lambda b, t, p, *_: (b, t, 0, p)),
        scratch_shapes=[pltpu.VMEM((2, n_pages, LANES, PAGE_SIZE), F32),
                        pltpu.VMEM((n_pages * PAGE_SIZE, LANES), F32), pltpu.SemaphoreType.DMA((2,))],
    )
    return pl.pallas_call(
        functools.partial(_compress_sample_kernel, n_pages=n_pages, n_chunk=n_chunk),
        grid_spec=gs,
        out_shape=jax.ShapeDtypeStruct((N, 2, n_chunk, KVW_B), F32),
        compiler_params=_cparams(("arbitrary", "arbitrary", "arbitrary")),
    )(page_table.reshape(-1), pool, *cw)


def _cmp_kernel(q_ref, kc_ref, vc_ref, cov_ref, g_ref, o_ref, sel_ref, idx_ref, *,
                tq, n_chunk, nselp, pos_base, top_n):
    nh, gqa = 2 * GQA_B, GQA_B
    i = pl.program_id(2)
    R = nh * tq
    Q = _prep_q(q_ref[...], nh, gqa, tq)
    kc = kc_ref[...].astype(BF16)
    vc = vc_ref[...].astype(BF16)
    s = _nt_dot(Q, kc)
    pos_r = pos_base + i * tq + (lax.broadcasted_iota(jnp.int32, (R, n_chunk), 0) & (tq - 1))
    cend = lax.broadcasted_iota(jnp.int32, (R, n_chunk), 1) * CMP_STRIDE + (CMP_LEN - 1)
    s = jnp.where(cend <= pos_r, s, -jnp.inf)
    m = jnp.max(s, axis=-1, keepdims=True)
    m = jnp.where(m > -jnp.inf, m, 0.0)
    e = jnp.exp(s - m)
    den = jnp.sum(e, axis=-1, keepdims=True)
    p = e / jnp.maximum(den, TINY)
    o = _dot(p.astype(BF16), vc)
    o = _apply_gates(o, g_ref, nh, tq)
    o_ref[...] = _unprep_o(o, nh, gqa, tq)

    imps = []
    for a in range(2):
        ps = p[(a * gqa) * tq:(a * gqa + 1) * tq]
        for g in range(1, gqa):
            ps = ps + p[(a * gqa + g) * tq:(a * gqa + g + 1) * tq]
        hi = ps.astype(BF16)
        lo = (ps - hi.astype(F32)).astype(BF16)
        imps.append(_nt_dot(cov_ref[...], hi) + _nt_dot(cov_ref[...], lo))
    _select_blocks(imps, sel_ref, idx_ref, pos_base + i * tq, tq, nselp, top_n)


def _select_blocks(imps, sel_ref, idx_ref, pos0, tq, nselp, top_n):
    blk = lax.broadcasted_iota(jnp.int32, (nselp, tq), 0)
    cur = (pos0 + lax.broadcasted_iota(jnp.int32, (1, tq), 1)) // SEL_BLOCK
    valid = blk <= cur
    forced = (blk == 0) | (blk == cur) | (blk == cur - 1)
    n_forced = jnp.sum(forced.astype(jnp.int32), axis=0, keepdims=True)
    zero_row = jnp.zeros((1, tq), jnp.int32)
    for a, imp in enumerate(imps):
        rem = jnp.where(valid & jnp.logical_not(forced), imp, -1.0)
        sel = forced
        idx_ref[a, pl.ds(0, 1), :] = zero_row
        idx_ref[a, pl.ds(1, 1), :] = cur
        idx_ref[a, pl.ds(2, 1), :] = jnp.maximum(cur - 1, 0)
        for it in range(top_n - 1):
            mx = jnp.max(rem, axis=0, keepdims=True)
            first = jnp.min(jnp.where(rem == mx, blk, nselp), axis=0, keepdims=True)
            active = (it < top_n - n_forced) & (mx >= 0.0)
            pick = (blk == first) & active
            sel = sel | pick
            rem = jnp.where(pick, -1.0, rem)
            idx_ref[a, pl.ds(3 + it, 1), :] = jnp.where(active, first, 0)
        for r in range(3 + top_n - 1, IDX_ROWS):
            idx_ref[a, pl.ds(r, 1), :] = zero_row
        sel_ref[a] = sel.astype(F32)


def _cmp_t_kernel(q_ref, kc_ref, vct_ref, cov_ref, g_ref, o_ref, sel_ref, idx_ref, *,
                  tq, n_chunk, nselp, pos_base, top_n):
    nh, gqa = 2 * GQA_B, GQA_B
    i = pl.program_id(2)
    qt = _q_transposed(q_ref[...], nh, gqa, tq)
    s = _dot(kc_ref[...].astype(BF16), qt)
    pos_l = pos_base + i * tq + lax.broadcasted_iota(jnp.int32, (n_chunk, tq), 1)
    cend = lax.broadcasted_iota(jnp.int32, (n_chunk, tq), 0) * CMP_STRIDE + (CMP_LEN - 1)
    ok = cend <= pos_l
    s = jnp.concatenate([jnp.where(ok, s[:, j * tq:(j + 1) * tq], -jnp.inf) for j in range(nh)], axis=1)
    m = jnp.max(s, axis=0, keepdims=True)
    m = jnp.where(m > -jnp.inf, m, 0.0)
    e = jnp.exp(s - m)
    den = jnp.sum(e, axis=0, keepdims=True)
    p = e / jnp.maximum(den, TINY)
    acc = _dot(vct_ref[...].astype(BF16), p.astype(BF16))
    g_t = g_ref[...].T
    parts = []
    for j in range(nh):
        half = (j // gqa) % 2
        parts.append(acc[half * HEAD_DIM:(half + 1) * HEAD_DIM, j * tq:(j + 1) * tq] * g_t[j:j + 1, :])
    o_ref[...] = jnp.concatenate(parts, axis=0).T

    imps = []
    for a in range(2):
        ps = p[:, (a * gqa) * tq:(a * gqa + 1) * tq]
        for g in range(1, gqa):
            ps = ps + p[:, (a * gqa + g) * tq:(a * gqa + g + 1) * tq]
        hi = ps.astype(BF16)
        lo = (ps - hi.astype(F32)).astype(BF16)
        imps.append(_dot(cov_ref[...], hi) + _dot(cov_ref[...], lo))
    _select_blocks(imps, sel_ref, idx_ref, pos_base + i * tq, tq, nselp, top_n)


def _cover_t(n_chunk, n_cmp, n_sel, nselp):
    c = jnp.arange(n_chunk)[None, :]
    j = jnp.arange(nselp)[:, None]
    cov = ((c * CMP_STRIDE <= j * SEL_BLOCK + SEL_BLOCK - 1) & (c * CMP_STRIDE + CMP_LEN - 1 >= j * SEL_BLOCK)
           & (c < n_cmp) & (j < n_sel))
    return cov.astype(BF16)


def _cmp_attention(q, q_off, kvc, gates, g_off, *, tq, pos_base, n_keys, kvc_t=None):
    N, Tq, W = q.shape
    n_chunk = kvc.shape[2]
    n_cmp = (n_keys - CMP_LEN) // CMP_STRIDE + 1
    n_sel = -(-n_keys // SEL_BLOCK)
    nselp = -(-n_sel // 8) * 8
    top_n = min(SEL_TOPN, n_sel)
    qs = 2 * GQA_B * HEAD_DIM
    Wg = gates.shape[-1]
    cov = _cover_t(n_chunk, n_cmp, n_sel, nselp)
    if kvc_t is None:
        body, values = _cmp_kernel, kvc
        v_spec = pl.BlockSpec((None, None, n_chunk, LANES), lambda b, p, i: (b, 1, 0, p))
    else:
        body, values = _cmp_t_kernel, kvc_t
        v_spec = pl.BlockSpec((None, None, LANES, n_chunk), lambda b, p, i: (b, 1, p, 0))
    return pl.pallas_call(
        functools.partial(body, tq=tq, n_chunk=n_chunk, nselp=nselp, pos_base=pos_base, top_n=top_n),
        grid=(N, 2, Tq // tq),
        in_specs=[
            pl.BlockSpec((None, tq, qs), lambda b, p, i: (b, i, q_off // qs + p)),
            pl.BlockSpec((None, None, n_chunk, LANES), lambda b, p, i: (b, 0, 0, p)),
            v_spec,
            pl.BlockSpec((nselp, n_chunk), lambda b, p, i: (0, 0)),
            pl.BlockSpec((None, tq, LANES), lambda b, p, i: (b, i, g_off // LANES + p)),
        ],
        out_specs=[
            pl.BlockSpec((None, tq, qs), lambda b, p, i: (b, i, p)),
            pl.BlockSpec((None, 2, nselp, tq), lambda b, p, i: (b, p, 0, i)),
            pl.BlockSpec((None, 2, IDX_ROWS, tq), lambda b, p, i: (b, p, 0, i)),
        ],
        out_shape=[
            jax.ShapeDtypeStruct((N, Tq, WIDTH_B), F32),
            jax.ShapeDtypeStruct((N, KV_HEADS_B, nselp, Tq), F32),
            jax.ShapeDtypeStruct((N, KV_HEADS_B, IDX_ROWS, Tq), jnp.int32),
        ],
        compiler_params=_cparams(("parallel", "parallel", "parallel")),
    )(q, kvc, values, cov, gates)


def _sel_prompt_kernel(q_ref, k_ref, vt_ref, sel_ref, g_ref, o_ref, m_ref, acc_ref, *, tq, tk):
    nh, gqa, n_sl = 2 * GQA_B, GQA_B, KV_HEADS_B // 2
    i = pl.program_id(1)
    q0 = i * tq
    qts = _slab_queries(q_ref, n_sl, nh, gqa, tq)
    _init_stats(m_ref, acc_ref)
    nblk = tk // SEL_BLOCK
    rel = lax.broadcasted_iota(jnp.int32, (tk, tq), 1) - lax.broadcasted_iota(jnp.int32, (tk, tq), 0)

    def body(jj, carry):
        ks = pl.multiple_of(jj * tk, tk)
        kt = k_ref[pl.ds(ks, tk), :].astype(BF16)
        vt_t = vt_ref[:, pl.ds(ks, tk)].astype(BF16)
        causal = rel + (q0 - ks) >= 0
        masks = []
        for kvh in range(KV_HEADS_B):
            rows = sel_ref[kvh, pl.ds(pl.multiple_of(jj * nblk, nblk), nblk), :]
            chosen = jnp.concatenate(
                [jnp.broadcast_to(rows[c:c + 1], (SEL_BLOCK, tq)) for c in range(nblk)], axis=0)
            masks += [(chosen > 0.5) & causal] * gqa
        _attend_tile_t(kt, vt_t, qts, masks, tq, m_ref, acc_ref)
        return carry

    lax.fori_loop(0, (q0 + tq - 1) // tk + 1, body, 0)
    _finish_t(o_ref, None, g_ref, m_ref, acc_ref, nh, gqa, tq)


def _sel_prompt(u, vt, sel_t, *, q_off, k_off, g_off, tq=128, tk=512):
    B, S, W = u.shape
    n_sl = KV_HEADS_B // 2
    nselp = sel_t.shape[2]
    tq = min(tq, S)
    tk = min(tk, S)
    assert tk % SEL_BLOCK == 0 and (tk // SEL_BLOCK) % 8 == 0 and S % tk == 0 and nselp * SEL_BLOCK >= S
    assert q_off % WIDTH_B == 0 and k_off % KVW_B == 0 and g_off % KVW_B == 0
    R = 2 * GQA_B * tq
    return pl.pallas_call(
        functools.partial(_sel_prompt_kernel, tq=tq, tk=tk),
        grid=(B, S // tq),
        in_specs=[
            pl.BlockSpec((None, tq, WIDTH_B), lambda b, i: (b, i, q_off // WIDTH_B)),
            pl.BlockSpec((None, S, KVW_B), lambda b, i: (b, 0, k_off // KVW_B)),
            pl.BlockSpec((None, None, KVW_B, S), lambda b, i: (b, 0, 0, 0)),
            pl.BlockSpec((None, KV_HEADS_B, nselp, tq), lambda b, i: (b, 0, 0, i)),
            pl.BlockSpec((None, tq, KVW_B), lambda b, i: (b, i, g_off // KVW_B)),
        ],
        out_specs=pl.BlockSpec((None, tq, WIDTH_B), lambda b, i: (b, i, 0)),
        out_shape=jax.ShapeDtypeStruct((B, S, WIDTH_B), F32),
        scratch_shapes=[pltpu.VMEM((1, n_sl * R), F32), pltpu.VMEM((n_sl, ACC_ROWS, R), F32)],
        compiler_params=_cparams(("parallel", "arbitrary")),
    )(u, u, vt, sel_t, u)


def _sel_sample_kernel(pt_ref, idx_ref, q_ref, kn_ref, vn_ref, g_ref, pool_ref, o_ref, kvbuf, sem, *,
                       nt, n_pages, n_pick):
    nh, gqa, tq = 2 * GQA_B, GQA_B, SAMPLE_ROWS
    n, p, a = pl.program_id(0), pl.program_id(1), pl.program_id(2)
    step = (n * 2 + p) * 2 + a
    n_steps = pl.num_programs(0) * 4
    blocks_per_page = PAGE_SIZE // SEL_BLOCK
    width = n_pick * PAGE_SIZE

    def picked_block(seq, pair, head, t, r):
        src_row = jnp.where(r == 0, 0, r + 1)
        return idx_ref[((seq * KV_HEADS_B + pair * 2 + head) * IDX_ROWS + src_row) * tq + t]

    def page_copy(st, c, lookup):
        seq, pair, head = st // 4, (st // 2) % 2, st % 2
        t, r = c // n_pick, c % n_pick
        page = pt_ref[seq * n_pages + picked_block(seq, pair, head, t, r) // blocks_per_page] if lookup else 0
        dst0 = pl.multiple_of(r * PAGE_SIZE, PAGE_SIZE)
        half0 = pl.multiple_of(head * HEAD_DIM, HEAD_DIM)
        row0 = pl.multiple_of(pair * LANES + head * HEAD_DIM, HEAD_DIM)
        return pltpu.make_async_copy(
            pool_ref.at[page, pl.ds(2, 2), pl.ds(row0, HEAD_DIM), :],
            kvbuf.at[head, t, :, pl.ds(half0, HEAD_DIM), pl.ds(dst0, PAGE_SIZE)], sem.at[head])

    def issue(st):
        def go(c, carry):
            page_copy(st, c, True).start()
            return carry
        lax.fori_loop(0, nt * n_pick, go, 0, unroll=4)

    @pl.when(step == 0)
    def _():
        kvbuf[0, :, :, HEAD_DIM:, :] = jnp.zeros((nt, 2, HEAD_DIM, width), F32)
        kvbuf[1, :, :, :HEAD_DIM, :] = jnp.zeros((nt, 2, HEAD_DIM, width), F32)
        issue(step)

    @pl.when(step + 1 < n_steps)
    def _():
        issue(step + 1)

    slot = a

    def wait(c, carry):
        page_copy(step, c, False).wait()
        return carry
    lax.fori_loop(0, nt * n_pick, wait, 0, unroll=4)

    R = nh * tq
    Q = _prep_q(q_ref[...], nh, gqa, tq)
    pad = jnp.zeros((LANES - tq, LANES), F32)
    kn = jnp.concatenate([kn_ref[...], pad], axis=0).astype(BF16)
    vn = jnp.concatenate([vn_ref[...], pad], axis=0).astype(BF16)
    row = lax.broadcasted_iota(jnp.int32, (R, LANES), 0)
    colk = lax.broadcasted_iota(jnp.int32, (R, LANES), 1)
    trow = row & (tq - 1)
    s_new = jnp.where((colk <= trow) & (colk < nt), _nt_dot(Q, kn), NEG)
    lane = lax.broadcasted_iota(jnp.int32, (1, width), 1)
    lane_pick = lane // PAGE_SIZE
    lane_half = (lane // SEL_BLOCK) % blocks_per_page
    o = jnp.zeros((R, LANES), F32)
    for t in range(nt):
        chosen = lane < 0
        for r in range(n_pick):
            half = picked_block(n, p, a, t, r) % blocks_per_page
            chosen = chosen | ((lane_pick == r) & (lane_half == half))
        s = jnp.where(chosen, _dot(Q, kvbuf[slot, t, 0].astype(BF16)), NEG)
        m = jnp.maximum(jnp.max(s, axis=-1, keepdims=True), jnp.max(s_new, axis=-1, keepdims=True))
        p1 = jnp.exp(s - m)
        p2 = jnp.exp(s_new - m)
        l = jnp.sum(p1, axis=-1, keepdims=True) + jnp.sum(p2, axis=-1, keepdims=True)
        o_t = (_nt_dot(p1.astype(BF16), kvbuf[slot, t, 1].astype(BF16)) + _dot(p2.astype(BF16), vn)) / l
        o = jnp.where(trow == t, o_t, o)

    g = g_ref[...]
    low = lax.broadcasted_iota(jnp.int32, (tq, LANES), 1) < HEAD_DIM
    first = a == 0
    placed = []
    for jj in range(gqa):
        oj = jnp.where(first, o[jj * tq:(jj + 1) * tq], o[(gqa + jj) * tq:(gqa + jj + 1) * tq])
        gj = jnp.where(first, g[:, jj:jj + 1], g[:, gqa + jj:gqa + jj + 1])
        oj = oj * gj
        placed.append(jnp.where(a == jj % 2, oj, pltpu.roll(oj, HEAD_DIM, axis=1)))
    o_ref[...] = jnp.concatenate([jnp.where(low, placed[0], placed[1]),
                                  jnp.where(low, placed[2], placed[3])], axis=1)


def _sel_sample(us, idx, pool, page_table, *, nt, q_off, k_off, v_off, g_off):
    N, tq, W = us.shape
    n_pages = page_table.shape[1]
    n_pick = SEL_TOPN - 1
    qs = 2 * GQA_B * HEAD_DIM
    hw = GQA_B * HEAD_DIM
    gs = pltpu.PrefetchScalarGridSpec(
        num_scalar_prefetch=2,
        grid=(N, 2, 2),
        in_specs=[
            pl.BlockSpec((None, tq, qs), lambda b, p, a, *_: (b, 0, q_off // qs + p)),
            pl.BlockSpec((None, tq, LANES), lambda b, p, a, *_: (b, 0, k_off // LANES + p)),
            pl.BlockSpec((None, tq, LANES), lambda b, p, a, *_: (b, 0, v_off // LANES + p)),
            pl.BlockSpec((None, tq, LANES), lambda b, p, a, *_: (b, 0, g_off // LANES + p)),
            pl.BlockSpec(memory_space=pl.ANY),
        ],
        out_specs=pl.BlockSpec((None, tq, hw), lambda b, p, a, *_: (b, 0, 2 * p + a)),
        scratch_shapes=[
            pltpu.VMEM((2, nt, 2, LANES, n_pick * PAGE_SIZE), F32),
            pltpu.SemaphoreType.DMA((2,)),
        ],
    )
    return pl.pallas_call(
        functools.partial(_sel_sample_kernel, nt=nt, n_pages=n_pages, n_pick=n_pick),
        grid_spec=gs,
        out_shape=jax.ShapeDtypeStruct((N, tq, WIDTH_B), F32),
        compiler_params=_cparams(("arbitrary", "arbitrary", "arbitrary")),
    )(page_table.reshape(-1), idx.reshape(-1), us, us, us, us, pool)


def _cache_attn_kernel(*refs, n_sl, nh, gqa, L, dil, win, nt, want_lse, gated):
    q_ref, kc_ref, vc_ref, kn_ref, vn_ref = refs[:5]
    pos = 5
    g_ref = None
    if gated:
        g_ref = refs[pos]
        pos += 1
    o_ref = refs[pos]
    lse_ref = refs[pos + 1] if want_lse else None
    tq = SAMPLE_ROWS
    R = nh * tq
    qs = nh * HEAD_DIM
    pad = jnp.zeros((LANES - tq, LANES), F32)
    t1 = lax.broadcasted_iota(jnp.int32, (R, L), 0) & (tq - 1)
    d1 = L + t1 - lax.broadcasted_iota(jnp.int32, (R, L), 1)
    ok1 = ((d1 & (dil - 1)) == 0) & (d1 <= win)
    t2 = lax.broadcasted_iota(jnp.int32, (R, LANES), 0) & (tq - 1)
    c2 = lax.broadcasted_iota(jnp.int32, (R, LANES), 1)
    d2 = t2 - c2
    ok2 = (d2 >= 0) & ((d2 & (dil - 1)) == 0) & (d2 <= win) & (c2 < nt)
    for sl in range(n_sl):
        rows = slice(sl * LANES, (sl + 1) * LANES)
        Q = _prep_q(q_ref[:, sl * qs:(sl + 1) * qs], nh, gqa, tq)
        kc_t = kc_ref[rows, :].astype(BF16)
        vc_t = vc_ref[rows, :].astype(BF16)
        kn = jnp.concatenate([kn_ref[:, rows], pad], axis=0).astype(BF16)
        vn = jnp.concatenate([vn_ref[:, rows], pad], axis=0).astype(BF16)
        s1 = jnp.where(ok1, _dot(Q, kc_t), NEG)
        s2 = jnp.where(ok2, _nt_dot(Q, kn), NEG)
        m = jnp.maximum(jnp.max(s1, axis=-1, keepdims=True), jnp.max(s2, axis=-1, keepdims=True))
        p1 = jnp.exp(s1 - m)
        p2 = jnp.exp(s2 - m)
        l = jnp.sum(p1, axis=-1, keepdims=True) + jnp.sum(p2, axis=-1, keepdims=True)
        o = (_nt_dot(p1.astype(BF16), vc_t) + _dot(p2.astype(BF16), vn)) / l
        if g_ref is not None:
            g = g_ref[:, rows]
            o = jnp.concatenate([o[j * tq:(j + 1) * tq] * g[:, j:j + 1] for j in range(nh)], axis=0)
        o_ref[:, sl * qs:(sl + 1) * qs] = _unprep_o(o, nh, gqa, tq)
        if want_lse:
            lse_ref[:, sl * qs:(sl + 1) * qs] = _unprep_o(jnp.broadcast_to(m + jnp.log(l), (R, LANES)),
                                                          nh, gqa, tq)


def _feature_major(x):
    nd = x.ndim
    xt = jnp.transpose(x, (0, 1) + tuple(range(3, nd)) + (2,))
    return xt.reshape(x.shape[0], x.shape[1], -1, x.shape[2])


def _cache_attention(us, cache_t, layer, *, q_off, k_off, v_off, nh, gqa, n_slabs, dil, win, nt, want_lse,
                     g_off=None):
    N, tq, W = us.shape
    L = cache_t.shape[3]
    cache = cache_t
    wo = n_slabs * nh * HEAD_DIM
    kwid = n_slabs * LANES
    assert q_off % wo == 0 and k_off % kwid == 0 and v_off % kwid == 0
    in_specs = [
        pl.BlockSpec((None, tq, wo), lambda b: (b, 0, q_off // wo)),
        pl.BlockSpec((None, None, kwid, L), lambda b: (layer, b, 0, 0)),
        pl.BlockSpec((None, None, kwid, L), lambda b: (layer, b, 1, 0)),
        pl.BlockSpec((None, tq, kwid), lambda b: (b, 0, k_off // kwid)),
        pl.BlockSpec((None, tq, kwid), lambda b: (b, 0, v_off // kwid)),
    ]
    args = [us, cache, cache, us, us]
    if g_off is not None:
        assert g_off % kwid == 0
        in_specs.append(pl.BlockSpec((None, tq, kwid), lambda b: (b, 0, g_off // kwid)))
        args.append(us)
    o_spec = pl.BlockSpec((None, tq, wo), lambda b: (b, 0, 0))
    o_shape = jax.ShapeDtypeStruct((N, tq, wo), F32)
    return pl.pallas_call(
        functools.partial(_cache_attn_kernel, n_sl=n_slabs, nh=nh, gqa=gqa, L=L, dil=dil, win=win, nt=nt,
                          want_lse=want_lse, gated=g_off is not None),
        grid=(N,),
        in_specs=in_specs,
        out_specs=[o_spec, o_spec] if want_lse else o_spec,
        out_shape=[o_shape, o_shape] if want_lse else o_shape,
        compiler_params=_cparams(("parallel",)),
    )(*args)


def _merge_a_kernel(o0, o1, o2, l0, l1, l2, z_ref, x_ref, w_ref, y_ref):
    a0, a1, a2 = l0[...], l1[...], l2[...]
    mx = jnp.maximum(jnp.maximum(a0, a1), a2)
    e0, e1, e2 = jnp.exp(a0 - mx), jnp.exp(a1 - mx), jnp.exp(a2 - mx)
    den = e0 + e1 + e2
    o = (e0 / den) * o0[...] + (e1 / den) * o1[...] + (e2 / den) * o2[...]
    y_ref[...] = x_ref[...] + _dot((o * jax.nn.silu(z_ref[...])).astype(BF16), w_ref[...])


def _merge_a(outs, lses, u, z_off, x, w_out_bf16):
    T, D = x.shape
    tm = min(T, 512)
    wa = WIDTH_A
    row = pl.BlockSpec((tm, wa), lambda i: (i, 0))
    return pl.pallas_call(
        _merge_a_kernel,
        grid=(T // tm,),
        in_specs=[row] * 6 + [
            pl.BlockSpec((tm, wa), lambda i: (i, z_off // wa)),
            pl.BlockSpec((tm, D), lambda i: (i, 0)),
            pl.BlockSpec((wa, D), lambda i: (0, 0)),
        ],
        out_specs=pl.BlockSpec((tm, D), lambda i: (i, 0)),
        out_shape=jax.ShapeDtypeStruct((T, D), F32),
        compiler_params=_cparams(("parallel",)),
    )(*outs, *lses, u, x, w_out_bf16)


def _merge_b_kernel(oc, os_, ow, z_ref, x_ref, w_ref, y_ref):
    o = oc[...] + os_[...] + ow[...]
    y_ref[...] = x_ref[...] + _dot((o * jax.nn.silu(z_ref[...])).astype(BF16), w_ref[...])


def _merge_b(o_c, o_s, o_w, u, z_off, x, w_out_bf16):
    T, D = x.shape
    tm = min(T, 512)
    wb = WIDTH_B
    row = pl.BlockSpec((tm, wb), lambda i: (i, 0))
    return pl.pallas_call(
        _merge_b_kernel,
        grid=(T // tm,),
        in_specs=[row] * 3 + [
            pl.BlockSpec((tm, wb), lambda i: (i, z_off // wb)),
            pl.BlockSpec((tm, D), lambda i: (i, 0)),
            pl.BlockSpec((wb, D), lambda i: (0, 0)),
        ],
        out_specs=pl.BlockSpec((tm, D), lambda i: (i, 0)),
        out_shape=jax.ShapeDtypeStruct((T, D), F32),
        compiler_params=_cparams(("parallel",)),
    )(o_c, o_s, o_w, u, x, w_out_bf16)


A_KINDS = ([EP_ROPE_Q] * (N_GROUPS_A * WIDTH_A // PROJ_TN) + [EP_ROPE_K] * (N_GROUPS_A * WIDTH_A // PROJ_TN)
           + [EP_NONE] * (N_GROUPS_A * WIDTH_A // PROJ_TN) + [EP_NONE] * (WIDTH_A // PROJ_TN))
A_K, A_V, A_Z = N_GROUPS_A * WIDTH_A, 2 * N_GROUPS_A * WIDTH_A, QKV_A

B_KINDS = ([EP_NORM_Q] * 4 + [EP_ROPE_Q] * 4 + [EP_NONE] * 4
           + [EP_NORM_K, EP_NONE, EP_ROPE_K, EP_NONE, EP_ROPE_K, EP_NONE] + [EP_SIG] * 3 + [EP_NONE])
B_PAD = len(B_KINDS) * PROJ_TN - B_COLS


def _b_weight(w_in):
    wq = w_in[:, :WIDTH_B]
    o1 = WIDTH_B + 6 * KVW_B
    wkv = w_in[:, WIDTH_B:o1]
    wg = w_in[:, o1:o1 + 3 * HEADS_B]
    wz = w_in[:, o1 + 3 * HEADS_B:]
    nh = 2 * GQA_B
    tiles = []
    for b in range(3):
        for p in range(2):
            g = wg[:, b * HEADS_B + p * nh: b * HEADS_B + (p + 1) * nh]
            tiles.append(jnp.pad(g, ((0, 0), (0, LANES - nh))))
    tiles.append(jnp.zeros((w_in.shape[0], B_PAD), w_in.dtype))
    return jnp.concatenate([wq, wq, wz, wkv] + tiles, axis=1).astype(BF16)


def _pad_rows(u, N, T):
    return jnp.pad(u.reshape(N, T, -1), ((0, 0), (0, SAMPLE_ROWS - T), (0, 0)))


def _layer_a_prompt(x, norm_g, w_bf16, q_gain, k_gain, w_out_bf16, tabs):
    B, S, D = x.shape
    u = _project(x.reshape(B * S, D), norm_g, w_bf16, A_KINDS, tabs[0], tabs[1], q_gain, k_gain)
    u3 = u.reshape(B, S, -1)
    outs, lses, states = [], [], []
    for g, (win, dil) in enumerate(DIL_PATTERNS):
        vt = _values_transposed(u3, A_V + g * WIDTH_A, WIDTH_A, dil)
        if dil == 1:
            uq, uk, q_off, k_off = u3, u3, g * WIDTH_A, A_K + g * WIDTH_A
        else:
            uq = u3[:, :, g * WIDTH_A:(g + 1) * WIDTH_A]
            uk = u3[:, :, A_K + g * WIDTH_A:A_K + (g + 1) * WIDTH_A]
            q_off, k_off = 0, 0
        o, lse = _band_attention(uq, vt, uk=uk, q_off=q_off, k_off=k_off,
                                 nh=2, gqa=1, n_slabs=HEADS_A // 2, kw=win // dil, dil=dil, want_lse=True,
                                 tk=128 if S // dil > 256 else 256)
        outs.append(o.reshape(B * S, WIDTH_A))
        lses.append(lse.reshape(B * S, WIDTH_A))
        w = min(win, S)
        kg = u3[:, S - w:, A_K + g * WIDTH_A: A_K + (g + 1) * WIDTH_A]
        vg = u3[:, S - w:, A_V + g * WIDTH_A: A_V + (g + 1) * WIDTH_A]
        states.append(jnp.stack([kg, vg], axis=2).reshape(B, w, 2, HEADS_A, HEAD_DIM))
    y = _merge_a(outs, lses, u, A_Z, x.reshape(B * S, D), w_out_bf16)
    return y.reshape(B, S, D), states


def _layer_a_sample(x, caches_t, layer, norm_g, w_bf16, q_gain, k_gain, w_out_bf16, tabs):
    N, T, D = x.shape
    u = _project(x.reshape(N * T, D), norm_g, w_bf16, A_KINDS, tabs[0], tabs[1], q_gain, k_gain)
    us = _pad_rows(u, N, T)
    outs, lses, new_rows = [], [], []
    for g, (win, dil) in enumerate(DIL_PATTERNS):
        o, lse = _cache_attention(us, caches_t[g], layer, q_off=g * WIDTH_A, k_off=A_K + g * WIDTH_A,
                                  v_off=A_V + g * WIDTH_A, nh=2, gqa=1, n_slabs=HEADS_A // 2,
                                  dil=dil, win=win, nt=T, want_lse=True)
        outs.append(o[:, :T].reshape(N * T, WIDTH_A))
        lses.append(lse[:, :T].reshape(N * T, WIDTH_A))
        new_rows.append(jnp.concatenate([us[:, :, A_K + g * WIDTH_A: A_K + (g + 1) * WIDTH_A],
                                         us[:, :, A_V + g * WIDTH_A: A_V + (g + 1) * WIDTH_A]], axis=2))
    y = _merge_a(outs, lses, u, A_Z, x.reshape(N * T, D), w_out_bf16)
    return y.reshape(N, T, D), new_rows


def _layer_b_prompt(x, norm_g, w_bf16, q_gain, k_gain, cw, w_out_bf16, tabs):
    B, S, D = x.shape
    u = _project(x.reshape(B * S, D), norm_g, w_bf16, B_KINDS, tabs[0], tabs[1], q_gain, k_gain)
    u3 = u.reshape(B, S, -1)
    kvc, kvc_t = _compress_prompt(u3, B_KV, cw)
    o_c, sel_t, _ = _cmp_attention(u3, B_QC, kvc, u3, B_GATE, tq=min(512, S), pos_base=0, n_keys=S,
                                   kvc_t=kvc_t)
    o_s = _sel_prompt(u3, _values_transposed(u3, B_KV + 3 * KVW_B, KVW_B, 1), sel_t, q_off=B_QR,
                      k_off=B_KV + 2 * KVW_B, g_off=B_GATE + 2 * LANES)
    o_w = _band_attention(u3, _values_transposed(u3, B_WIN + KVW_B, KVW_B, 1), q_off=B_QR, k_off=B_WIN,
                          nh=2 * GQA_B, gqa=GQA_B, n_slabs=2, kw=WIN_B, dil=1, want_lse=False, gates=u3,
                          g_off=B_GATE + 4 * LANES, tq=256, tk=256)
    y = _merge_b(o_c.reshape(B * S, -1), o_s.reshape(B * S, -1), o_w.reshape(B * S, -1), u, B_Z,
                 x.reshape(B * S, D), w_out_bf16)
    rows = u3[:, :, B_KV:B_KV + 4 * KVW_B].reshape(B, S, 4, KV_HEADS_B, HEAD_DIM)
    w = min(WIN_B, S)
    wrows = u3[:, S - w:, B_WIN:B_WIN + 2 * KVW_B].reshape(B, w, 2, KV_HEADS_B, HEAD_DIM)
    return y.reshape(B, S, D), rows, wrows


def _layer_b_sample(x, pool_t, page_table, win_t, layer, norm_g, w_bf16, q_gain, k_gain, cw, w_out_bf16, tabs):
    N, T, D = x.shape
    past_len = page_table.shape[1] * PAGE_SIZE
    assert past_len % SEL_BLOCK == 0 and T <= SEL_BLOCK and T <= SAMPLE_ROWS
    u = _project(x.reshape(N * T, D), norm_g, w_bf16, B_KINDS, tabs[0], tabs[1], q_gain, k_gain)
    us = _pad_rows(u, N, T)
    u3 = u.reshape(N, T, -1)
    kvc = _compress_sample(pool_t, page_table, cw)
    o_c, _, idx = _cmp_attention(us, B_QC, kvc, us, B_GATE, tq=SAMPLE_ROWS, pos_base=past_len,
                                 n_keys=past_len + T)
    o_s = _sel_sample(us, idx, pool_t, page_table, nt=T, q_off=B_QR, k_off=B_KV + 2 * KVW_B,
                      v_off=B_KV + 3 * KVW_B, g_off=B_GATE + 2 * LANES)
    o_w = _cache_attention(us, win_t, layer, q_off=B_QR, k_off=B_WIN, v_off=B_WIN + KVW_B,
                           nh=2 * GQA_B, gqa=GQA_B, n_slabs=2, dil=1, win=WIN_B, nt=T, want_lse=False,
                           g_off=B_GATE + 4 * LANES)
    y = _merge_b(o_c[:, :T].reshape(N * T, -1), o_s[:, :T].reshape(N * T, -1), o_w[:, :T].reshape(N * T, -1),
                 u, B_Z, x.reshape(N * T, D), w_out_bf16)
    rows = u3[:, :, B_KV:B_KV + 4 * KVW_B].reshape(N, T, 4, KV_HEADS_B, HEAD_DIM)
    return y.reshape(N, T, D), rows, us[:, :, B_WIN:B_WIN + 2 * KVW_B]


def _append_kernel(c_ref, n_ref, o_ref, *, L, nt, fb):
    rolled = pltpu.roll(c_ref[...], L - nt, axis=1)
    pad = jnp.zeros((LANES - SAMPLE_ROWS, LANES), F32)
    new = n_ref[...]
    new_t = jnp.concatenate(
        [jnp.concatenate([new[:, c * LANES:(c + 1) * LANES], pad], axis=0).T for c in range(fb // LANES)], axis=0)
    new_t = pltpu.roll(new_t, LANES - nt, axis=1)
    lane = lax.broadcasted_iota(jnp.int32, (fb, LANES), 1)
    if L > LANES:
        o_ref[:, :L - LANES] = rolled[:, :L - LANES]
    o_ref[:, L - LANES:] = jnp.where(lane >= LANES - nt, new_t, rolled[:, L - LANES:])


def _append_cache(cache, cache_t, new_rows, nt, win):
    J, N, F, L = cache_t.shape
    tail = cache.shape[3:]
    if L + nt <= win or L % LANES:
        new = new_rows[:, :, :nt].reshape((J, N, nt) + tail)
        return jnp.concatenate([cache, new], axis=2)[:, :, -min(win, L + nt):]
    assert L == win
    fb = min(F, LANES * max(1, 4096 // L))
    assert F % fb == 0
    out_t = pl.pallas_call(
        functools.partial(_append_kernel, L=L, nt=nt, fb=fb),
        grid=(J, N, F // fb),
        in_specs=[pl.BlockSpec((None, None, fb, L), lambda j, n, f: (j, n, f, 0)),
                  pl.BlockSpec((None, None, SAMPLE_ROWS, fb), lambda j, n, f: (j, n, 0, f))],
        out_specs=pl.BlockSpec((None, None, fb, L), lambda j, n, f: (j, n, f, 0)),
        out_shape=jax.ShapeDtypeStruct((J, N, F, L), F32),
        compiler_params=_cparams(("parallel", "parallel", "parallel")),
    )(cache_t, new_rows)
    nd = len(tail)
    out = out_t.reshape((J, N) + tail + (L,))
    return jnp.transpose(out, (0, 1, nd + 2) + tuple(range(2, nd + 2)))


def kernel(x_prompt, x_sample, cache_dil_0, cache_dil_1, cache_dil_2, cache_nsa_paged, cache_nsa_win,
           page_table, a_norm, a_w_in, a_q_norm, a_k_norm, a_w_out, b_norm, b_w_in, b_q_norm, b_k_norm,
           b_cmp_pe, b_cmp_w1, b_cmp_w2, b_w_out):
    dil_caches = (cache_dil_0, cache_dil_1, cache_dil_2)
    B, S, _ = x_prompt.shape
    N, T, _ = x_sample.shape
    past_len = page_table.shape[1] * PAGE_SIZE
    depth = a_norm.shape[0] + b_norm.shape[0]
    n_pool = cache_nsa_paged.shape[1]
    tabs_p = _rope_tables(jnp.tile(jnp.arange(S), B))
    tabs_s = _rope_tables(jnp.tile(past_len + jnp.arange(T), N))
    dil_t = [_feature_major(c) for c in dil_caches]
    win_t = _feature_major(cache_nsa_win)
    pool_t = _feature_major(cache_nsa_paged)
    pool_t = pool_t.reshape(-1, 4, KVW_B, PAGE_SIZE)
    xp, xs = x_prompt, x_sample
    dil_p = [[] for _ in DIL_PATTERNS]
    dil_new = [[] for _ in DIL_PATTERNS]
    rows_p, rows_s, win_p, win_new = [], [], [], []
    for layer in range(depth):
        j = layer // 2
        if layer % 2 == 0:
            w = a_w_in[j].astype(BF16)
            wo = a_w_out[j].astype(BF16)
            xp, st_p = _layer_a_prompt(xp, a_norm[j], w, a_q_norm[j], a_k_norm[j], wo, tabs_p)
            xs, new_s = _layer_a_sample(xs, dil_t, j, a_norm[j], w, a_q_norm[j], a_k_norm[j], wo, tabs_s)
            for g in range(N_GROUPS_A):
                dil_p[g].append(st_p[g])
                dil_new[g].append(new_s[g])
        else:
            w = _b_weight(b_w_in[j])
            wo = b_w_out[j].astype(BF16)
            cw = _compress_weights(b_cmp_pe[j], b_cmp_w1[j], b_cmp_w2[j], b_k_norm[j])
            xp, rp, wp = _layer_b_prompt(xp, b_norm[j], w, b_q_norm[j], b_k_norm[j], cw, wo, tabs_p)
            xs, rs, wn = _layer_b_sample(xs, pool_t, page_table + j * n_pool, win_t, j, b_norm[j], w,
                                         b_q_norm[j], b_k_norm[j], cw, wo, tabs_s)
            rows_p.append(rp)
            win_p.append(wp)
            rows_s.append(rs)
            win_new.append(wn)
    dil_s = [_append_cache(dil_caches[g], dil_t[g], jnp.stack(dil_new[g]), T, DIL_PATTERNS[g][0])
             for g in range(N_GROUPS_A)]
    win_s = _append_cache(cache_nsa_win, win_t, jnp.stack(win_new), T, WIN_B)
    return (xp, xs,
            jnp.stack(dil_p[0]), jnp.stack(dil_p[1]), jnp.stack(dil_p[2]),
            jnp.stack(rows_p), jnp.stack(win_p),
            dil_s[0], dil_s[1], dil_s[2],
            jnp.stack(rows_s), win_s)
```
